```python
import math
import jax, jax.numpy as jnp
from jax import lax
import numpy as np

D_MODEL = 2048
BATCH = 4
SEQ = 2048
DEPTH = 1
DEC_BATCH = 128
DEC_SEQ = 4
PAST_LEN = 16384
PAGE_SIZE = 128

N_MEM = 256
M_HEADS = 4
M_DH = D_MODEL // 8
M_W = M_HEADS * M_DH
MLSTM_CHUNK = 64
C_W = D_MODEL // 4
CONV_K = 31
X_HEADS = 4
X_DH = D_MODEL // 16
X_W = X_HEADS * X_DH
MIX_W = M_W + C_W + X_W
IN_COLS = 4 * M_W + 2 * M_HEADS + 2 * C_W + X_W
IN_SPLITS = (M_W, 2 * M_W, 3 * M_W, 4 * M_W, 4 * M_W + 2 * M_HEADS,
             4 * M_W + 2 * M_HEADS + C_W, 4 * M_W + 2 * M_HEADS + 2 * C_W)
P_HEADS = 8
P_NKEYS = 128
P_NEXP = P_NKEYS * P_NKEYS
P_DQ = 256
P_TOPK = 16
P_TOK_BLOCK = 128
EPS = 1e-6

kernel_name = 'hymba_mlstm_conformer_peer_step'


def rmsnorm(x, g):
    xf = x.astype(jnp.float32)
    y = xf * lax.rsqrt(jnp.mean(xf * xf, axis=-1, keepdims=True) + EPS)
    return (y * g.astype(jnp.float32)).astype(x.dtype)


def layernorm(x, g, b):
    xf = x.astype(jnp.float32)
    mu = jnp.mean(xf, axis=-1, keepdims=True)
    var = jnp.mean(jnp.square(xf - mu), axis=-1, keepdims=True)
    y = (xf - mu) * lax.rsqrt(var + EPS) * g.astype(jnp.float32) + b.astype(jnp.float32)
    return y.astype(x.dtype)


def mlstm_chunkwise(q, k, v, logi, logf, C0, n0, m0):
    B, L, H, d = q.shape
    Lc = math.gcd(L, MLSTM_CHUNK)
    nc = L // Lc
    f32 = jnp.float32

    def chunks(a):
        return jnp.moveaxis(a.astype(f32).reshape((B, nc, Lc) + a.shape[2:]), 1, 0)

    causal = jnp.tril(jnp.ones((Lc, Lc), dtype=bool))

    def step(carry, inp):
        C, n, m = carry
        qc, kc, vc, li, lf = inp
        b = jnp.cumsum(lf, axis=1)
        logD = b[:, :, None, :] - b[:, None, :, :] + li[:, None, :, :]
        logD = jnp.where(causal[None, :, :, None], logD, -jnp.inf)
        log_inter = b + m[:, None, :]
        m_t = jnp.maximum(log_inter, jnp.max(logD, axis=2))
        Dm = jnp.exp(logD - m_t[:, :, None, :])
        inter = jnp.exp(log_inter - m_t)
        s = jnp.einsum('bthd,bshd->btsh', qc, kc) * Dm
        num = jnp.einsum('btsh,bshd->bthd', s, vc) + inter[..., None] * jnp.einsum('bhvk,bthk->bthv', C, qc)
        den = jnp.sum(s, axis=2) + inter * jnp.einsum('bhk,bthk->bth', n, qc)
        h = num / jnp.maximum(jnp.abs(den), jnp.exp(-m_t))[..., None]
        m_new = m_t[:, -1]
        w_s = jnp.exp(b[:, -1:, :] - b + li - m_new[:, None, :])
        decay = jnp.exp(b[:, -1] + m - m_new)
        C_new = decay[..., None, None] * C + jnp.einsum('bsh,bshv,bshk->bhvk', w_s, vc, kc)
        n_new = decay[..., None] * n + jnp.einsum('bsh,bshk->bhk', w_s, kc)
        return (C_new, n_new, m_new), h

    carry0 = (C0.astype(f32), n0.astype(f32), m0.astype(f32))
    (C, n, m), hs = lax.scan(step, carry0, (chunks(q), chunks(k), chunks(v), chunks(logi), chunks(logf)))
    h = jnp.moveaxis(hs, 0, 1).reshape(B, L, H, d)
    return h, C, n, m


def causal_dwconv(u, past, w, b):
    xp = jnp.concatenate([past, u], axis=1)
    y = lax.conv_general_dilated(xp, w[:, None, :].astype(u.dtype), window_strides=(1,), padding='VALID',
                                 dimension_numbers=('NWC', 'WIO', 'NWC'), feature_group_count=u.shape[-1])
    return y + b.astype(u.dtype), xp[:, -(CONV_K - 1):]


def mem_kv(mem, g, w_k, w_v):
    B, N, _ = mem.shape
    mn = rmsnorm(mem, g)
    k = (mn @ w_k).reshape(B, N, X_HEADS, X_DH)
    v = (mn @ w_v).reshape(B, N, X_HEADS, X_DH)
    return k, v


def cross_attn(q, k, v):
    B, L = q.shape[:2]
    s = jnp.einsum('blhd,bnhd->bhln', q, k).astype(jnp.float32) * (X_DH ** -0.5)
    p = jax.nn.softmax(s, axis=-1)
    o = jnp.einsum('bhln,bnhd->blhd', p.astype(v.dtype), v)
    return o.reshape(B, L, X_W)


def peer(xn, wq, subkeys, u_tab, v_tab):
    B, L, D = xn.shape
    N = B * L
    T = P_TOK_BLOCK
    nb = -(-N // T)
    xf = jnp.pad(xn.reshape(N, D), ((0, nb * T - N), (0, 0))).reshape(nb, T, D)

    def block(xb):
        q = (xb @ wq).reshape(T, P_HEADS, 2, P_DQ // 2)
        s = jnp.einsum('thcd,hckd->thck', q, subkeys.astype(q.dtype)).astype(jnp.float32)
        sv, si = lax.top_k(s, P_TOPK)
        cand = sv[:, :, 0, :, None] + sv[:, :, 1, None, :]
        cidx = si[:, :, 0, :, None] * P_NKEYS + si[:, :, 1, None, :]
        cs, ci = lax.top_k(cand.reshape(T, P_HEADS, P_TOPK * P_TOPK), P_TOPK)
        eidx = jnp.take_along_axis(cidx.reshape(T, P_HEADS, P_TOPK * P_TOPK), ci, axis=-1)
        g = jax.nn.softmax(cs, axis=-1).reshape(T, P_HEADS * P_TOPK)
        eidx = eidx.reshape(T, P_HEADS * P_TOPK)
        ue = u_tab[eidx]
        ve = v_tab[eidx]
        a = jax.nn.gelu(jnp.einsum('td,ted->te', xb, ue).astype(jnp.float32))
        return jnp.einsum('te,ted->td', (g * a).astype(xb.dtype), ve)

    out = lax.map(block, xf).reshape(nb * T, D)[:N]
    return out.reshape(B, L, D)


def layer(x, mem_k, mem_v, C0, n0, m0, conv_past, p):
    (norm1_g, w_in, b_gate, mlstm_norm_g, conv_w, conv_b, conv_ln_g, conv_ln_b,
     w_out, norm2_g, peer_wq, peer_subkeys, peer_u, peer_v) = p
    B, L, _ = x.shape
    xn = rmsnorm(x, norm1_g)
    proj = xn @ w_in
    q_m, k_m, v_m, o_m, gates, a_c, g_c, q_x = jnp.split(proj, IN_SPLITS, axis=-1)

    q = q_m.reshape(B, L, M_HEADS, M_DH)
    k = k_m.reshape(B, L, M_HEADS, M_DH) * (M_DH ** -0.5)
    v = v_m.reshape(B, L, M_HEADS, M_DH)
    gates = gates.astype(jnp.float32) + b_gate.astype(jnp.float32)
    logi = gates[..., :M_HEADS]
    logf = jax.nn.log_sigmoid(gates[..., M_HEADS:])
    h, C, n, m = mlstm_chunkwise(q, k, v, logi, logf, C0, n0, m0)
    h = rmsnorm(h, mlstm_norm_g) * jax.nn.sigmoid(o_m.astype(jnp.float32)).reshape(B, L, M_HEADS, M_DH)
    h_m = h.reshape(B, L, M_W).astype(x.dtype)

    u = a_c * jax.nn.sigmoid(g_c)
    y, conv_new = causal_dwconv(u, conv_past.astype(u.dtype), conv_w, conv_b)
    y_c = jax.nn.silu(layernorm(y, conv_ln_g, conv_ln_b)).astype(x.dtype)

    o_x = cross_attn(q_x.reshape(B, L, X_HEADS, X_DH), mem_k.astype(x.dtype), mem_v.astype(x.dtype))

    x = x + jnp.concatenate([h_m, y_c, o_x], axis=-1) @ w_out
    x = x + peer(rmsnorm(x, norm2_g), peer_wq, peer_subkeys, peer_u, peer_v)
    return x, C, n, m, conv_new


def setup_inputs(seed: int = 0) -> dict:
    key = jax.random.key(seed)
    ks = jax.random.split(key, 32)
    f32 = jnp.float32

    def nrm(k, shape, s):
        return jax.random.normal(k, shape, f32) * s

    def gain(k, shape):
        return 1.0 + 0.01 * jax.random.normal(k, shape, f32)

    b_gate = jnp.concatenate([0.1 * jax.random.normal(ks[30], (DEPTH, M_HEADS), f32),
                              3.0 + 3.0 * jax.random.uniform(ks[31], (DEPTH, M_HEADS), f32)], axis=-1)
    return {
        'x_prompt': nrm(ks[0], (BATCH, SEQ, D_MODEL), 1.0),
        'x_sample': nrm(ks[1], (DEC_BATCH, DEC_SEQ, D_MODEL), 1.0),
        'mem_prompt': nrm(ks[2], (BATCH, N_MEM, D_MODEL), 1.0),
        'cache_mem_k': nrm(ks[3], (DEPTH, DEC_BATCH, N_MEM, X_HEADS, X_DH), 1.0),
        'cache_mem_v': nrm(ks[4], (DEPTH, DEC_BATCH, N_MEM, X_HEADS, X_DH), 1.0),
        'state_C': nrm(ks[5], (DEPTH, DEC_BATCH, M_HEADS, M_DH, M_DH), 0.05),
        'state_n': nrm(ks[6], (DEPTH, DEC_BATCH, M_HEADS, M_DH), 0.05),
        'state_m': nrm(ks[7], (DEPTH, DEC_BATCH, M_HEADS), 1.0),
        'state_conv': nrm(ks[8], (DEPTH, DEC_BATCH, CONV_K - 1, C_W), 0.5),
        'norm1_g': gain(ks[9], (DEPTH, D_MODEL)),
        'w_in': nrm(ks[10], (DEPTH, D_MODEL, IN_COLS), D_MODEL ** -0.5),
        'b_gate': b_gate,
        'mlstm_norm_g': gain(ks[11], (DEPTH, M_HEADS, M_DH)),
        'conv_w': nrm(ks[12], (DEPTH, CONV_K, C_W), CONV_K ** -0.5),
        'conv_b': nrm(ks[13], (DEPTH, C_W), 0.02),
        'conv_ln_g': gain(ks[14], (DEPTH, C_W)),
        'conv_ln_b': nrm(ks[15], (DEPTH, C_W), 0.02),
        'mem_norm_g': gain(ks[16], (DEPTH, D_MODEL)),
        'w_mk': nrm(ks[17], (DEPTH, D_MODEL, X_W), D_MODEL ** -0.5),
        'w_mv': nrm(ks[18], (DEPTH, D_MODEL, X_W), D_MODEL ** -0.5),
        'w_out': nrm(ks[19], (DEPTH, MIX_W, D_MODEL), MIX_W ** -0.5),
        'norm2_g': gain(ks[20], (DEPTH, D_MODEL)),
        'peer_wq': nrm(ks[21], (DEPTH, D_MODEL, P_HEADS * P_DQ), D_MODEL ** -0.5),
        'peer_subkeys': nrm(ks[22], (DEPTH, P_HEADS, 2, P_NKEYS, P_DQ // 2), (P_DQ // 2) ** -0.5),
        'peer_u': nrm(ks[23], (DEPTH, P_NEXP, D_MODEL), D_MODEL ** -0.5),
        'peer_v': nrm(ks[24], (DEPTH, P_NEXP, D_MODEL), 0.3),
        'final_g': gain(ks[25], (D_MODEL,)),
    }


def reference(x_prompt, x_sample, mem_prompt, cache_mem_k, cache_mem_v, state_C, state_n, state_m, state_conv,
              norm1_g, w_in, b_gate, mlstm_norm_g, conv_w, conv_b, conv_ln_g, conv_ln_b, mem_norm_g, w_mk, w_mv,
              w_out, norm2_g, peer_wq, peer_subkeys, peer_u, peer_v, final_g):
    Bp = x_prompt.shape[0]
    xp, xs = x_prompt, x_sample
    mk_l, mv_l, Cp_l, np_l, mp_l, cvp_l, Cs_l, ns_l, ms_l, cvs_l = ([] for _ in range(10))
    for l in range(DEPTH):
        p = (norm1_g[l], w_in[l], b_gate[l], mlstm_norm_g[l], conv_w[l], conv_b[l], conv_ln_g[l], conv_ln_b[l],
             w_out[l], norm2_g[l], peer_wq[l], peer_subkeys[l], peer_u[l], peer_v[l])
        mk, mv = mem_kv(mem_prompt, mem_norm_g[l], w_mk[l], w_mv[l])
        C0 = jnp.zeros((Bp, M_HEADS, M_DH, M_DH), jnp.float32)
        n0 = jnp.zeros((Bp, M_HEADS, M_DH), jnp.float32)
        m0 = jnp.zeros((Bp, M_HEADS), jnp.float32)
        cv0 = jnp.zeros((Bp, CONV_K - 1, C_W), xp.dtype)
        xp, Cp, np_, mp, cvp = layer(xp, mk, mv, C0, n0, m0, cv0, p)
        xs, Cs, ns, ms, cvs = layer(xs, cache_mem_k[l], cache_mem_v[l], state_C[l], state_n[l], state_m[l],
                                    state_conv[l], p)
        mk_l.append(mk); mv_l.append(mv)
        Cp_l.append(Cp); np_l.append(np_); mp_l.append(mp); cvp_l.append(cvp)
        Cs_l.append(Cs); ns_l.append(ns); ms_l.append(ms); cvs_l.append(cvs)
    y_prompt = rmsnorm(xp, final_g)
    y_sample = rmsnorm(xs, final_g)
    new_mem_k_prompt = jnp.stack(mk_l)
    new_mem_v_prompt = jnp.stack(mv_l)
    new_C_prompt = jnp.stack(Cp_l)
    new_n_prompt = jnp.stack(np_l)
    new_m_prompt = jnp.stack(mp_l)
    new_conv_prompt = jnp.stack(cvp_l)
    new_C_sample = jnp.stack(Cs_l)
    new_n_sample = jnp.stack(ns_l)
    new_m_sample = jnp.stack(ms_l)
    new_conv_sample = jnp.stack(cvs_l)
    return (y_prompt, y_sample, new_mem_k_prompt, new_mem_v_prompt, new_C_prompt, new_n_prompt, new_m_prompt,
            new_conv_prompt, new_C_sample, new_n_sample, new_m_sample, new_conv_sample)
```

```python
import functools

import jax
import jax.numpy as jnp
from jax import lax
from jax.experimental import pallas as pl
from jax.experimental.pallas import tpu as pltpu

F32 = jnp.float32
BF16 = jnp.bfloat16

D_MODEL = 2048
M_HEADS = 4
M_DH = 256
M_W = M_HEADS * M_DH
C_W = 512
CONV_K = 31
CONV_PAST = CONV_K - 1
X_HEADS = 4
X_DH = 128
X_W = X_HEADS * X_DH
N_MEM = 256
P_HEADS = 8
P_NKEYS = 128
P_NEXP = P_NKEYS * P_NKEYS
P_TOPK = 16
EPS = 1e-6

SUBLANES = 8
LANES = 128
VMEM_LIMIT_BYTES = 58 * 1024 * 1024

MLSTM_L = 256
GROUP_B = 4
TOK_TILE = 512
PEER_TE = 512
CONV_T = 256
CONV_OFF = 32


def _dot(a, b):
    return jnp.dot(a, b, preferred_element_type=F32)


def _dot_nt(a, b):
    return lax.dot_general(a, b, (((1,), (1,)), ((), ())), preferred_element_type=F32)


def _dot_tn(a, b):
    return lax.dot_general(a, b, (((0,), (0,)), ((), ())), preferred_element_type=F32)


def _dot_exact(a, b):
    return jnp.dot(a, b, preferred_element_type=F32, precision=lax.Precision.HIGHEST)


def _rms(x, g):
    return x * lax.rsqrt(jnp.mean(x * x, axis=-1, keepdims=True) + EPS) * g


def _sigmoid(x):
    return 1.0 / (1.0 + jnp.exp(-x))


def _log_sigmoid(x):
    return jnp.minimum(x, 0.0) - jnp.log1p(jnp.exp(-jnp.abs(x)))


def _params(sem):
    return pltpu.CompilerParams(dimension_semantics=sem, vmem_limit_bytes=VMEM_LIMIT_BYTES)


def _const_spec(shape):
    nd = len(shape)
    return pl.BlockSpec(shape, lambda *_: (0,) * nd, pipeline_mode=pl.Buffered(1))


def _inproj_kernel(x_ref, g_ref, wb_ref, wf_ref, wg_ref, wgt_ref, bgr_ref, bgc_ref,
                   pb_ref, pf_ref, gcol_ref, grow_ref):
    xn = _rms(x_ref[...], g_ref[...]).astype(BF16)
    for c in range(0, wb_ref.shape[1], 512):
        pb_ref[:, c:c + 512] = _dot(xn, wb_ref[:, c:c + 512]).astype(BF16)
    for c in range(0, wf_ref.shape[1], 512):
        pf_ref[:, c:c + 512] = _dot(xn, wf_ref[:, c:c + 512])
    gc = _dot(xn, wg_ref[...]) + bgr_ref[...]
    lane = lax.broadcasted_iota(jnp.int32, gc.shape, 1)
    gcol_ref[...] = jnp.where(lane >= M_HEADS, _log_sigmoid(gc), gc)
    gr = _dot_nt(wgt_ref[...], xn) + bgc_ref[...]
    row = lax.broadcasted_iota(jnp.int32, gr.shape, 0)
    grow_ref[...] = jnp.where(row >= M_HEADS, _log_sigmoid(gr), gr)[0:SUBLANES, :]


def _inproj(x, g, wb, wf, wg, wgt, bgr, bgc):
    n = x.shape[0]
    tm = TOK_TILE
    return pl.pallas_call(
        _inproj_kernel,
        grid=(n // tm,),
        in_specs=[
            pl.BlockSpec((tm, D_MODEL), lambda i: (i, 0)),
            _const_spec((1, D_MODEL)),
            _const_spec(wb.shape), _const_spec(wf.shape), _const_spec(wg.shape),
            _const_spec(wgt.shape), _const_spec(bgr.shape), _const_spec(bgc.shape),
        ],
        out_specs=[
            pl.BlockSpec((tm, wb.shape[1]), lambda i: (i, 0)),
            pl.BlockSpec((tm, wf.shape[1]), lambda i: (i, 0)),
            pl.BlockSpec((tm, LANES), lambda i: (i, 0)),
            pl.BlockSpec((SUBLANES, tm), lambda i: (0, i)),
        ],
        out_shape=[
            jax.ShapeDtypeStruct((n, wb.shape[1]), BF16),
            jax.ShapeDtypeStruct((n, wf.shape[1]), F32),
            jax.ShapeDtypeStruct((n, LANES), F32),
            jax.ShapeDtypeStruct((SUBLANES, n), F32),
        ],
        compiler_params=_params(("parallel",)),
        name="inproj",
    )(x, g, wb, wf, wg, wgt, bgr, bgc)


def _mlstm_head_out(hh, gnorm, o):
    hn = hh * lax.rsqrt(jnp.mean(hh * hh, axis=-1, keepdims=True) + EPS) * gnorm
    return (hn * _sigmoid(o)).astype(BF16)


def _mlstm_prompt_kernel(q_ref, k_ref, v_ref, o_ref, gcol_ref, grow_ref, gn_ref,
                         hm_ref, c_out_ref, n_out_ref, m_out_ref, c_s, n_s, m_s):
    c = pl.program_id(1)
    L = MLSTM_L

    @pl.when(c == 0)
    def _():
        c_s[...] = jnp.zeros_like(c_s)
        n_s[...] = jnp.zeros_like(n_s)
        m_s[...] = jnp.zeros_like(m_s)

    row = lax.broadcasted_iota(jnp.int32, (L, L), 0)
    col = lax.broadcasted_iota(jnp.int32, (L, L), 1)
    causal = col <= row
    tril = jnp.where(causal, 1.0, 0.0).astype(F32)
    triu = jnp.where(row <= col, 1.0, 0.0).astype(F32)
    gcol = gcol_ref[...]
    grow = grow_ref[...]
    bc_all = _dot_exact(tril, gcol)
    br_all = _dot_exact(grow, triu)

    for h in range(M_HEADS):
        sl = slice(h * M_DH, (h + 1) * M_DH)
        q = q_ref[:, sl]
        k = k_ref[:, sl]
        v = v_ref[:, sl]
        li_r = grow[h:h + 1, :]
        li_c = gcol[:, h:h + 1]
        b_c = bc_all[:, M_HEADS + h:M_HEADS + h + 1]
        b_r = br_all[M_HEADS + h:M_HEADS + h + 1, :]
        m_prev = m_s[h:h + 1, 0:1]
        log_d = jnp.where(causal, b_c - b_r + li_r, -jnp.inf)
        log_inter = b_c + m_prev
        m_t = jnp.maximum(log_inter, jnp.max(log_d, axis=1, keepdims=True))
        dm = jnp.exp(log_d - m_t)
        inter = jnp.exp(log_inter - m_t)
        s = _dot_nt(q, k) * dm
        cmat = c_s[h]
        nrow = n_s[h:h + 1, :]
        num = _dot(s.astype(BF16), v) + inter * _dot_nt(q, cmat.astype(BF16))
        qn = jnp.sum(q.astype(F32) * nrow, axis=1, keepdims=True)
        den = jnp.sum(s, axis=1, keepdims=True) + inter * qn
        hh = num / jnp.maximum(jnp.abs(den), jnp.exp(-m_t))
        hm_ref[:, sl] = _mlstm_head_out(hh, gn_ref[:, sl], o_ref[:, sl])

        m_new = m_t[L - 1:L, :]
        b_last = b_c[L - 1:L, :]
        w_r = jnp.exp(b_last - b_r + li_r - m_new)
        w_c = jnp.exp(b_last - b_c + li_c - m_new)
        decay = jnp.exp(b_last + m_prev - m_new)
        vw = (v.astype(F32) * w_c).astype(BF16)
        c_s[h] = decay * cmat + _dot_tn(vw, k)
        wk = _dot(jnp.broadcast_to(w_r, (SUBLANES, L)).astype(BF16), k)
        n_s[h:h + 1, :] = decay * nrow + wk[0:1, :]
        m_s[h:h + 1, :] = jnp.broadcast_to(m_new, (1, LANES))

    @pl.when(c == pl.num_programs(1) - 1)
    def _():
        c_out_ref[0] = c_s[...]
        n_out_ref[0] = n_s[0:M_HEADS, :]
        m_out_ref[0] = m_s[...]


def _mlstm_prompt(pb, pf, gcol, grow, gnorm, batch, seq):
    L = MLSTM_L
    nc = seq // L
    n = batch * seq
    rows = lambda b, c: b * nc + c
    return pl.pallas_call(
        _mlstm_prompt_kernel,
        grid=(batch, nc),
        in_specs=[
            pl.BlockSpec((L, M_W), lambda b, c: (rows(b, c), 0)),
            pl.BlockSpec((L, M_W), lambda b, c: (rows(b, c), 1)),
            pl.BlockSpec((L, M_W), lambda b, c: (rows(b, c), 2)),
            pl.BlockSpec((L, M_W), lambda b, c: (rows(b, c), 0)),
            pl.BlockSpec((L, LANES), lambda b, c: (rows(b, c), 0)),
            pl.BlockSpec((SUBLANES, L), lambda b, c: (0, rows(b, c))),
            _const_spec((1, M_W)),
        ],
        out_specs=[
            pl.BlockSpec((L, M_W), lambda b, c: (rows(b, c), 0)),
            pl.BlockSpec((1, M_HEADS, M_DH, M_DH), lambda b, c: (b, 0, 0, 0)),
            pl.BlockSpec((1, M_HEADS, M_DH), lambda b, c: (b, 0, 0)),
            pl.BlockSpec((1, SUBLANES, LANES), lambda b, c: (b, 0, 0)),
        ],
        out_shape=[
            jax.ShapeDtypeStruct((n, M_W), BF16),
            jax.ShapeDtypeStruct((batch, M_HEADS, M_DH, M_DH), F32),
            jax.ShapeDtypeStruct((batch, M_HEADS, M_DH), F32),
            jax.ShapeDtypeStruct((batch, SUBLANES, LANES), F32),
        ],
        scratch_shapes=[
            pltpu.VMEM((M_HEADS, M_DH, M_DH), F32),
            pltpu.VMEM((SUBLANES, M_DH), F32),
            pltpu.VMEM((SUBLANES, LANES), F32),
        ],
        compiler_params=_params(("parallel", "arbitrary")),
        name="mlstm_prompt",
    )(pb, pb, pb, pf, gcol, grow, gnorm)


def _seq_index(token, seq):
    assert seq & (seq - 1) == 0
    return token >> (seq.bit_length() - 1)


def _to_row(colvec, eye):
    return jnp.sum(jnp.where(eye, colvec, 0.0), axis=0, keepdims=True)


def _mlstm_sample_kernel(q_ref, k_ref, v_ref, o_ref, gcol_ref, mtok_ref, gn_ref, c_in_ref, n_in_ref,
                         hm_ref, c_out_ref, n_out_ref, mtok_out_ref, *, seq):
    R = GROUP_B * seq
    row = lax.broadcasted_iota(jnp.int32, (R, R), 0)
    col = lax.broadcasted_iota(jnp.int32, (R, R), 1)
    eye = row == col
    same = _seq_index(row, seq) == _seq_index(col, seq)
    causal = same & (col <= row)
    rblk = _seq_index(lax.broadcasted_iota(jnp.int32, (R, 1), 0), seq)
    lane = lax.broadcasted_iota(jnp.int32, (R, LANES), 1)
    gcol = gcol_ref[...]
    mtok = mtok_ref[...]
    m_out = jnp.zeros((R, LANES), F32)

    def per_block(vals):
        out = jnp.zeros((R, 1), F32)
        for j in range(GROUP_B):
            out = jnp.where(rblk == j, vals[j], out)
        return out

    for h in range(M_HEADS):
        sl = slice(h * M_DH, (h + 1) * M_DH)
        q = q_ref[:, sl]
        k = k_ref[:, sl]
        v = v_ref[:, sl]
        kf = k.astype(F32)
        li_c = gcol[:, h:h + 1]
        lf_c = gcol[:, M_HEADS + h:M_HEADS + h + 1]
        li_r = _to_row(li_c, eye)
        lf_r = _to_row(lf_c, eye)
        m_prev = mtok[:, h:h + 1]
        b_c = jnp.sum(jnp.where(causal, lf_r, 0.0), axis=1, keepdims=True)
        b_r = _to_row(b_c, eye)
        log_d = jnp.where(causal, b_c - b_r + li_r, -jnp.inf)
        log_inter = b_c + m_prev
        m_t = jnp.maximum(log_inter, jnp.max(log_d, axis=1, keepdims=True))
        dm = jnp.exp(log_d - m_t)
        inter = jnp.exp(log_inter - m_t)
        s = _dot_nt(q, k) * dm
        qc = jnp.zeros((R, M_DH), F32)
        ntok = jnp.zeros((R, M_DH), F32)
        for j in range(GROUP_B):
            qc = jnp.where(rblk == j, _dot_nt(q, c_in_ref[j, h].astype(BF16)), qc)
            ntok = jnp.where(rblk == j, n_in_ref[j, h:h + 1, :], ntok)
        num = _dot(s.astype(BF16), v) + inter * qc
        qn = jnp.sum(q.astype(F32) * ntok, axis=1, keepdims=True)
        den = jnp.sum(s, axis=1, keepdims=True) + inter * qn
        hh = num / jnp.maximum(jnp.abs(den), jnp.exp(-m_t))
        hm_ref[:, sl] = _mlstm_head_out(hh, gn_ref[:, sl], o_ref[:, sl])

        last = [j * seq + seq - 1 for j in range(GROUP_B)]
        m_new = per_block([m_t[r:r + 1, :] for r in last])
        b_last = per_block([b_c[r:r + 1, :] for r in last])
        w_c = jnp.exp(b_last - b_c + li_c - m_new)
        decay_c = jnp.exp(b_last + m_prev - m_new)
        vw = v.astype(F32) * w_c
        wk = w_c * kf
        for j in range(GROUP_B):
            r = last[j]
            decay = decay_c[r:r + 1, :]
            upd = _dot_tn(jnp.where(rblk == j, vw, 0.0).astype(BF16), k)
            c_out_ref[j, h] = decay * c_in_ref[j, h] + upd
            n_out_ref[j, h:h + 1, :] = (decay * n_in_ref[j, h:h + 1, :]
                                        + jnp.sum(jnp.where(rblk == j, wk, 0.0), axis=0, keepdims=True))
        m_out = jnp.where(lane == h, m_new, m_out)
    mtok_out_ref[...] = m_out


def _mlstm_sample(pb, pf, gcol, mtok, gnorm, state_c, state_n, seq):
    batch = state_c.shape[0]
    R = GROUP_B * seq
    n = batch * seq
    return pl.pallas_call(
        functools.partial(_mlstm_sample_kernel, seq=seq),
        grid=(batch // GROUP_B,),
        in_specs=[
            pl.BlockSpec((R, M_W), lambda i: (i, 0)),
            pl.BlockSpec((R, M_W), lambda i: (i, 1)),
            pl.BlockSpec((R, M_W), lambda i: (i, 2)),
            pl.BlockSpec((R, M_W), lambda i: (i, 0)),
            pl.BlockSpec((R, LANES), lambda i: (i, 0)),
            pl.BlockSpec((R, LANES), lambda i: (i, 0)),
            _const_spec((1, M_W)),
            pl.BlockSpec((GROUP_B, M_HEADS, M_DH, M_DH), lambda i: (i, 0, 0, 0)),
            pl.BlockSpec((GROUP_B, M_HEADS, M_DH), lambda i: (i, 0, 0)),
        ],
        out_specs=[
            pl.BlockSpec((R, M_W), lambda i: (i, 0)),
            pl.BlockSpec((GROUP_B, M_HEADS, M_DH, M_DH), lambda i: (i, 0, 0, 0)),
            pl.BlockSpec((GROUP_B, M_HEADS, M_DH), lambda i: (i, 0, 0)),
            pl.BlockSpec((R, LANES), lambda i: (i, 0)),
        ],
        out_shape=[
            jax.ShapeDtypeStruct((n, M_W), BF16),
            jax.ShapeDtypeStruct(state_c.shape, F32),
            jax.ShapeDtypeStruct(state_n.shape, F32),
            jax.ShapeDtypeStruct((n, LANES), F32),
        ],
        compiler_params=_params(("parallel",)),
        name="mlstm_sample",
    )(pb, pb, pb, pf, gcol, mtok, gnorm, state_c, state_n)


def _conv_out(y, lg, lb):
    mu = jnp.mean(y, axis=-1, keepdims=True)
    yc = y - mu
    var = jnp.mean(yc * yc, axis=-1, keepdims=True)
    z = yc * lax.rsqrt(var + EPS) * lg + lb
    return (z * _sigmoid(z)).astype(BF16)


def _conv_prompt_kernel(a_ref, g_ref, w_ref, b_ref, lg_ref, lb_ref, yc_ref, new_ref, xp_s):
    t = pl.program_id(1)
    T = CONV_T

    @pl.when(t == 0)
    def _():
        xp_s[0:CONV_OFF, :] = jnp.zeros((CONV_OFF, C_W), F32)

    xp_s[CONV_OFF:CONV_OFF + T, :] = a_ref[...] * _sigmoid(g_ref[...])
    base = CONV_OFF - CONV_PAST
    acc = jnp.zeros((T, C_W), F32) + b_ref[...]
    for j in range(CONV_K):
        acc = acc + w_ref[j:j + 1, :] * xp_s[base + j:base + j + T, :]
    yc_ref[...] = _conv_out(acc, lg_ref[...], lb_ref[...])
    tail = xp_s[CONV_OFF + T - CONV_PAST:CONV_OFF + T, :]
    xp_s[base:CONV_OFF, :] = tail

    @pl.when(t == pl.num_programs(1) - 1)
    def _():
        new_ref[0] = tail


def _conv_prompt(pf, w, b, lg, lb, batch, seq):
    T = CONV_T
    nt = seq // T
    n = batch * seq
    return pl.pallas_call(
        _conv_prompt_kernel,
        grid=(batch, nt),
        in_specs=[
            pl.BlockSpec((T, C_W), lambda bi, t: (bi * nt + t, 2)),
            pl.BlockSpec((T, C_W), lambda bi, t: (bi * nt + t, 3)),
            _const_spec(w.shape), _const_spec((1, C_W)), _const_spec((1, C_W)), _const_spec((1, C_W)),
        ],
        out_specs=[
            pl.BlockSpec((T, C_W), lambda bi, t: (bi * nt + t, 0)),
            pl.BlockSpec((1, CONV_PAST, C_W), lambda bi, t: (bi, 0, 0)),
        ],
        out_shape=[
            jax.ShapeDtypeStruct((n, C_W), BF16),
            jax.ShapeDtypeStruct((batch, CONV_PAST, C_W), F32),
        ],
        scratch_shapes=[pltpu.VMEM((CONV_OFF + T, C_W), F32)],
        compiler_params=_params(("parallel", "arbitrary")),
        name="conv_prompt",
    )(pf, pf, w, b, lg, lb)


def _conv_sample_kernel(a_ref, g_ref, past_ref, w_ref, b_ref, lg_ref, lb_ref, yc_ref, new_ref,
                        xp_s, y_s, *, seq, gb):
    u = a_ref[...] * _sigmoid(g_ref[...])
    for j in range(gb):
        xp_s[0:CONV_PAST, :] = past_ref[j]
        xp_s[CONV_PAST:CONV_PAST + seq, :] = u[j * seq:(j + 1) * seq, :]
        acc = jnp.zeros((seq, C_W), F32) + b_ref[...]
        for i in range(CONV_K):
            acc = acc + w_ref[i:i + 1, :] * xp_s[i:i + seq, :]
        y_s[j * seq:(j + 1) * seq, :] = acc
        new_ref[j] = xp_s[seq:seq + CONV_PAST, :]
    yc_ref[...] = _conv_out(y_s[...], lg_ref[...], lb_ref[...])


def _conv_sample(pf, past, w, b, lg, lb, seq):
    batch = past.shape[0]
    gb = 8
    R = gb * seq
    n = batch * seq
    return pl.pallas_call(
        functools.partial(_conv_sample_kernel, seq=seq, gb=gb),
        grid=(batch // gb,),
        in_specs=[
            pl.BlockSpec((R, C_W), lambda i: (i, 2)),
            pl.BlockSpec((R, C_W), lambda i: (i, 3)),
            pl.BlockSpec((gb, CONV_PAST, C_W), lambda i: (i, 0, 0)),
            _const_spec(w.shape), _const_spec((1, C_W)), _const_spec((1, C_W)), _const_spec((1, C_W)),
        ],
        out_specs=[
            pl.BlockSpec((R, C_W), lambda i: (i, 0)),
            pl.BlockSpec((gb, CONV_PAST, C_W), lambda i: (i, 0, 0)),
        ],
        out_shape=[
            jax.ShapeDtypeStruct((n, C_W), BF16),
            jax.ShapeDtypeStruct(past.shape, F32),
        ],
        scratch_shapes=[pltpu.VMEM((CONV_PAST + seq + 6, C_W), F32), pltpu.VMEM((R, C_W), F32)],
        compiler_params=_params(("parallel",)),
        name="conv_sample",
    )(pf, pf, past, w, b, lg, lb)


def _memkv_kernel(x_ref, g_ref, w_ref, k_ref, v_ref):
    xn = _rms(x_ref[...], g_ref[...]).astype(BF16)
    k_ref[...] = _dot(xn, w_ref[:, 0:X_W])
    v_ref[...] = _dot(xn, w_ref[:, X_W:2 * X_W])


def _memkv(mem, g, w):
    n = mem.shape[0]
    tm = TOK_TILE
    return pl.pallas_call(
        _memkv_kernel,
        grid=(n // tm,),
        in_specs=[pl.BlockSpec((tm, D_MODEL), lambda i: (i, 0)), _const_spec((1, D_MODEL)),
                  _const_spec(w.shape)],
        out_specs=[pl.BlockSpec((tm, X_W), lambda i: (i, 0)), pl.BlockSpec((tm, X_W), lambda i: (i, 0))],
        out_shape=[jax.ShapeDtypeStruct((n, X_W), F32), jax.ShapeDtypeStruct((n, X_W), F32)],
        compiler_params=_params(("parallel",)),
        name="memkv",
    )(mem, g, w)


def _attend(q, k, v):
    s = _dot_nt(q, k.astype(BF16)) * (X_DH ** -0.5)
    p = jnp.exp(s - jnp.max(s, axis=-1, keepdims=True))
    p = p / jnp.sum(p, axis=-1, keepdims=True)
    return _dot(p.astype(BF16), v.astype(BF16))


def _xattn_prompt_kernel(q_ref, k_ref, v_ref, o_ref):
    for h in range(X_HEADS):
        sl = slice(h * X_DH, (h + 1) * X_DH)
        o_ref[:, sl] = _attend(q_ref[:, sl], k_ref[:, sl], v_ref[:, sl]).astype(BF16)


def _xattn_prompt(pb, k, v, batch, seq):
    tq = TOK_TILE
    nt = seq // tq
    qblk = 3 * M_W // X_W
    return pl.pallas_call(
        _xattn_prompt_kernel,
        grid=(batch, nt),
        in_specs=[
            pl.BlockSpec((tq, X_W), lambda b, t: (b * nt + t, qblk)),
            pl.BlockSpec((N_MEM, X_W), lambda b, t: (b, 0)),
            pl.BlockSpec((N_MEM, X_W), lambda b, t: (b, 0)),
        ],
        out_specs=pl.BlockSpec((tq, X_W), lambda b, t: (b * nt + t, 0)),
        out_shape=jax.ShapeDtypeStruct((batch * seq, X_W), BF16),
        compiler_params=_params(("parallel", "parallel")),
        name="xattn_prompt",
    )(pb, k, v)


def _xattn_sample_kernel(q_ref, k_ref, v_ref, o_ref, *, seq):
    R = GROUP_B * seq
    rblk = _seq_index(lax.broadcasted_iota(jnp.int32, (R, 1), 0), seq)
    for h in range(X_HEADS):
        sl = slice(h * X_DH, (h + 1) * X_DH)
        q = q_ref[:, sl]
        o = jnp.zeros((R, X_DH), F32)
        for j in range(GROUP_B):
            o = jnp.where(rblk == j, _attend(q, k_ref[j, :, sl], v_ref[j, :, sl]), o)
        o_ref[:, sl] = o.astype(BF16)


def _xattn_sample(pb, k, v, seq):
    batch = k.shape[0]
    R = GROUP_B * seq
    qblk = 3 * M_W // X_W
    return pl.pallas_call(
        functools.partial(_xattn_sample_kernel, seq=seq),
        grid=(batch // GROUP_B,),
        in_specs=[
            pl.BlockSpec((R, X_W), lambda i: (i, qblk)),
            pl.BlockSpec((GROUP_B, N_MEM, X_W), lambda i: (i, 0, 0)),
            pl.BlockSpec((GROUP_B, N_MEM, X_W), lambda i: (i, 0, 0)),
        ],
        out_specs=pl.BlockSpec((R, X_W), lambda i: (i, 0)),
        out_shape=jax.ShapeDtypeStruct((batch * seq, X_W), BF16),
        compiler_params=_params(("parallel",)),
        name="xattn_sample",
    )(pb, k, v)


def _outproj_kernel(x_ref, hm_ref, yc_ref, ox_ref, w_ref, o_ref):
    acc = x_ref[...]
    acc = acc + _dot(hm_ref[...], w_ref[0:M_W, :])
    acc = acc + _dot(yc_ref[...], w_ref[M_W:M_W + C_W, :])
    acc = acc + _dot(ox_ref[...], w_ref[M_W + C_W:, :])
    o_ref[...] = acc


def _outproj(x, hm, yc, ox, w):
    n = x.shape[0]
    tm = TOK_TILE
    return pl.pallas_call(
        _outproj_kernel,
        grid=(n // tm,),
        in_specs=[
            pl.BlockSpec((tm, D_MODEL), lambda i: (i, 0)),
            pl.BlockSpec((tm, M_W), lambda i: (i, 0)),
            pl.BlockSpec((tm, C_W), lambda i: (i, 0)),
            pl.BlockSpec((tm, X_W), lambda i: (i, 0)),
            _const_spec(w.shape),
        ],
        out_specs=pl.BlockSpec((tm, D_MODEL), lambda i: (i, 0)),
        out_shape=jax.ShapeDtypeStruct((n, D_MODEL), F32),
        compiler_params=_params(("parallel",)),
        name="outproj",
    )(x, hm, yc, ox, w)


def _sort16_pairs():
    n, pairs, p = P_TOPK, [], 1
    while p < n:
        k = p
        while k >= 1:
            for j in range(k % p, n - k, 2 * k):
                for i in range(min(k, n - j - k)):
                    if (i + j) // (2 * p) == (i + j + k) // (2 * p):
                        pairs.append((i + j, i + j + k))
            k //= 2
        p *= 2
    return pairs


_SORT16 = _sort16_pairs()
_BITONIC16 = [(i, i + d) for d in (8, 4, 2, 1) for i in range(P_TOPK) if not i & d]
_CANDS = [(p, q) for p in range(P_TOPK) for q in range(P_TOPK) if (p + 1) * (q + 1) <= P_TOPK]


def _exchange(xs, pairs):
    for a, b in pairs:
        hi = jnp.maximum(xs[a], xs[b])
        lo = jnp.minimum(xs[a], xs[b])
        xs[a], xs[b] = hi, lo
    return xs


def _top16_sorted(st):
    t = st.shape[1]
    x3 = st.reshape(P_NKEYS // SUBLANES, SUBLANES, t)
    xs = _exchange([x3[g] for g in range(P_TOPK)], _SORT16)
    for shift in (4, 2, 1):
        other = [pltpu.roll(xs[P_TOPK - 1 - r], shift, 0) for r in range(P_TOPK)]
        xs = _exchange([jnp.maximum(xs[r], other[r]) for r in range(P_TOPK)], _BITONIC16)
    return xs


def _peer_kernel(x_ref, g2_ref, gf_ref, wqt_ref, sk_ref, u_ref, vt_ref, y_ref,
                 xnt_s, s1_s, s2_s, f1_s, f2_s, th_s, acc_s, wg_s):
    e = pl.program_id(1)
    tm = x_ref.shape[0]

    @pl.when(e == 0)
    def _():
        xn = _rms(x_ref[...], g2_ref[...])
        xnt = jnp.transpose(xn).astype(BF16)
        xnt_s[...] = xnt
        qt = _dot(wqt_ref[...], xnt).astype(BF16)
        sub = lax.broadcasted_iota(jnp.int32, (SUBLANES, tm), 0)
        a_all = [jnp.zeros((SUBLANES, tm), F32) for _ in range(P_TOPK)]
        b_all = [jnp.zeros((SUBLANES, tm), F32) for _ in range(P_TOPK)]
        for h in range(P_HEADS):
            for c, (s_s, tops) in enumerate(((s1_s, a_all), (s2_s, b_all))):
                r0 = (2 * h + c) * P_NKEYS
                st = _dot(sk_ref[2 * h + c], qt[r0:r0 + P_NKEYS, :])
                s_s[h] = st
                srt = _top16_sorted(st)
                for r in range(P_TOPK):
                    tops[r] = jnp.where(sub == h, srt[r], tops[r])
        cands = [a_all[p] + b_all[q] for p, q in _CANDS]
        theta = jnp.full((SUBLANES, tm), jnp.inf, F32)
        for x in cands:
            cnt = jnp.zeros((SUBLANES, tm), F32)
            for y in cands:
                if y is not x:
                    cnt = cnt + jnp.where(y > x, 1.0, 0.0)
            theta = jnp.minimum(theta, jnp.where(cnt < P_TOPK, x, jnp.inf))
        mx = cands[0]
        z = jnp.zeros((SUBLANES, tm), F32)
        for x in cands:
            z = z + jnp.where(x >= theta, jnp.exp(x - mx), 0.0)
        th_s[...] = theta
        rz = 1.0 / z
        for h in range(P_HEADS):
            f1_s[h] = jnp.exp(s1_s[h] - a_all[0][h:h + 1, :]) * rz[h:h + 1, :]
            f2_s[h] = jnp.exp(s2_s[h] - b_all[0][h:h + 1, :])
        acc_s[...] = jnp.zeros_like(acc_s)

    te = u_ref.shape[0]
    at = _dot(u_ref[...], xnt_s[...])
    for sb in range(te // P_NKEYS):
        i = e * (te // P_NKEYS) + sb
        w = jnp.zeros((P_NKEYS, tm), F32)
        for h in range(P_HEADS):
            s1row = s1_s[h, pl.ds(i, 1), :]
            f1row = f1_s[h, pl.ds(i, 1), :]
            hit = (s2_s[h] + s1row) >= th_s[h:h + 1, :]
            w = w + jnp.where(hit, f2_s[h], 0.0) * f1row
        a = at[sb * P_NKEYS:(sb + 1) * P_NKEYS, :]
        gelu = 0.5 * a * (1.0 + jnp.tanh(0.7978845608028654 * (a + 0.044715 * (a * a * a))))
        wg_s[sb * P_NKEYS:(sb + 1) * P_NKEYS, :] = (w * gelu).astype(BF16)
    acc_s[...] += _dot(vt_ref[...], wg_s[...])

    @pl.when(e == pl.num_programs(1) - 1)
    def _():
        y = x_ref[...] + jnp.transpose(acc_s[...])
        y_ref[...] = _rms(y, gf_ref[...])


def _peer(x1, g2, gf, wqt, sk, u, vt, tm):
    n = x1.shape[0]
    te = PEER_TE
    return pl.pallas_call(
        _peer_kernel,
        grid=(n // tm, P_NEXP // te),
        in_specs=[
            pl.BlockSpec((tm, D_MODEL), lambda i, e: (i, 0), pipeline_mode=pl.Buffered(1)),
            _const_spec((1, D_MODEL)), _const_spec((1, D_MODEL)),
            _const_spec(wqt.shape), _const_spec(sk.shape),
            pl.BlockSpec((te, D_MODEL), lambda i, e: (e, 0)),
            pl.BlockSpec((D_MODEL, te), lambda i, e: (0, e)),
        ],
        out_specs=pl.BlockSpec((tm, D_MODEL), lambda i, e: (i, 0)),
        out_shape=jax.ShapeDtypeStruct((n, D_MODEL), F32),
        scratch_shapes=[
            pltpu.VMEM((D_MODEL, tm), BF16),
            pltpu.VMEM((P_HEADS, P_NKEYS, tm), F32),
            pltpu.VMEM((P_HEADS, P_NKEYS, tm), F32),
            pltpu.VMEM((P_HEADS, P_NKEYS, tm), F32),
            pltpu.VMEM((P_HEADS, P_NKEYS, tm), F32),
            pltpu.VMEM((SUBLANES, tm), F32),
            pltpu.VMEM((D_MODEL, tm), F32),
            pltpu.VMEM((te, tm), BF16),
        ],
        compiler_params=_params(("parallel", "arbitrary")),
        name="peer",
    )(x1, g2, gf, wqt, sk, u, vt)


def kernel(x_prompt, x_sample, mem_prompt, cache_mem_k, cache_mem_v, state_C, state_n, state_m, state_conv,
           norm1_g, w_in, b_gate, mlstm_norm_g, conv_w, conv_b, conv_ln_g, conv_ln_b, mem_norm_g, w_mk, w_mv,
           w_out, norm2_g, peer_wq, peer_subkeys, peer_u, peer_v, final_g):
    depth = w_in.shape[0]
    assert depth == 1
    bp, seq_p, _ = x_prompt.shape
    bs, seq_s, _ = x_sample.shape
    l = 0

    w = w_in[l]
    o0, g0, a0, gg0, qx0 = 3 * M_W, 4 * M_W, 4 * M_W + 2 * M_HEADS, 4 * M_W + 2 * M_HEADS + C_W, \
        4 * M_W + 2 * M_HEADS + 2 * C_W
    wb = jnp.concatenate([w[:, 0:M_W], w[:, M_W:2 * M_W] * (M_DH ** -0.5), w[:, 2 * M_W:3 * M_W],
                          w[:, qx0:qx0 + X_W]], axis=1).astype(BF16)
    wf = jnp.concatenate([w[:, o0:o0 + M_W], w[:, a0:a0 + C_W], w[:, gg0:gg0 + C_W]], axis=1).astype(BF16)
    wgate = w[:, g0:g0 + 2 * M_HEADS].astype(BF16)
    wg = jnp.pad(wgate, ((0, 0), (0, LANES - 2 * M_HEADS)))
    wgt = jnp.pad(wgate.T, ((0, 2 * SUBLANES - 2 * M_HEADS), (0, 0)))
    bgr = jnp.pad(b_gate[l][None, :], ((0, 0), (0, LANES - 2 * M_HEADS)))
    bgc = jnp.pad(b_gate[l][:, None], ((0, 2 * SUBLANES - 2 * M_HEADS), (0, 0)))
    g1 = norm1_g[l][None, :]
    gnorm = mlstm_norm_g[l].reshape(1, M_W)
    cw = jnp.pad(conv_w[l], ((0, 1), (0, 0)))
    cb, clg, clb = conv_b[l][None, :], conv_ln_g[l][None, :], conv_ln_b[l][None, :]
    wkv = jnp.concatenate([w_mk[l], w_mv[l]], axis=1).astype(BF16)
    wo = w_out[l].astype(BF16)
    wqt = peer_wq[l].T.astype(BF16)
    sk = peer_subkeys[l].reshape(2 * P_HEADS, P_NKEYS, P_NKEYS).astype(BF16)
    u = peer_u[l].astype(BF16)
    vt = peer_v[l].T.astype(BF16)
    g2 = norm2_g[l][None, :]
    gf = final_g[None, :]

    xp = x_prompt.reshape(bp * seq_p, D_MODEL)
    xs = x_sample.reshape(bs * seq_s, D_MODEL)

    pb, pf, gcol, grow = _inproj(xp, g1, wb, wf, wg, wgt, bgr, bgc)
    hm, c_p, n_p, m_p = _mlstm_prompt(pb, pf, gcol, grow, gnorm, bp, seq_p)
    yc, conv_p = _conv_prompt(pf, cw, cb, clg, clb, bp, seq_p)
    mk, mv = _memkv(mem_prompt.reshape(bp * N_MEM, D_MODEL), mem_norm_g[l][None, :], wkv)
    ox = _xattn_prompt(pb, mk, mv, bp, seq_p)
    x1 = _outproj(xp, hm, yc, ox, wo)
    y_p = _peer(x1, g2, gf, wqt, sk, u, vt, TOK_TILE)

    pb, pf, gcol, _ = _inproj(xs, g1, wb, wf, wg, wgt, bgr, bgc)
    mtok = jnp.pad(jnp.repeat(state_m[l], seq_s, axis=0), ((0, 0), (0, LANES - M_HEADS)))
    hm, c_s, n_s, mtok_new = _mlstm_sample(pb, pf, gcol, mtok, gnorm, state_C[l], state_n[l], seq_s)
    yc, conv_s = _conv_sample(pf, state_conv[l], cw, cb, clg, clb, seq_s)
    ox = _xattn_sample(pb, cache_mem_k[l].reshape(bs, N_MEM, X_W), cache_mem_v[l].reshape(bs, N_MEM, X_W), seq_s)
    x1 = _outproj(xs, hm, yc, ox, wo)
    y_s = _peer(x1, g2, gf, wqt, sk, u, vt, TOK_TILE)

    m_s = mtok_new.reshape(bs, seq_s, LANES)[:, seq_s - 1, :M_HEADS]
    return (y_p.reshape(bp, seq_p, D_MODEL),
            y_s.reshape(bs, seq_s, D_MODEL),
            mk.reshape(1, bp, N_MEM, X_HEADS, X_DH),
            mv.reshape(1, bp, N_MEM, X_HEADS, X_DH),
            c_p[None], n_p[None], m_p[None, :, :M_HEADS, 0], conv_p[None],
            c_s[None], n_s[None], m_s[None], conv_s[None])
```

```python
import functools

import jax
import jax.numpy as jnp
from jax import lax
from jax.experimental import pallas as pl
from jax.experimental.pallas import tpu as pltpu

F32 = jnp.float32
BF16 = jnp.bfloat16

D_MODEL = 2048
M_HEADS = 4
M_DH = 256
M_W = M_HEADS * M_DH
C_W = 512
CONV_K = 31
CONV_PAST = CONV_K - 1
X_HEADS = 4
X_DH = 128
X_W = X_HEADS * X_DH
N_MEM = 256
P_HEADS = 8
P_NKEYS = 128
P_NEXP = P_NKEYS * P_NKEYS
P_TOPK = 16
EPS = 1e-6

SUBLANES = 8
LANES = 128
VMEM_LIMIT_BYTES = 58 * 1024 * 1024

MLSTM_L = 256
GROUP_B = 4
TOK_TILE = 512
PEER_TE = 512
CONV_T = 256
CONV_OFF = 32


def _dot(a, b):
    return jnp.dot(a, b, preferred_element_type=F32)


def _dot_nt(a, b):
    return lax.dot_general(a, b, (((1,), (1,)), ((), ())), preferred_element_type=F32)


def _dot_tn(a, b):
    return lax.dot_general(a, b, (((0,), (0,)), ((), ())), preferred_element_type=F32)


def _dot_exact(a, b):
    return jnp.dot(a, b, preferred_element_type=F32, precision=lax.Precision.HIGHEST)


def _rms(x, g):
    return x * lax.rsqrt(jnp.mean(x * x, axis=-1, keepdims=True) + EPS) * g


def _sigmoid(x):
    return 1.0 / (1.0 + jnp.exp(-x))


def _log_sigmoid(x):
    return jnp.minimum(x, 0.0) - jnp.log1p(jnp.exp(-jnp.abs(x)))


def _params(sem):
    return pltpu.CompilerParams(dimension_semantics=sem, vmem_limit_bytes=VMEM_LIMIT_BYTES)


def _const_spec(shape):
    nd = len(shape)
    return pl.BlockSpec(shape, lambda *_: (0,) * nd, pipeline_mode=pl.Buffered(1))


def _inproj_kernel(x_ref, g_ref, wb_ref, wf_ref, wg_ref, wgt_ref, bgr_ref, bgc_ref,
                   pb_ref, pf_ref, gcol_ref, grow_ref):
    xn = _rms(x_ref[...], g_ref[...]).astype(BF16)
    for c in range(0, wb_ref.shape[1], 512):
        pb_ref[:, c:c + 512] = _dot(xn, wb_ref[:, c:c + 512]).astype(BF16)
    for c in range(0, wf_ref.shape[1], 512):
        pf_ref[:, c:c + 512] = _dot(xn, wf_ref[:, c:c + 512])
    gc = _dot(xn, wg_ref[...]) + bgr_ref[...]
    lane = lax.broadcasted_iota(jnp.int32, gc.shape, 1)
    gcol_ref[...] = jnp.where(lane >= M_HEADS, _log_sigmoid(gc), gc)
    gr = _dot_nt(wgt_ref[...], xn) + bgc_ref[...]
    row = lax.broadcasted_iota(jnp.int32, gr.shape, 0)
    grow_ref[...] = jnp.where(row >= M_HEADS, _log_sigmoid(gr), gr)[0:SUBLANES, :]


def _inproj(x, g, wb, wf, wg, wgt, bgr, bgc):
    n = x.shape[0]
    tm = TOK_TILE
    return pl.pallas_call(
        _inproj_kernel,
        grid=(n // tm,),
        in_specs=[
            pl.BlockSpec((tm, D_MODEL), lambda i: (i, 0)),
            _const_spec((1, D_MODEL)),
            _const_spec(wb.shape), _const_spec(wf.shape), _const_spec(wg.shape),
            _const_spec(wgt.shape), _const_spec(bgr.shape), _const_spec(bgc.shape),
        ],
        out_specs=[
            pl.BlockSpec((tm, wb.shape[1]), lambda i: (i, 0)),
            pl.BlockSpec((tm, wf.shape[1]), lambda i: (i, 0)),
            pl.BlockSpec((tm, LANES), lambda i: (i, 0)),
            pl.BlockSpec((SUBLANES, tm), lambda i: (0, i)),
        ],
        out_shape=[
            jax.ShapeDtypeStruct((n, wb.shape[1]), BF16),
            jax.ShapeDtypeStruct((n, wf.shape[1]), F32),
            jax.ShapeDtypeStruct((n, LANES), F32),
            jax.ShapeDtypeStruct((SUBLANES, n), F32),
        ],
        compiler_params=_params(("parallel",)),
        name="inproj",
    )(x, g, wb, wf, wg, wgt, bgr, bgc)


def _mlstm_head_out(hh, gnorm, o):
    hn = hh * lax.rsqrt(jnp.mean(hh * hh, axis=-1, keepdims=True) + EPS) * gnorm
    return (hn * _sigmoid(o)).astype(BF16)


def _mlstm_prompt_kernel(q_ref, k_ref, v_ref, o_ref, gcol_ref, grow_ref, gn_ref,
                         hm_ref, c_out_ref, n_out_ref, m_out_ref, c_s, n_s, m_s):
    c = pl.program_id(1)
    L = MLSTM_L

    @pl.when(c == 0)
    def _():
        c_s[...] = jnp.zeros_like(c_s)
        n_s[...] = jnp.zeros_like(n_s)
        m_s[...] = jnp.zeros_like(m_s)

    row = lax.broadcasted_iota(jnp.int32, (L, L), 0)
    col = lax.broadcasted_iota(jnp.int32, (L, L), 1)
    causal = col <= row
    tril = jnp.where(causal, 1.0, 0.0).astype(F32)
    triu = jnp.where(row <= col, 1.0, 0.0).astype(F32)
    gcol = gcol_ref[...]
    grow = grow_ref[...]
    bc_all = _dot_exact(tril, gcol)
    br_all = _dot_exact(grow, triu)

    for h in range(M_HEADS):
        sl = slice(h * M_DH, (h + 1) * M_DH)
        q = q_ref[:, sl]
        k = k_ref[:, sl]
        v = v_ref[:, sl]
        li_r = grow[h:h + 1, :]
        li_c = gcol[:, h:h + 1]
        b_c = bc_all[:, M_HEADS + h:M_HEADS + h + 1]
        b_r = br_all[M_HEADS + h:M_HEADS + h + 1, :]
        m_prev = m_s[h:h + 1, 0:1]
        log_d = jnp.where(causal, b_c - b_r + li_r, -jnp.inf)
        log_inter = b_c + m_prev
        m_t = jnp.maximum(log_inter, jnp.max(log_d, axis=1, keepdims=True))
        dm = jnp.exp(log_d - m_t)
        inter = jnp.exp(log_inter - m_t)
        s = _dot_nt(q, k) * dm
        cmat = c_s[h]
        nrow = n_s[h:h + 1, :]
        num = _dot(s.astype(BF16), v) + inter * _dot_nt(q, cmat.astype(BF16))
        qn = jnp.sum(q.astype(F32) * nrow, axis=1, keepdims=True)
        den = jnp.sum(s, axis=1, keepdims=True) + inter * qn
        hh = num / jnp.maximum(jnp.abs(den), jnp.exp(-m_t))
        hm_ref[:, sl] = _mlstm_head_out(hh, gn_ref[:, sl], o_ref[:, sl])

        m_new = m_t[L - 1:L, :]
        b_last = b_c[L - 1:L, :]
        w_r = jnp.exp(b_last - b_r + li_r - m_new)
        w_c = jnp.exp(b_last - b_c + li_c - m_new)
        decay = jnp.exp(b_last + m_prev - m_new)
        vw = (v.astype(F32) * w_c).astype(BF16)
        c_s[h] = decay * cmat + _dot_tn(vw, k)
        wk = _dot(jnp.broadcast_to(w_r, (SUBLANES, L)).astype(BF16), k)
        n_s[h:h + 1, :] = decay * nrow + wk[0:1, :]
        m_s[h:h + 1, :] = jnp.broadcast_to(m_new, (1, LANES))

    @pl.when(c == pl.num_programs(1) - 1)
    def _():
        c_out_ref[0] = c_s[...]
        n_out_ref[0] = n_s[0:M_HEADS, :]
        m_out_ref[0] = m_s[...]


def _mlstm_prompt(pb, pf, gcol, grow, gnorm, batch, seq):
    L = MLSTM_L
    nc = seq // L
    n = batch * seq
    rows = lambda b, c: b * nc + c
    return pl.pallas_call(
        _mlstm_prompt_kernel,
        grid=(batch, nc),
        in_specs=[
            pl.BlockSpec((L, M_W), lambda b, c: (rows(b, c), 0)),
            pl.BlockSpec((L, M_W), lambda b, c: (rows(b, c), 1)),
            pl.BlockSpec((L, M_W), lambda b, c: (rows(b, c), 2)),
            pl.BlockSpec((L, M_W), lambda b, c: (rows(b, c), 0)),
            pl.BlockSpec((L, LANES), lambda b, c: (rows(b, c), 0)),
            pl.BlockSpec((SUBLANES, L), lambda b, c: (0, rows(b, c))),
            _const_spec((1, M_W)),
        ],
        out_specs=[
            pl.BlockSpec((L, M_W), lambda b, c: (rows(b, c), 0)),
            pl.BlockSpec((1, M_HEADS, M_DH, M_DH), lambda b, c: (b, 0, 0, 0)),
            pl.BlockSpec((1, M_HEADS, M_DH), lambda b, c: (b, 0, 0)),
            pl.BlockSpec((1, SUBLANES, LANES), lambda b, c: (b, 0, 0)),
        ],
        out_shape=[
            jax.ShapeDtypeStruct((n, M_W), BF16),
            jax.ShapeDtypeStruct((batch, M_HEADS, M_DH, M_DH), F32),
            jax.ShapeDtypeStruct((batch, M_HEADS, M_DH), F32),
            jax.ShapeDtypeStruct((batch, SUBLANES, LANES), F32),
        ],
        scratch_shapes=[
            pltpu.VMEM((M_HEADS, M_DH, M_DH), F32),
            pltpu.VMEM((SUBLANES, M_DH), F32),
            pltpu.VMEM((SUBLANES, LANES), F32),
        ],
        compiler_params=_params(("parallel", "arbitrary")),
        name="mlstm_prompt",
    )(pb, pb, pb, pf, gcol, grow, gnorm)


def _seq_index(token, seq):
    assert seq & (seq - 1) == 0
    return token >> (seq.bit_length() - 1)


def _to_row(colvec, eye):
    return jnp.sum(jnp.where(eye, colvec, 0.0), axis=0, keepdims=True)


def _mlstm_sample_kernel(q_ref, k_ref, v_ref, o_ref, gcol_ref, mtok_ref, gn_ref, c_in_ref, n_in_ref,
                         hm_ref, c_out_ref, n_out_ref, mtok_out_ref, *, seq):
    R = GROUP_B * seq
    row = lax.broadcasted_iota(jnp.int32, (R, R), 0)
    col = lax.broadcasted_iota(jnp.int32, (R, R), 1)
    eye = row == col
    same = _seq_index(row, seq) == _seq_index(col, seq)
    causal = same & (col <= row)
    rblk = _seq_index(lax.broadcasted_iota(jnp.int32, (R, 1), 0), seq)
    lane = lax.broadcasted_iota(jnp.int32, (R, LANES), 1)
    gcol = gcol_ref[...]
    mtok = mtok_ref[...]
    m_out = jnp.zeros((R, LANES), F32)

    def per_block(vals):
        out = jnp.zeros((R, 1), F32)
        for j in range(GROUP_B):
            out = jnp.where(rblk == j, vals[j], out)
        return out

    for h in range(M_HEADS):
        sl = slice(h * M_DH, (h + 1) * M_DH)
        q = q_ref[:, sl]
        k = k_ref[:, sl]
        v = v_ref[:, sl]
        kf = k.astype(F32)
        li_c = gcol[:, h:h + 1]
        lf_c = gcol[:, M_HEADS + h:M_HEADS + h + 1]
        li_r = _to_row(li_c, eye)
        lf_r = _to_row(lf_c, eye)
        m_prev = mtok[:, h:h + 1]
        b_c = jnp.sum(jnp.where(causal, lf_r, 0.0), axis=1, keepdims=True)
        b_r = _to_row(b_c, eye)
        log_d = jnp.where(causal, b_c - b_r + li_r, -jnp.inf)
        log_inter = b_c + m_prev
        m_t = jnp.maximum(log_inter, jnp.max(log_d, axis=1, keepdims=True))
        dm = jnp.exp(log_d - m_t)
        inter = jnp.exp(log_inter - m_t)
        s = _dot_nt(q, k) * dm
        qc = jnp.zeros((R, M_DH), F32)
        ntok = jnp.zeros((R, M_DH), F32)
        for j in range(GROUP_B):
            qc = jnp.where(rblk == j, _dot_nt(q, c_in_ref[j, h].astype(BF16)), qc)
            ntok = jnp.where(rblk == j, n_in_ref[j, h:h + 1, :], ntok)
        num = _dot(s.astype(BF16), v) + inter * qc
        qn = jnp.sum(q.astype(F32) * ntok, axis=1, keepdims=True)
        den = jnp.sum(s, axis=1, keepdims=True) + inter * qn
        hh = num / jnp.maximum(jnp.abs(den), jnp.exp(-m_t))
        hm_ref[:, sl] = _mlstm_head_out(hh, gn_ref[:, sl], o_ref[:, sl])

        last = [j * seq + seq - 1 for j in range(GROUP_B)]
        m_new = per_block([m_t[r:r + 1, :] for r in last])
        b_last = per_block([b_c[r:r + 1, :] for r in last])
        w_c = jnp.exp(b_last - b_c + li_c - m_new)
        decay_c = jnp.exp(b_last + m_prev - m_new)
        vw = v.astype(F32) * w_c
        wk = w_c * kf
        for j in range(GROUP_B):
            r = last[j]
            decay = decay_c[r:r + 1, :]
            upd = _dot_tn(jnp.where(rblk == j, vw, 0.0).astype(BF16), k)
            c_out_ref[j, h] = decay * c_in_ref[j, h] + upd
            n_out_ref[j, h:h + 1, :] = (decay * n_in_ref[j, h:h + 1, :]
                                        + jnp.sum(jnp.where(rblk == j, wk, 0.0), axis=0, keepdims=True))
        m_out = jnp.where(lane == h, m_new, m_out)
    mtok_out_ref[...] = m_out


def _mlstm_sample(pb, pf, gcol, mtok, gnorm, state_c, state_n, seq):
    batch = state_c.shape[0]
    R = GROUP_B * seq
    n = batch * seq
    return pl.pallas_call(
        functools.partial(_mlstm_sample_kernel, seq=seq),
        grid=(batch // GROUP_B,),
        in_specs=[
            pl.BlockSpec((R, M_W), lambda i: (i, 0)),
            pl.BlockSpec((R, M_W), lambda i: (i, 1)),
            pl.BlockSpec((R, M_W), lambda i: (i, 2)),
            pl.BlockSpec((R, M_W), lambda i: (i, 0)),
            pl.BlockSpec((R, LANES), lambda i: (i, 0)),
            pl.BlockSpec((R, LANES), lambda i: (i, 0)),
            _const_spec((1, M_W)),
            pl.BlockSpec((GROUP_B, M_HEADS, M_DH, M_DH), lambda i: (i, 0, 0, 0)),
            pl.BlockSpec((GROUP_B, M_HEADS, M_DH), lambda i: (i, 0, 0)),
        ],
        out_specs=[
            pl.BlockSpec((R, M_W), lambda i: (i, 0)),
            pl.BlockSpec((GROUP_B, M_HEADS, M_DH, M_DH), lambda i: (i, 0, 0, 0)),
            pl.BlockSpec((GROUP_B, M_HEADS, M_DH), lambda i: (i, 0, 0)),
            pl.BlockSpec((R, LANES), lambda i: (i, 0)),
        ],
        out_shape=[
            jax.ShapeDtypeStruct((n, M_W), BF16),
            jax.ShapeDtypeStruct(state_c.shape, F32),
            jax.ShapeDtypeStruct(state_n.shape, F32),
            jax.ShapeDtypeStruct((n, LANES), F32),
        ],
        compiler_params=_params(("parallel",)),
        name="mlstm_sample",
    )(pb, pb, pb, pf, gcol, mtok, gnorm, state_c, state_n)


def _conv_out(y, lg, lb):
    mu = jnp.mean(y, axis=-1, keepdims=True)
    yc = y - mu
    var = jnp.mean(yc * yc, axis=-1, keepdims=True)
    z = yc * lax.rsqrt(var + EPS) * lg + lb
    return (z * _sigmoid(z)).astype(BF16)


def _conv_prompt_kernel(a_ref, g_ref, w_ref, b_ref, lg_ref, lb_ref, yc_ref, new_ref, xp_s):
    t = pl.program_id(1)
    T = CONV_T

    @pl.when(t == 0)
    def _():
        xp_s[0:CONV_OFF, :] = jnp.zeros((CONV_OFF, C_W), F32)

    xp_s[CONV_OFF:CONV_OFF + T, :] = a_ref[...] * _sigmoid(g_ref[...])
    base = CONV_OFF - CONV_PAST
    acc = jnp.zeros((T, C_W), F32) + b_ref[...]
    for j in range(CONV_K):
        acc = acc + w_ref[j:j + 1, :] * xp_s[base + j:base + j + T, :]
    yc_ref[...] = _conv_out(acc, lg_ref[...], lb_ref[...])
    tail = xp_s[CONV_OFF + T - CONV_PAST:CONV_OFF + T, :]
    xp_s[base:CONV_OFF, :] = tail

    @pl.when(t == pl.num_programs(1) - 1)
    def _():
        new_ref[0] = tail


def _conv_prompt(pf, w, b, lg, lb, batch, seq):
    T = CONV_T
    nt = seq // T
    n = batch * seq
    return pl.pallas_call(
        _conv_prompt_kernel,
        grid=(batch, nt),
        in_specs=[
            pl.BlockSpec((T, C_W), lambda bi, t: (bi * nt + t, 2)),
            pl.BlockSpec((T, C_W), lambda bi, t: (bi * nt + t, 3)),
            _const_spec(w.shape), _const_spec((1, C_W)), _const_spec((1, C_W)), _const_spec((1, C_W)),
        ],
        out_specs=[
            pl.BlockSpec((T, C_W), lambda bi, t: (bi * nt + t, 0)),
            pl.BlockSpec((1, CONV_PAST, C_W), lambda bi, t: (bi, 0, 0)),
        ],
        out_shape=[
            jax.ShapeDtypeStruct((n, C_W), BF16),
            jax.ShapeDtypeStruct((batch, CONV_PAST, C_W), F32),
        ],
        scratch_shapes=[pltpu.VMEM((CONV_OFF + T, C_W), F32)],
        compiler_params=_params(("parallel", "arbitrary")),
        name="conv_prompt",
    )(pf, pf, w, b, lg, lb)


def _conv_sample_kernel(a_ref, g_ref, past_ref, w_ref, b_ref, lg_ref, lb_ref, yc_ref, new_ref,
                        xp_s, y_s, *, seq, gb):
    u = a_ref[...] * _sigmoid(g_ref[...])
    for j in range(gb):
        xp_s[0:CONV_PAST, :] = past_ref[j]
        xp_s[CONV_PAST:CONV_PAST + seq, :] = u[j * seq:(j + 1) * seq, :]
        acc = jnp.zeros((seq, C_W), F32) + b_ref[...]
        for i in range(CONV_K):
            acc = acc + w_ref[i:i + 1, :] * xp_s[i:i + seq, :]
        y_s[j * seq:(j + 1) * seq, :] = acc
        new_ref[j] = xp_s[seq:seq + CONV_PAST, :]
    yc_ref[...] = _conv_out(y_s[...], lg_ref[...], lb_ref[...])


def _conv_sample(pf, past, w, b, lg, lb, seq):
    batch = past.shape[0]
    gb = 8
    R = gb * seq
    n = batch * seq
    return pl.pallas_call(
        functools.partial(_conv_sample_kernel, seq=seq, gb=gb),
        grid=(batch // gb,),
        in_specs=[
            pl.BlockSpec((R, C_W), lambda i: (i, 2)),
            pl.BlockSpec((R, C_W), lambda i: (i, 3)),
            pl.BlockSpec((gb, CONV_PAST, C_W), lambda i: (i, 0, 0)),
            _const_spec(w.shape), _const_spec((1, C_W)), _const_spec((1, C_W)), _const_spec((1, C_W)),
        ],
        out_specs=[
            pl.BlockSpec((R, C_W), lambda i: (i, 0)),
            pl.BlockSpec((gb, CONV_PAST, C_W), lambda i: (i, 0, 0)),
        ],
        out_shape=[
            jax.ShapeDtypeStruct((n, C_W), BF16),
            jax.ShapeDtypeStruct(past.shape, F32),
        ],
        scratch_shapes=[pltpu.VMEM((CONV_PAST + seq + 6, C_W), F32), pltpu.VMEM((R, C_W), F32)],
        compiler_params=_params(("parallel",)),
        name="conv_sample",
    )(pf, pf, past, w, b, lg, lb)


def _memkv_kernel(x_ref, g_ref, w_ref, k_ref, v_ref):
    xn = _rms(x_ref[...], g_ref[...]).astype(BF16)
    k_ref[...] = _dot(xn, w_ref[:, 0:X_W])
    v_ref[...] = _dot(xn, w_ref[:, X_W:2 * X_W])


def _memkv(mem, g, w):
    n = mem.shape[0]
    tm = TOK_TILE
    return pl.pallas_call(
        _memkv_kernel,
        grid=(n // tm,),
        in_specs=[pl.BlockSpec((tm, D_MODEL), lambda i: (i, 0)), _const_spec((1, D_MODEL)),
                  _const_spec(w.shape)],
        out_specs=[pl.BlockSpec((tm, X_W), lambda i: (i, 0)), pl.BlockSpec((tm, X_W), lambda i: (i, 0))],
        out_shape=[jax.ShapeDtypeStruct((n, X_W), F32), jax.ShapeDtypeStruct((n, X_W), F32)],
        compiler_params=_params(("parallel",)),
        name="memkv",
    )(mem, g, w)


def _attend(q, k, v):
    s = _dot_nt(q, k.astype(BF16)) * (X_DH ** -0.5)
    p = jnp.exp(s - jnp.max(s, axis=-1, keepdims=True))
    p = p / jnp.sum(p, axis=-1, keepdims=True)
    return _dot(p.astype(BF16), v.astype(BF16))


def _xattn_prompt_kernel(q_ref, k_ref, v_ref, o_ref):
    for h in range(X_HEADS):
        sl = slice(h * X_DH, (h + 1) * X_DH)
        o_ref[:, sl] = _attend(q_ref[:, sl], k_ref[:, sl], v_ref[:, sl]).astype(BF16)


def _xattn_prompt(pb, k, v, batch, seq):
    tq = TOK_TILE
    nt = seq // tq
    qblk = 3 * M_W // X_W
    return pl.pallas_call(
        _xattn_prompt_kernel,
        grid=(batch, nt),
        in_specs=[
            pl.BlockSpec((tq, X_W), lambda b, t: (b * nt + t, qblk)),
            pl.BlockSpec((N_MEM, X_W), lambda b, t: (b, 0)),
            pl.BlockSpec((N_MEM, X_W), lambda b, t: (b, 0)),
        ],
        out_specs=pl.BlockSpec((tq, X_W), lambda b, t: (b * nt + t, 0)),
        out_shape=jax.ShapeDtypeStruct((batch * seq, X_W), BF16),
        compiler_params=_params(("parallel", "parallel")),
        name="xattn_prompt",
    )(pb, k, v)


def _xattn_sample_kernel(q_ref, k_ref, v_ref, o_ref, *, seq):
    R = GROUP_B * seq
    assert R == 16 and X_HEADS == 4
    rblk = _seq_index(lax.broadcasted_iota(jnp.int32, (R, 1), 0), seq)
    nk = N_MEM * X_HEADS
    row_head = lax.broadcasted_iota(jnp.int32, (X_HEADS * R, nk), 0) >> 4
    col_head = lax.broadcasted_iota(jnp.int32, (X_HEADS * R, nk), 1) & (X_HEADS - 1)
    own = row_head == col_head
    q4 = jnp.concatenate([q_ref[:, h * X_DH:(h + 1) * X_DH] for h in range(X_HEADS)], axis=0)
    outs = [jnp.zeros((R, X_DH), F32) for _ in range(X_HEADS)]
    for j in range(GROUP_B):
        s = _dot_nt(q4, k_ref[j].astype(BF16)) * (X_DH ** -0.5)
        s = jnp.where(own, s, -jnp.inf)
        p = jnp.exp(s - jnp.max(s, axis=-1, keepdims=True))
        p = p / jnp.sum(p, axis=-1, keepdims=True)
        o4 = _dot(p.astype(BF16), v_ref[j].astype(BF16))
        outs = [jnp.where(rblk == j, o4[h * R:(h + 1) * R, :], outs[h]) for h in range(X_HEADS)]
    for h in range(X_HEADS):
        o_ref[:, h * X_DH:(h + 1) * X_DH] = outs[h].astype(BF16)


def _xattn_sample(pb, k, v, seq):
    batch = k.shape[0]
    R = GROUP_B * seq
    qblk = 3 * M_W // X_W
    return pl.pallas_call(
        functools.partial(_xattn_sample_kernel, seq=seq),
        grid=(batch // GROUP_B,),
        in_specs=[
            pl.BlockSpec((R, X_W), lambda i: (i, qblk)),
            pl.BlockSpec((GROUP_B, N_MEM * X_HEADS, X_DH), lambda i: (i, 0, 0)),
            pl.BlockSpec((GROUP_B, N_MEM * X_HEADS, X_DH), lambda i: (i, 0, 0)),
        ],
        out_specs=pl.BlockSpec((R, X_W), lambda i: (i, 0)),
        out_shape=jax.ShapeDtypeStruct((batch * seq, X_W), BF16),
        compiler_params=_params(("parallel",)),
        name="xattn_sample",
    )(pb, k, v)


def _outproj_kernel(x_ref, hm_ref, yc_ref, ox_ref, w_ref, o_ref):
    acc = x_ref[...]
    acc = acc + _dot(hm_ref[...], w_ref[0:M_W, :])
    acc = acc + _dot(yc_ref[...], w_ref[M_W:M_W + C_W, :])
    acc = acc + _dot(ox_ref[...], w_ref[M_W + C_W:, :])
    o_ref[...] = acc


def _outproj(x, hm, yc, ox, w):
    n = x.shape[0]
    tm = TOK_TILE
    return pl.pallas_call(
        _outproj_kernel,
        grid=(n // tm,),
        in_specs=[
            pl.BlockSpec((tm, D_MODEL), lambda i: (i, 0)),
            pl.BlockSpec((tm, M_W), lambda i: (i, 0)),
            pl.BlockSpec((tm, C_W), lambda i: (i, 0)),
            pl.BlockSpec((tm, X_W), lambda i: (i, 0)),
            _const_spec(w.shape),
        ],
        out_specs=pl.BlockSpec((tm, D_MODEL), lambda i: (i, 0)),
        out_shape=jax.ShapeDtypeStruct((n, D_MODEL), F32),
        compiler_params=_params(("parallel",)),
        name="outproj",
    )(x, hm, yc, ox, w)


def _sort16_pairs():
    n, pairs, p = P_TOPK, [], 1
    while p < n:
        k = p
        while k >= 1:
            for j in range(k % p, n - k, 2 * k):
                for i in range(min(k, n - j - k)):
                    if (i + j) // (2 * p) == (i + j + k) // (2 * p):
                        pairs.append((i + j, i + j + k))
            k //= 2
        p *= 2
    return pairs


_SORT16 = _sort16_pairs()
_BITONIC16 = [(i, i + d) for d in (8, 4, 2, 1) for i in range(P_TOPK) if not i & d]
_CAND_ROW_LEN = [P_TOPK // (p + 1) for p in range(P_TOPK)]


def _exchange(xs, pairs):
    for a, b in pairs:
        hi = jnp.maximum(xs[a], xs[b])
        lo = jnp.minimum(xs[a], xs[b])
        xs[a], xs[b] = hi, lo
    return xs


def _merge_top16(xs, ys):
    xs = list(xs)
    for r, y in enumerate(ys):
        xs[P_TOPK - 1 - r] = jnp.maximum(xs[P_TOPK - 1 - r], y)
    return _exchange(xs, _BITONIC16)


def _top16_sorted(st):
    t = st.shape[1]
    x3 = st.reshape(P_NKEYS // SUBLANES, SUBLANES, t)
    xs = _exchange([x3[g] for g in range(P_TOPK)], _SORT16)
    for shift in (4, 2, 1):
        xs = _merge_top16(xs, [pltpu.roll(x, shift, 0) for x in xs])
    return xs


def _theta_and_z(a, b):
    top = [a[0] + b[q] for q in range(P_TOPK)]
    for p in range(1, P_TOPK):
        top = _merge_top16(top, [a[p] + b[q] for q in range(_CAND_ROW_LEN[p])])
    z = jnp.ones_like(top[0])
    for r in range(1, P_TOPK):
        z = z + jnp.exp(top[r] - top[0])
    return top[P_TOPK - 1], z


def _peer_kernel(x_ref, g2_ref, gf_ref, wqt_ref, sk_ref, u_ref, vt_ref, y_ref,
                 xnt_s, s1_s, s2_s, f2_s, th_s, a0_s, rz_s, acc_s, wg_s, stage_s):
    e = pl.program_id(1)
    tm = x_ref.shape[0]
    nsb = u_ref.shape[0] // P_NKEYS

    def prologue():
        xn = _rms(x_ref[...], g2_ref[...])
        xnt = jnp.transpose(xn).astype(BF16)
        xnt_s[...] = xnt
        qt = _dot(wqt_ref[...], xnt).astype(BF16)
        for h in range(P_HEADS):
            for c, s_s in enumerate((s1_s, s2_s)):
                r0 = (2 * h + c) * P_NKEYS
                st = _dot(sk_ref[2 * h + c], qt[r0:r0 + P_NKEYS, :])
                for tc in range(tm // LANES):
                    s_s[h, tc] = st[:, tc * LANES:(tc + 1) * LANES]
        sub = lax.broadcasted_iota(jnp.int32, (SUBLANES, LANES), 0)
        for tc in range(tm // LANES):
            ls = slice(tc * LANES, (tc + 1) * LANES)
            tops = []
            for s_s in (s1_s, s2_s):
                top = [jnp.zeros((SUBLANES, LANES), F32) for _ in range(P_TOPK)]
                for h in range(P_HEADS):
                    srt = _top16_sorted(s_s[h, tc])
                    top = [jnp.where(sub == h, srt[r], top[r]) for r in range(P_TOPK)]
                tops.append(top)
            theta, z = _theta_and_z(tops[0], tops[1])
            th_s[:, ls] = theta
            a0_s[:, ls] = tops[0][0]
            rz_s[:, ls] = 0.5 / z
            b0 = tops[1][0]
            for h in range(P_HEADS):
                f2_s[h, tc] = jnp.exp(s2_s[h, tc] - b0[h:h + 1, :])
        acc_s[...] = jnp.zeros_like(acc_s)

    def gate_times_gelu(at, sb):
        i = e * nsb + sb
        rows = slice(sb * P_NKEYS, (sb + 1) * P_NKEYS)
        for tc in range(tm // LANES):
            ls = slice(tc * LANES, (tc + 1) * LANES)
            w = jnp.zeros((P_NKEYS, LANES), F32)
            for h in range(P_HEADS):
                s1row = s1_s[h, tc, pl.ds(i, 1), :]
                f1row = jnp.exp(s1row - a0_s[h:h + 1, ls]) * rz_s[h:h + 1, ls]
                hit = (s2_s[h, tc] + s1row) >= th_s[h:h + 1, ls]
                w = w + jnp.where(hit, f2_s[h, tc], 0.0) * f1row
            a = at[rows, ls]
            inner = a * (0.7978845608028654 + 0.035677408136300125 * (a * a))
            stage_s[:, ls] = w * (a + a * jnp.tanh(inner))
        wg_s[rows, :] = stage_s[...].astype(BF16)

    @pl.when(e == 0)
    def _():
        prologue()

    at = _dot(u_ref[...], xnt_s[...])
    for sb in range(nsb):
        gate_times_gelu(at, sb)
    acc_s[...] += _dot(vt_ref[...], wg_s[...])

    @pl.when(e == pl.num_programs(1) - 1)
    def _():
        y = x_ref[...] + jnp.transpose(acc_s[...])
        y_ref[...] = _rms(y, gf_ref[...])


def _peer(x1, g2, gf, wqt, sk, u, vt, tm):
    n = x1.shape[0]
    te = PEER_TE
    chunked = (P_HEADS, tm // LANES, P_NKEYS, LANES)
    return pl.pallas_call(
        _peer_kernel,
        grid=(n // tm, P_NEXP // te),
        in_specs=[
            pl.BlockSpec((tm, D_MODEL), lambda i, e: (i, 0), pipeline_mode=pl.Buffered(1)),
            _const_spec((1, D_MODEL)), _const_spec((1, D_MODEL)),
            _const_spec(wqt.shape), _const_spec(sk.shape),
            pl.BlockSpec((te, D_MODEL), lambda i, e: (e, 0)),
            pl.BlockSpec((D_MODEL, te), lambda i, e: (0, e)),
        ],
        out_specs=pl.BlockSpec((tm, D_MODEL), lambda i, e: (i, 0)),
        out_shape=jax.ShapeDtypeStruct((n, D_MODEL), F32),
        scratch_shapes=[
            pltpu.VMEM((D_MODEL, tm), BF16),
            pltpu.VMEM(chunked, F32),
            pltpu.VMEM(chunked, F32),
            pltpu.VMEM(chunked, F32),
            pltpu.VMEM((SUBLANES, tm), F32),
            pltpu.VMEM((SUBLANES, tm), F32),
            pltpu.VMEM((SUBLANES, tm), F32),
            pltpu.VMEM((D_MODEL, tm), F32),
            pltpu.VMEM((te, tm), BF16),
            pltpu.VMEM((P_NKEYS, tm), F32),
        ],
        compiler_params=_params(("parallel", "arbitrary")),
        name="peer",
    )(x1, g2, gf, wqt, sk, u, vt)


def kernel(x_prompt, x_sample, mem_prompt, cache_mem_k, cache_mem_v, state_C, state_n, state_m, state_conv,
           norm1_g, w_in, b_gate, mlstm_norm_g, conv_w, conv_b, conv_ln_g, conv_ln_b, mem_norm_g, w_mk, w_mv,
           w_out, norm2_g, peer_wq, peer_subkeys, peer_u, peer_v, final_g):
    depth = w_in.shape[0]
    assert depth == 1
    bp, seq_p, _ = x_prompt.shape
    bs, seq_s, _ = x_sample.shape
    l = 0

    w = w_in[l]
    o0, g0, a0, gg0, qx0 = 3 * M_W, 4 * M_W, 4 * M_W + 2 * M_HEADS, 4 * M_W + 2 * M_HEADS + C_W, \
        4 * M_W + 2 * M_HEADS + 2 * C_W
    wb = jnp.concatenate([w[:, 0:M_W], w[:, M_W:2 * M_W] * (M_DH ** -0.5), w[:, 2 * M_W:3 * M_W],
                          w[:, qx0:qx0 + X_W]], axis=1).astype(BF16)
    wf = jnp.concatenate([w[:, o0:o0 + M_W], w[:, a0:a0 + C_W], w[:, gg0:gg0 + C_W]], axis=1).astype(BF16)
    wgate = w[:, g0:g0 + 2 * M_HEADS].astype(BF16)
    wg = jnp.pad(wgate, ((0, 0), (0, LANES - 2 * M_HEADS)))
    wgt = jnp.pad(wgate.T, ((0, 2 * SUBLANES - 2 * M_HEADS), (0, 0)))
    bgr = jnp.pad(b_gate[l][None, :], ((0, 0), (0, LANES - 2 * M_HEADS)))
    bgc = jnp.pad(b_gate[l][:, None], ((0, 2 * SUBLANES - 2 * M_HEADS), (0, 0)))
    g1 = norm1_g[l][None, :]
    gnorm = mlstm_norm_g[l].reshape(1, M_W)
    cw = jnp.pad(conv_w[l], ((0, 1), (0, 0)))
    cb, clg, clb = conv_b[l][None, :], conv_ln_g[l][None, :], conv_ln_b[l][None, :]
    wkv = jnp.concatenate([w_mk[l], w_mv[l]], axis=1).astype(BF16)
    wo = w_out[l].astype(BF16)
    wqt = peer_wq[l].T.astype(BF16)
    sk = peer_subkeys[l].reshape(2 * P_HEADS, P_NKEYS, P_NKEYS).astype(BF16)
    u = peer_u[l].astype(BF16)
    vt = peer_v[l].T.astype(BF16)
    g2 = norm2_g[l][None, :]
    gf = final_g[None, :]

    xp = x_prompt.reshape(bp * seq_p, D_MODEL)
    xs = x_sample.reshape(bs * seq_s, D_MODEL)

    pb, pf, gcol, grow = _inproj(xp, g1, wb, wf, wg, wgt, bgr, bgc)
    hm, c_p, n_p, m_p = _mlstm_prompt(pb, pf, gcol, grow, gnorm, bp, seq_p)
    yc, conv_p = _conv_prompt(pf, cw, cb, clg, clb, bp, seq_p)
    mk, mv = _memkv(mem_prompt.reshape(bp * N_MEM, D_MODEL), mem_norm_g[l][None, :], wkv)
    ox = _xattn_prompt(pb, mk, mv, bp, seq_p)
    x1 = _outproj(xp, hm, yc, ox, wo)
    y_p = _peer(x1, g2, gf, wqt, sk, u, vt, TOK_TILE)

    pb, pf, gcol, _ = _inproj(xs, g1, wb, wf, wg, wgt, bgr, bgc)
    mtok = jnp.pad(jnp.repeat(state_m[l], seq_s, axis=0), ((0, 0), (0, LANES - M_HEADS)))
    hm, c_s, n_s, mtok_new = _mlstm_sample(pb, pf, gcol, mtok, gnorm, state_C[l], state_n[l], seq_s)
    yc, conv_s = _conv_sample(pf, state_conv[l], cw, cb, clg, clb, seq_s)
    ox = _xattn_sample(pb, cache_mem_k.reshape(bs, N_MEM * X_HEADS, X_DH),
                       cache_mem_v.reshape(bs, N_MEM * X_HEADS, X_DH), seq_s)
    x1 = _outproj(xs, hm, yc, ox, wo)
    y_s = _peer(x1, g2, gf, wqt, sk, u, vt, TOK_TILE)

    m_s = mtok_new.reshape(bs, seq_s, LANES)[:, seq_s - 1, :M_HEADS]
    return (y_p.reshape(bp, seq_p, D_MODEL),
            y_s.reshape(bs, seq_s, D_MODEL),
            mk.reshape(1, bp, N_MEM, X_HEADS, X_DH),
            mv.reshape(1, bp, N_MEM, X_HEADS, X_DH),
            c_p[None], n_p[None], m_p[None, :, :M_HEADS, 0], conv_p[None],
            c_s[None], n_s[None], m_s[None], conv_s[None])
```

```python
import functools

import jax
import jax.numpy as jnp
from jax import lax
from jax.experimental import pallas as pl
from jax.experimental.pallas import tpu as pltpu

F32 = jnp.float32
BF16 = jnp.bfloat16

D_MODEL = 2048
M_HEADS = 4
M_DH = 256
M_W = M_HEADS * M_DH
C_W = 512
CONV_K = 31
CONV_PAST = CONV_K - 1
X_HEADS = 4
X_DH = 128
X_W = X_HEADS * X_DH
N_MEM = 256
P_HEADS = 8
P_NKEYS = 128
P_NEXP = P_NKEYS * P_NKEYS
P_TOPK = 16
EPS = 1e-6

SUBLANES = 8
LANES = 128
VMEM_LIMIT_BYTES = 58 * 1024 * 1024

MLSTM_L = 256
GROUP_B = 4
TOK_TILE = 512
PEER_TE = 1024
CONV_T = 256
CONV_OFF = 32


def _dot(a, b):
    return jnp.dot(a, b, preferred_element_type=F32)


def _dot_nt(a, b):
    return lax.dot_general(a, b, (((1,), (1,)), ((), ())), preferred_element_type=F32)


def _dot_tn(a, b):
    return lax.dot_general(a, b, (((0,), (0,)), ((), ())), preferred_element_type=F32)


def _dot_exact(a, b):
    return jnp.dot(a, b, preferred_element_type=F32, precision=lax.Precision.HIGHEST)


def _rms(x, g):
    return x * lax.rsqrt(jnp.mean(x * x, axis=-1, keepdims=True) + EPS) * g


def _sigmoid(x):
    return 1.0 / (1.0 + jnp.exp(-x))


def _log_sigmoid(x):
    return jnp.minimum(x, 0.0) - jnp.log1p(jnp.exp(-jnp.abs(x)))


def _params(sem):
    return pltpu.CompilerParams(dimension_semantics=sem, vmem_limit_bytes=VMEM_LIMIT_BYTES)


def _const_spec(shape):
    nd = len(shape)
    return pl.BlockSpec(shape, lambda *_: (0,) * nd, pipeline_mode=pl.Buffered(1))


def _inproj_kernel(x_ref, g_ref, wb_ref, wf_ref, wg_ref, wgt_ref, bgr_ref, bgc_ref,
                   pb_ref, pf_ref, gcol_ref, grow_ref):
    xn = _rms(x_ref[...], g_ref[...]).astype(BF16)
    for c in range(0, wb_ref.shape[1], 512):
        pb_ref[:, c:c + 512] = _dot(xn, wb_ref[:, c:c + 512]).astype(BF16)
    for c in range(0, wf_ref.shape[1], 512):
        pf_ref[:, c:c + 512] = _dot(xn, wf_ref[:, c:c + 512])
    gc = _dot(xn, wg_ref[...]) + bgr_ref[...]
    lane = lax.broadcasted_iota(jnp.int32, gc.shape, 1)
    gcol_ref[...] = jnp.where(lane >= M_HEADS, _log_sigmoid(gc), gc)
    gr = _dot_nt(wgt_ref[...], xn) + bgc_ref[...]
    row = lax.broadcasted_iota(jnp.int32, gr.shape, 0)
    grow_ref[...] = jnp.where(row >= M_HEADS, _log_sigmoid(gr), gr)[0:SUBLANES, :]


def _inproj(x, g, wb, wf, wg, wgt, bgr, bgc):
    n = x.shape[0]
    tm = TOK_TILE
    return pl.pallas_call(
        _inproj_kernel,
        grid=(n // tm,),
        in_specs=[
            pl.BlockSpec((tm, D_MODEL), lambda i: (i, 0)),
            _const_spec((1, D_MODEL)),
            _const_spec(wb.shape), _const_spec(wf.shape), _const_spec(wg.shape),
            _const_spec(wgt.shape), _const_spec(bgr.shape), _const_spec(bgc.shape),
        ],
        out_specs=[
            pl.BlockSpec((tm, wb.shape[1]), lambda i: (i, 0)),
            pl.BlockSpec((tm, wf.shape[1]), lambda i: (i, 0)),
            pl.BlockSpec((tm, LANES), lambda i: (i, 0)),
            pl.BlockSpec((SUBLANES, tm), lambda i: (0, i)),
        ],
        out_shape=[
            jax.ShapeDtypeStruct((n, wb.shape[1]), BF16),
            jax.ShapeDtypeStruct((n, wf.shape[1]), F32),
            jax.ShapeDtypeStruct((n, LANES), F32),
            jax.ShapeDtypeStruct((SUBLANES, n), F32),
        ],
        compiler_params=_params(("parallel",)),
        name="inproj",
    )(x, g, wb, wf, wg, wgt, bgr, bgc)


def _mlstm_head_out(hh, gnorm, o):
    hn = hh * lax.rsqrt(jnp.mean(hh * hh, axis=-1, keepdims=True) + EPS) * gnorm
    return (hn * _sigmoid(o)).astype(BF16)


def _mlstm_prompt_kernel(q_ref, k_ref, v_ref, o_ref, gcol_ref, grow_ref, gn_ref,
                         hm_ref, c_out_ref, n_out_ref, m_out_ref, c_s, n_s, m_s):
    c = pl.program_id(1)
    L = MLSTM_L

    @pl.when(c == 0)
    def _():
        c_s[...] = jnp.zeros_like(c_s)
        n_s[...] = jnp.zeros_like(n_s)
        m_s[...] = jnp.zeros_like(m_s)

    row = lax.broadcasted_iota(jnp.int32, (L, L), 0)
    col = lax.broadcasted_iota(jnp.int32, (L, L), 1)
    causal = col <= row
    tril = jnp.where(causal, 1.0, 0.0).astype(F32)
    triu = jnp.where(row <= col, 1.0, 0.0).astype(F32)
    gcol = gcol_ref[...]
    grow = grow_ref[...]
    bc_all = _dot_exact(tril, gcol)
    br_all = _dot_exact(grow, triu)

    for h in range(M_HEADS):
        sl = slice(h * M_DH, (h + 1) * M_DH)
        q = q_ref[:, sl]
        k = k_ref[:, sl]
        v = v_ref[:, sl]
        li_r = grow[h:h + 1, :]
        li_c = gcol[:, h:h + 1]
        b_c = bc_all[:, M_HEADS + h:M_HEADS + h + 1]
        b_r = br_all[M_HEADS + h:M_HEADS + h + 1, :]
        m_prev = m_s[h:h + 1, 0:1]
        log_d = jnp.where(causal, b_c - b_r + li_r, -jnp.inf)
        log_inter = b_c + m_prev
        m_t = jnp.maximum(log_inter, jnp.max(log_d, axis=1, keepdims=True))
        dm = jnp.exp(log_d - m_t)
        inter = jnp.exp(log_inter - m_t)
        s = _dot_nt(q, k) * dm
        cmat = c_s[h]
        nrow = n_s[h:h + 1, :]
        num = _dot(s.astype(BF16), v) + inter * _dot_nt(q, cmat.astype(BF16))
        qn = jnp.sum(q.astype(F32) * nrow, axis=1, keepdims=True)
        den = jnp.sum(s, axis=1, keepdims=True) + inter * qn
        hh = num / jnp.maximum(jnp.abs(den), jnp.exp(-m_t))
        hm_ref[:, sl] = _mlstm_head_out(hh, gn_ref[:, sl], o_ref[:, sl])

        m_new = m_t[L - 1:L, :]
        b_last = b_c[L - 1:L, :]
        w_r = jnp.exp(b_last - b_r + li_r - m_new)
        w_c = jnp.exp(b_last - b_c + li_c - m_new)
        decay = jnp.exp(b_last + m_prev - m_new)
        vw = (v.astype(F32) * w_c).astype(BF16)
        c_s[h] = decay * cmat + _dot_tn(vw, k)
        wk = _dot(jnp.broadcast_to(w_r, (SUBLANES, L)).astype(BF16), k)
        n_s[h:h + 1, :] = decay * nrow + wk[0:1, :]
        m_s[h:h + 1, :] = jnp.broadcast_to(m_new, (1, LANES))

    @pl.when(c == pl.num_programs(1) - 1)
    def _():
        c_out_ref[0] = c_s[...]
        n_out_ref[0] = n_s[0:M_HEADS, :]
        m_out_ref[0] = m_s[...]


def _mlstm_prompt(pb, pf, gcol, grow, gnorm, batch, seq):
    L = MLSTM_L
    nc = seq // L
    n = batch * seq
    rows = lambda b, c: b * nc + c
    return pl.pallas_call(
        _mlstm_prompt_kernel,
        grid=(batch, nc),
        in_specs=[
            pl.BlockSpec((L, M_W), lambda b, c: (rows(b, c), 0)),
            pl.BlockSpec((L, M_W), lambda b, c: (rows(b, c), 1)),
            pl.BlockSpec((L, M_W), lambda b, c: (rows(b, c), 2)),
            pl.BlockSpec((L, M_W), lambda b, c: (rows(b, c), 0)),
            pl.BlockSpec((L, LANES), lambda b, c: (rows(b, c), 0)),
            pl.BlockSpec((SUBLANES, L), lambda b, c: (0, rows(b, c))),
            _const_spec((1, M_W)),
        ],
        out_specs=[
            pl.BlockSpec((L, M_W), lambda b, c: (rows(b, c), 0)),
            pl.BlockSpec((1, M_HEADS, M_DH, M_DH), lambda b, c: (b, 0, 0, 0)),
            pl.BlockSpec((1, M_HEADS, M_DH), lambda b, c: (b, 0, 0)),
            pl.BlockSpec((1, SUBLANES, LANES), lambda b, c: (b, 0, 0)),
        ],
        out_shape=[
            jax.ShapeDtypeStruct((n, M_W), BF16),
            jax.ShapeDtypeStruct((batch, M_HEADS, M_DH, M_DH), F32),
            jax.ShapeDtypeStruct((batch, M_HEADS, M_DH), F32),
            jax.ShapeDtypeStruct((batch, SUBLANES, LANES), F32),
        ],
        scratch_shapes=[
            pltpu.VMEM((M_HEADS, M_DH, M_DH), F32),
            pltpu.VMEM((SUBLANES, M_DH), F32),
            pltpu.VMEM((SUBLANES, LANES), F32),
        ],
        compiler_params=_params(("parallel", "arbitrary")),
        name="mlstm_prompt",
    )(pb, pb, pb, pf, gcol, grow, gnorm)


def _seq_index(token, seq):
    assert seq & (seq - 1) == 0
    return token >> (seq.bit_length() - 1)


def _to_row(colvec, eye):
    return jnp.sum(jnp.where(eye, colvec, 0.0), axis=0, keepdims=True)


def _mlstm_sample_kernel(q_ref, k_ref, v_ref, o_ref, gcol_ref, mtok_ref, gn_ref, c_in_ref, n_in_ref,
                         hm_ref, c_out_ref, n_out_ref, mtok_out_ref, *, seq):
    R = GROUP_B * seq
    row = lax.broadcasted_iota(jnp.int32, (R, R), 0)
    col = lax.broadcasted_iota(jnp.int32, (R, R), 1)
    eye = row == col
    same = _seq_index(row, seq) == _seq_index(col, seq)
    causal = same & (col <= row)
    rblk = _seq_index(lax.broadcasted_iota(jnp.int32, (R, 1), 0), seq)
    lane = lax.broadcasted_iota(jnp.int32, (R, LANES), 1)
    gcol = gcol_ref[...]
    mtok = mtok_ref[...]
    m_out = jnp.zeros((R, LANES), F32)

    def per_block(vals):
        out = jnp.zeros((R, 1), F32)
        for j in range(GROUP_B):
            out = jnp.where(rblk == j, vals[j], out)
        return out

    for h in range(M_HEADS):
        sl = slice(h * M_DH, (h + 1) * M_DH)
        q = q_ref[:, sl]
        k = k_ref[:, sl]
        v = v_ref[:, sl]
        kf = k.astype(F32)
        li_c = gcol[:, h:h + 1]
        lf_c = gcol[:, M_HEADS + h:M_HEADS + h + 1]
        li_r = _to_row(li_c, eye)
        lf_r = _to_row(lf_c, eye)
        m_prev = mtok[:, h:h + 1]
        b_c = jnp.sum(jnp.where(causal, lf_r, 0.0), axis=1, keepdims=True)
        b_r = _to_row(b_c, eye)
        log_d = jnp.where(causal, b_c - b_r + li_r, -jnp.inf)
        log_inter = b_c + m_prev
        m_t = jnp.maximum(log_inter, jnp.max(log_d, axis=1, keepdims=True))
        dm = jnp.exp(log_d - m_t)
        inter = jnp.exp(log_inter - m_t)
        s = _dot_nt(q, k) * dm
        qc = jnp.zeros((R, M_DH), F32)
        ntok = jnp.zeros((R, M_DH), F32)
        for j in range(GROUP_B):
            qc = jnp.where(rblk == j, _dot_nt(q, c_in_ref[j, h].astype(BF16)), qc)
            ntok = jnp.where(rblk == j, n_in_ref[j, h:h + 1, :], ntok)
        num = _dot(s.astype(BF16), v) + inter * qc
        qn = jnp.sum(q.astype(F32) * ntok, axis=1, keepdims=True)
        den = jnp.sum(s, axis=1, keepdims=True) + inter * qn
        hh = num / jnp.maximum(jnp.abs(den), jnp.exp(-m_t))
        hm_ref[:, sl] = _mlstm_head_out(hh, gn_ref[:, sl], o_ref[:, sl])

        last = [j * seq + seq - 1 for j in range(GROUP_B)]
        m_new = per_block([m_t[r:r + 1, :] for r in last])
        b_last = per_block([b_c[r:r + 1, :] for r in last])
        w_c = jnp.exp(b_last - b_c + li_c - m_new)
        decay_c = jnp.exp(b_last + m_prev - m_new)
        vw = v.astype(F32) * w_c
        wk = w_c * kf
        for j in range(GROUP_B):
            r = last[j]
            decay = decay_c[r:r + 1, :]
            upd = _dot_tn(jnp.where(rblk == j, vw, 0.0).astype(BF16), k)
            c_out_ref[j, h] = decay * c_in_ref[j, h] + upd
            n_out_ref[j, h:h + 1, :] = (decay * n_in_ref[j, h:h + 1, :]
                                        + jnp.sum(jnp.where(rblk == j, wk, 0.0), axis=0, keepdims=True))
        m_out = jnp.where(lane == h, m_new, m_out)
    mtok_out_ref[...] = m_out


def _mlstm_sample(pb, pf, gcol, mtok, gnorm, state_c, state_n, seq):
    batch = state_c.shape[0]
    R = GROUP_B * seq
    n = batch * seq
    return pl.pallas_call(
        functools.partial(_mlstm_sample_kernel, seq=seq),
        grid=(batch // GROUP_B,),
        in_specs=[
            pl.BlockSpec((R, M_W), lambda i: (i, 0)),
            pl.BlockSpec((R, M_W), lambda i: (i, 1)),
            pl.BlockSpec((R, M_W), lambda i: (i, 2)),
            pl.BlockSpec((R, M_W), lambda i: (i, 0)),
            pl.BlockSpec((R, LANES), lambda i: (i, 0)),
            pl.BlockSpec((R, LANES), lambda i: (i, 0)),
            _const_spec((1, M_W)),
            pl.BlockSpec((GROUP_B, M_HEADS, M_DH, M_DH), lambda i: (i, 0, 0, 0)),
            pl.BlockSpec((GROUP_B, M_HEADS, M_DH), lambda i: (i, 0, 0)),
        ],
        out_specs=[
            pl.BlockSpec((R, M_W), lambda i: (i, 0)),
            pl.BlockSpec((GROUP_B, M_HEADS, M_DH, M_DH), lambda i: (i, 0, 0, 0)),
            pl.BlockSpec((GROUP_B, M_HEADS, M_DH), lambda i: (i, 0, 0)),
            pl.BlockSpec((R, LANES), lambda i: (i, 0)),
        ],
        out_shape=[
            jax.ShapeDtypeStruct((n, M_W), BF16),
            jax.ShapeDtypeStruct(state_c.shape, F32),
            jax.ShapeDtypeStruct(state_n.shape, F32),
            jax.ShapeDtypeStruct((n, LANES), F32),
        ],
        compiler_params=_params(("parallel",)),
        name="mlstm_sample",
    )(pb, pb, pb, pf, gcol, mtok, gnorm, state_c, state_n)


def _conv_out(y, lg, lb):
    mu = jnp.mean(y, axis=-1, keepdims=True)
    yc = y - mu
    var = jnp.mean(yc * yc, axis=-1, keepdims=True)
    z = yc * lax.rsqrt(var + EPS) * lg + lb
    return (z * _sigmoid(z)).astype(BF16)


def _conv_prompt_kernel(a_ref, g_ref, w_ref, b_ref, lg_ref, lb_ref, yc_ref, new_ref, xp_s, xsh_s):
    t = pl.program_id(1)
    T = CONV_T

    @pl.when(t == 0)
    def _():
        xp_s[0:CONV_OFF, :] = jnp.zeros((CONV_OFF, C_W), F32)

    xp_s[CONV_OFF:CONV_OFF + T, :] = a_ref[...] * _sigmoid(g_ref[...])
    base = CONV_OFF - CONV_PAST
    acc = jnp.zeros((T, C_W), F32) + b_ref[...]
    for b in range(SUBLANES):
        taps = [j for j in range(b, CONV_K, SUBLANES)]
        span = taps[-1] - b + T
        xsh_s[b, 0:span, :] = xp_s[base + b:base + b + span, :]
        for j in taps:
            acc = acc + w_ref[j:j + 1, :] * xsh_s[b, j - b:j - b + T, :]
    yc_ref[...] = _conv_out(acc, lg_ref[...], lb_ref[...])
    tail = xp_s[CONV_OFF + T - CONV_PAST:CONV_OFF + T, :]
    xp_s[base:CONV_OFF, :] = tail

    @pl.when(t == pl.num_programs(1) - 1)
    def _():
        new_ref[0] = tail


def _conv_prompt(pf, w, b, lg, lb, batch, seq):
    T = CONV_T
    nt = seq // T
    n = batch * seq
    return pl.pallas_call(
        _conv_prompt_kernel,
        grid=(batch, nt),
        in_specs=[
            pl.BlockSpec((T, C_W), lambda bi, t: (bi * nt + t, 2)),
            pl.BlockSpec((T, C_W), lambda bi, t: (bi * nt + t, 3)),
            _const_spec(w.shape), _const_spec((1, C_W)), _const_spec((1, C_W)), _const_spec((1, C_W)),
        ],
        out_specs=[
            pl.BlockSpec((T, C_W), lambda bi, t: (bi * nt + t, 0)),
            pl.BlockSpec((1, CONV_PAST, C_W), lambda bi, t: (bi, 0, 0)),
        ],
        out_shape=[
            jax.ShapeDtypeStruct((n, C_W), BF16),
            jax.ShapeDtypeStruct((batch, CONV_PAST, C_W), F32),
        ],
        scratch_shapes=[pltpu.VMEM((CONV_OFF + T, C_W), F32),
                        pltpu.VMEM((SUBLANES, CONV_OFF + T, C_W), F32)],
        compiler_params=_params(("parallel", "arbitrary")),
        name="conv_prompt",
    )(pf, pf, w, b, lg, lb)


def _conv_sample_kernel(a_ref, g_ref, past_ref, w_ref, b_ref, lg_ref, lb_ref, yc_ref, new_ref,
                        xp_s, y_s, *, seq, gb):
    u = a_ref[...] * _sigmoid(g_ref[...])
    for j in range(gb):
        xp_s[0:CONV_PAST, :] = past_ref[j]
        xp_s[CONV_PAST:CONV_PAST + seq, :] = u[j * seq:(j + 1) * seq, :]
        acc = jnp.zeros((seq, C_W), F32) + b_ref[...]
        for i in range(CONV_K):
            acc = acc + w_ref[i:i + 1, :] * xp_s[i:i + seq, :]
        y_s[j * seq:(j + 1) * seq, :] = acc
        new_ref[j] = xp_s[seq:seq + CONV_PAST, :]
    yc_ref[...] = _conv_out(y_s[...], lg_ref[...], lb_ref[...])


def _conv_sample(pf, past, w, b, lg, lb, seq):
    batch = past.shape[0]
    gb = 8
    R = gb * seq
    n = batch * seq
    return pl.pallas_call(
        functools.partial(_conv_sample_kernel, seq=seq, gb=gb),
        grid=(batch // gb,),
        in_specs=[
            pl.BlockSpec((R, C_W), lambda i: (i, 2)),
            pl.BlockSpec((R, C_W), lambda i: (i, 3)),
            pl.BlockSpec((gb, CONV_PAST, C_W), lambda i: (i, 0, 0)),
            _const_spec(w.shape), _const_spec((1, C_W)), _const_spec((1, C_W)), _const_spec((1, C_W)),
        ],
        out_specs=[
            pl.BlockSpec((R, C_W), lambda i: (i, 0)),
            pl.BlockSpec((gb, CONV_PAST, C_W), lambda i: (i, 0, 0)),
        ],
        out_shape=[
            jax.ShapeDtypeStruct((n, C_W), BF16),
            jax.ShapeDtypeStruct(past.shape, F32),
        ],
        scratch_shapes=[pltpu.VMEM((CONV_PAST + seq + 6, C_W), F32), pltpu.VMEM((R, C_W), F32)],
        compiler_params=_params(("parallel",)),
        name="conv_sample",
    )(pf, pf, past, w, b, lg, lb)


def _memkv_kernel(x_ref, g_ref, w_ref, k_ref, v_ref):
    xn = _rms(x_ref[...], g_ref[...]).astype(BF16)
    k_ref[...] = _dot(xn, w_ref[:, 0:X_W])
    v_ref[...] = _dot(xn, w_ref[:, X_W:2 * X_W])


def _memkv(mem, g, w):
    n = mem.shape[0]
    tm = TOK_TILE
    return pl.pallas_call(
        _memkv_kernel,
        grid=(n // tm,),
        in_specs=[pl.BlockSpec((tm, D_MODEL), lambda i: (i, 0)), _const_spec((1, D_MODEL)),
                  _const_spec(w.shape)],
        out_specs=[pl.BlockSpec((tm, X_W), lambda i: (i, 0)), pl.BlockSpec((tm, X_W), lambda i: (i, 0))],
        out_shape=[jax.ShapeDtypeStruct((n, X_W), F32), jax.ShapeDtypeStruct((n, X_W), F32)],
        compiler_params=_params(("parallel",)),
        name="memkv",
    )(mem, g, w)


def _attend(q, k, v):
    s = _dot_nt(q, k.astype(BF16)) * (X_DH ** -0.5)
    p = jnp.exp(s - jnp.max(s, axis=-1, keepdims=True))
    p = p / jnp.sum(p, axis=-1, keepdims=True)
    return _dot(p.astype(BF16), v.astype(BF16))


def _xattn_prompt_kernel(q_ref, k_ref, v_ref, o_ref):
    for h in range(X_HEADS):
        sl = slice(h * X_DH, (h + 1) * X_DH)
        o_ref[:, sl] = _attend(q_ref[:, sl], k_ref[:, sl], v_ref[:, sl]).astype(BF16)


def _xattn_prompt(pb, k, v, batch, seq):
    tq = TOK_TILE
    nt = seq // tq
    qblk = 3 * M_W // X_W
    return pl.pallas_call(
        _xattn_prompt_kernel,
        grid=(batch, nt),
        in_specs=[
            pl.BlockSpec((tq, X_W), lambda b, t: (b * nt + t, qblk)),
            pl.BlockSpec((N_MEM, X_W), lambda b, t: (b, 0)),
            pl.BlockSpec((N_MEM, X_W), lambda b, t: (b, 0)),
        ],
        out_specs=pl.BlockSpec((tq, X_W), lambda b, t: (b * nt + t, 0)),
        out_shape=jax.ShapeDtypeStruct((batch * seq, X_W), BF16),
        compiler_params=_params(("parallel", "parallel")),
        name="xattn_prompt",
    )(pb, k, v)


def _xattn_sample_kernel(q_ref, k_ref, v_ref, o_ref, *, seq):
    R = GROUP_B * seq
    assert R == 16 and X_HEADS == 4
    rblk = _seq_index(lax.broadcasted_iota(jnp.int32, (R, 1), 0), seq)
    nk = N_MEM * X_HEADS
    row_head = lax.broadcasted_iota(jnp.int32, (X_HEADS * R, nk), 0) >> 4
    col_head = lax.broadcasted_iota(jnp.int32, (X_HEADS * R, nk), 1) & (X_HEADS - 1)
    own = row_head == col_head
    q4 = jnp.concatenate([q_ref[:, h * X_DH:(h + 1) * X_DH] for h in range(X_HEADS)], axis=0)
    outs = [jnp.zeros((R, X_DH), F32) for _ in range(X_HEADS)]
    for j in range(GROUP_B):
        s = _dot_nt(q4, k_ref[j].astype(BF16)) * (X_DH ** -0.5)
        s = jnp.where(own, s, -jnp.inf)
        p = jnp.exp(s - jnp.max(s, axis=-1, keepdims=True))
        p = p / jnp.sum(p, axis=-1, keepdims=True)
        o4 = _dot(p.astype(BF16), v_ref[j].astype(BF16))
        outs = [jnp.where(rblk == j, o4[h * R:(h + 1) * R, :], outs[h]) for h in range(X_HEADS)]
    for h in range(X_HEADS):
        o_ref[:, h * X_DH:(h + 1) * X_DH] = outs[h].astype(BF16)


def _xattn_sample(pb, k, v, seq):
    batch = k.shape[0]
    R = GROUP_B * seq
    qblk = 3 * M_W // X_W
    return pl.pallas_call(
        functools.partial(_xattn_sample_kernel, seq=seq),
        grid=(batch // GROUP_B,),
        in_specs=[
            pl.BlockSpec((R, X_W), lambda i: (i, qblk)),
            pl.BlockSpec((GROUP_B, N_MEM * X_HEADS, X_DH), lambda i: (i, 0, 0)),
            pl.BlockSpec((GROUP_B, N_MEM * X_HEADS, X_DH), lambda i: (i, 0, 0)),
        ],
        out_specs=pl.BlockSpec((R, X_W), lambda i: (i, 0)),
        out_shape=jax.ShapeDtypeStruct((batch * seq, X_W), BF16),
        compiler_params=_params(("parallel",)),
        name="xattn_sample",
    )(pb, k, v)


def _outproj_kernel(x_ref, hm_ref, yc_ref, ox_ref, w_ref, g2_ref, wq_ref, x1_ref, xnt_ref, q_ref):
    acc = x_ref[...]
    acc = acc + _dot(hm_ref[...], w_ref[0:M_W, :])
    acc = acc + _dot(yc_ref[...], w_ref[M_W:M_W + C_W, :])
    acc = acc + _dot(ox_ref[...], w_ref[M_W + C_W:, :])
    x1_ref[...] = acc
    xn = _rms(acc, g2_ref[...])
    xnt_ref[...] = jnp.transpose(xn).astype(BF16)
    q_ref[...] = _dot(xn.astype(BF16), wq_ref[...]).astype(BF16)


def _outproj(x, hm, yc, ox, w, g2, wq):
    n = x.shape[0]
    tm = TOK_TILE
    return pl.pallas_call(
        _outproj_kernel,
        grid=(n // tm,),
        in_specs=[
            pl.BlockSpec((tm, D_MODEL), lambda i: (i, 0)),
            pl.BlockSpec((tm, M_W), lambda i: (i, 0)),
            pl.BlockSpec((tm, C_W), lambda i: (i, 0)),
            pl.BlockSpec((tm, X_W), lambda i: (i, 0)),
            _const_spec(w.shape), _const_spec((1, D_MODEL)), _const_spec(wq.shape),
        ],
        out_specs=[
            pl.BlockSpec((tm, D_MODEL), lambda i: (i, 0)),
            pl.BlockSpec((D_MODEL, tm), lambda i: (0, i)),
            pl.BlockSpec((tm, wq.shape[1]), lambda i: (i, 0)),
        ],
        out_shape=[
            jax.ShapeDtypeStruct((n, D_MODEL), F32),
            jax.ShapeDtypeStruct((D_MODEL, n), BF16),
            jax.ShapeDtypeStruct((n, wq.shape[1]), BF16),
        ],
        compiler_params=_params(("parallel",)),
        name="outproj",
    )(x, hm, yc, ox, w, g2, wq)


def _sort16_pairs():
    n, pairs, p = P_TOPK, [], 1
    while p < n:
        k = p
        while k >= 1:
            for j in range(k % p, n - k, 2 * k):
                for i in range(min(k, n - j - k)):
                    if (i + j) // (2 * p) == (i + j + k) // (2 * p):
                        pairs.append((i + j, i + j + k))
            k //= 2
        p *= 2
    return pairs


_SORT16 = _sort16_pairs()
_BITONIC16 = [(i, i + d) for d in (8, 4, 2, 1) for i in range(P_TOPK) if not i & d]
_CAND_ROW_LEN = [P_TOPK // (p + 1) for p in range(P_TOPK)]


def _exchange(xs, pairs):
    for a, b in pairs:
        hi = jnp.maximum(xs[a], xs[b])
        lo = jnp.minimum(xs[a], xs[b])
        xs[a], xs[b] = hi, lo
    return xs


def _merge_top16(xs, ys):
    xs = list(xs)
    for r, y in enumerate(ys):
        xs[P_TOPK - 1 - r] = jnp.maximum(xs[P_TOPK - 1 - r], y)
    return _exchange(xs, _BITONIC16)


def _top16_sorted(st):
    t = st.shape[1]
    x3 = st.reshape(P_NKEYS // SUBLANES, SUBLANES, t)
    xs = _exchange([x3[g] for g in range(P_TOPK)], _SORT16)
    for shift in (4, 2, 1):
        xs = _merge_top16(xs, [pltpu.roll(x, shift, 0) for x in xs])
    return xs


def _theta_and_z(a, b):
    top = [a[0] + b[q] for q in range(P_TOPK)]
    for p in range(1, P_TOPK):
        top = _merge_top16(top, [a[p] + b[q] for q in range(_CAND_ROW_LEN[p])])
    z = jnp.ones_like(top[0])
    for r in range(1, P_TOPK):
        z = z + jnp.exp(top[r] - top[0])
    return top[P_TOPK - 1], z


def _peer_kernel(x_ref, xnt_ref, q_ref, gf_ref, sk_ref, u_ref, vt_ref, y_ref,
                 s1_s, s2_s, f2_s, th_s, a0_s, rz_s, acc_s, wg_s, stage_s):
    e = pl.program_id(1)
    tm = x_ref.shape[0]
    nsb = u_ref.shape[0] // P_NKEYS

    def prologue():
        for h in range(P_HEADS):
            for c, s_s in enumerate((s1_s, s2_s)):
                r0 = (2 * h + c) * P_NKEYS
                st = _dot_nt(sk_ref[2 * h + c], q_ref[:, r0:r0 + P_NKEYS])
                for tc in range(tm // LANES):
                    s_s[h, tc] = st[:, tc * LANES:(tc + 1) * LANES]
        sub = lax.broadcasted_iota(jnp.int32, (SUBLANES, LANES), 0)
        for tc in range(tm // LANES):
            ls = slice(tc * LANES, (tc + 1) * LANES)
            tops = []
            for s_s in (s1_s, s2_s):
                top = [jnp.zeros((SUBLANES, LANES), F32) for _ in range(P_TOPK)]
                for h in range(P_HEADS):
                    srt = _top16_sorted(s_s[h, tc])
                    top = [jnp.where(sub == h, srt[r], top[r]) for r in range(P_TOPK)]
                tops.append(top)
            theta, z = _theta_and_z(tops[0], tops[1])
            th_s[:, ls] = theta
            a0_s[:, ls] = tops[0][0]
            rz_s[:, ls] = 0.5 / z
            b0 = tops[1][0]
            for h in range(P_HEADS):
                f2_s[h, tc] = jnp.exp(s2_s[h, tc] - b0[h:h + 1, :])
        acc_s[...] = jnp.zeros_like(acc_s)

    def gate_times_gelu(at, sb):
        i = e * nsb + sb
        rows = slice(sb * P_NKEYS, (sb + 1) * P_NKEYS)
        for tc in range(tm // LANES):
            ls = slice(tc * LANES, (tc + 1) * LANES)
            w = jnp.zeros((P_NKEYS, LANES), F32)
            for h in range(P_HEADS):
                s1row = s1_s[h, tc, pl.ds(i, 1), :]
                f1row = jnp.exp(s1row - a0_s[h:h + 1, ls]) * rz_s[h:h + 1, ls]
                hit = (s2_s[h, tc] + s1row) >= th_s[h:h + 1, ls]
                w = w + jnp.where(hit, f2_s[h, tc], 0.0) * f1row
            a = at[rows, ls]
            inner = a * (0.7978845608028654 + 0.035677408136300125 * (a * a))
            stage_s[:, ls] = w * (a + a * jnp.tanh(inner))
        wg_s[rows, :] = stage_s[...].astype(BF16)

    @pl.when(e == 0)
    def _():
        prologue()

    at = _dot(u_ref[...], xnt_ref[...])
    for sb in range(nsb):
        gate_times_gelu(at, sb)
    acc_s[...] += _dot(vt_ref[...], wg_s[...])

    @pl.when(e == pl.num_programs(1) - 1)
    def _():
        y = x_ref[...] + jnp.transpose(acc_s[...])
        y_ref[...] = _rms(y, gf_ref[...])


def _peer(x1, xnt, q, gf, sk, u, vt, tm):
    n = x1.shape[0]
    te = PEER_TE
    chunked = (P_HEADS, tm // LANES, P_NKEYS, LANES)
    once = pl.Buffered(1)
    return pl.pallas_call(
        _peer_kernel,
        grid=(n // tm, P_NEXP // te),
        in_specs=[
            pl.BlockSpec((tm, D_MODEL), lambda i, e: (i, 0), pipeline_mode=once),
            pl.BlockSpec((D_MODEL, tm), lambda i, e: (0, i), pipeline_mode=once),
            pl.BlockSpec((tm, q.shape[1]), lambda i, e: (i, 0), pipeline_mode=once),
            _const_spec((1, D_MODEL)), _const_spec(sk.shape),
            pl.BlockSpec((te, D_MODEL), lambda i, e: (e, 0)),
            pl.BlockSpec((D_MODEL, te), lambda i, e: (0, e)),
        ],
        out_specs=pl.BlockSpec((tm, D_MODEL), lambda i, e: (i, 0)),
        out_shape=jax.ShapeDtypeStruct((n, D_MODEL), F32),
        scratch_shapes=[
            pltpu.VMEM(chunked, F32),
            pltpu.VMEM(chunked, F32),
            pltpu.VMEM(chunked, F32),
            pltpu.VMEM((SUBLANES, tm), F32),
            pltpu.VMEM((SUBLANES, tm), F32),
            pltpu.VMEM((SUBLANES, tm), F32),
            pltpu.VMEM((D_MODEL, tm), F32),
            pltpu.VMEM((te, tm), BF16),
            pltpu.VMEM((P_NKEYS, tm), F32),
        ],
        compiler_params=_params(("parallel", "arbitrary")),
        name="peer",
    )(x1, xnt, q, gf, sk, u, vt)


def kernel(x_prompt, x_sample, mem_prompt, cache_mem_k, cache_mem_v, state_C, state_n, state_m, state_conv,
           norm1_g, w_in, b_gate, mlstm_norm_g, conv_w, conv_b, conv_ln_g, conv_ln_b, mem_norm_g, w_mk, w_mv,
           w_out, norm2_g, peer_wq, peer_subkeys, peer_u, peer_v, final_g):
    depth = w_in.shape[0]
    assert depth == 1
    bp, seq_p, _ = x_prompt.shape
    bs, seq_s, _ = x_sample.shape
    l = 0

    w = w_in[l]
    o0, g0, a0, gg0, qx0 = 3 * M_W, 4 * M_W, 4 * M_W + 2 * M_HEADS, 4 * M_W + 2 * M_HEADS + C_W, \
        4 * M_W + 2 * M_HEADS + 2 * C_W
    wb = jnp.concatenate([w[:, 0:M_W], w[:, M_W:2 * M_W] * (M_DH ** -0.5), w[:, 2 * M_W:3 * M_W],
                          w[:, qx0:qx0 + X_W]], axis=1).astype(BF16)
    wf = jnp.concatenate([w[:, o0:o0 + M_W], w[:, a0:a0 + C_W], w[:, gg0:gg0 + C_W]], axis=1).astype(BF16)
    wgate = w[:, g0:g0 + 2 * M_HEADS].astype(BF16)
    wg = jnp.pad(wgate, ((0, 0), (0, LANES - 2 * M_HEADS)))
    wgt = jnp.pad(wgate.T, ((0, 2 * SUBLANES - 2 * M_HEADS), (0, 0)))
    bgr = jnp.pad(b_gate[l][None, :], ((0, 0), (0, LANES - 2 * M_HEADS)))
    bgc = jnp.pad(b_gate[l][:, None], ((0, 2 * SUBLANES - 2 * M_HEADS), (0, 0)))
    g1 = norm1_g[l][None, :]
    gnorm = mlstm_norm_g[l].reshape(1, M_W)
    cw = jnp.pad(conv_w[l], ((0, 1), (0, 0)))
    cb, clg, clb = conv_b[l][None, :], conv_ln_g[l][None, :], conv_ln_b[l][None, :]
    wkv = jnp.concatenate([w_mk[l], w_mv[l]], axis=1).astype(BF16)
    wo = w_out[l].astype(BF16)
    wq = peer_wq[l].astype(BF16)
    sk = peer_subkeys[l].reshape(2 * P_HEADS, P_NKEYS, P_NKEYS).astype(BF16)
    u = peer_u[l].astype(BF16)
    vt = peer_v[l].T.astype(BF16)
    g2 = norm2_g[l][None, :]
    gf = final_g[None, :]

    xp = x_prompt.reshape(bp * seq_p, D_MODEL)
    xs = x_sample.reshape(bs * seq_s, D_MODEL)

    pb, pf, gcol, grow = _inproj(xp, g1, wb, wf, wg, wgt, bgr, bgc)
    hm, c_p, n_p, m_p = _mlstm_prompt(pb, pf, gcol, grow, gnorm, bp, seq_p)
    yc, conv_p = _conv_prompt(pf, cw, cb, clg, clb, bp, seq_p)
    mk, mv = _memkv(mem_prompt.reshape(bp * N_MEM, D_MODEL), mem_norm_g[l][None, :], wkv)
    ox = _xattn_prompt(pb, mk, mv, bp, seq_p)
    x1, xnt, q = _outproj(xp, hm, yc, ox, wo, g2, wq)
    y_p = _peer(x1, xnt, q, gf, sk, u, vt, TOK_TILE)

    pb, pf, gcol, _ = _inproj(xs, g1, wb, wf, wg, wgt, bgr, bgc)
    mtok = jnp.pad(jnp.repeat(state_m[l], seq_s, axis=0), ((0, 0), (0, LANES - M_HEADS)))
    hm, c_s, n_s, mtok_new = _mlstm_sample(pb, pf, gcol, mtok, gnorm, state_C[l], state_n[l], seq_s)
    yc, conv_s = _conv_sample(pf, state_conv[l], cw, cb, clg, clb, seq_s)
    ox = _xattn_sample(pb, cache_mem_k.reshape(bs, N_MEM * X_HEADS, X_DH),
                       cache_mem_v.reshape(bs, N_MEM * X_HEADS, X_DH), seq_s)
    x1, xnt, q = _outproj(xs, hm, yc, ox, wo, g2, wq)
    y_s = _peer(x1, xnt, q, gf, sk, u, vt, TOK_TILE)

    m_s = mtok_new.reshape(bs, seq_s, LANES)[:, seq_s - 1, :M_HEADS]
    return (y_p.reshape(bp, seq_p, D_MODEL),
            y_s.reshape(bs, seq_s, D_MODEL),
            mk.reshape(1, bp, N_MEM, X_HEADS, X_DH),
            mv.reshape(1, bp, N_MEM, X_HEADS, X_DH),
            c_p[None], n_p[None], m_p[None, :, :M_HEADS, 0], conv_p[None],
            c_s[None], n_s[None], m_s[None], conv_s[None])
```

```python
import functools

import jax
import jax.numpy as jnp
from jax import lax
from jax.experimental import pallas as pl
from jax.experimental.pallas import tpu as pltpu

F32 = jnp.float32
BF16 = jnp.bfloat16

D_MODEL = 2048
M_HEADS = 4
M_DH = 256
M_W = M_HEADS * M_DH
C_W = 512
CONV_K = 31
CONV_PAST = CONV_K - 1
X_HEADS = 4
X_DH = 128
X_W = X_HEADS * X_DH
N_MEM = 256
P_HEADS = 8
P_NKEYS = 128
P_NEXP = P_NKEYS * P_NKEYS
P_TOPK = 16
EPS = 1e-6

SUBLANES = 8
LANES = 128
VMEM_LIMIT_BYTES = 58 * 1024 * 1024

MLSTM_L = 256
GROUP_B = 4
TOK_TILE = 512
PEER_TE = 1024
CONV_T = 256
CONV_OFF = 32


def _dot(a, b):
    return jnp.dot(a, b, preferred_element_type=F32)


def _dot_nt(a, b):
    return lax.dot_general(a, b, (((1,), (1,)), ((), ())), preferred_element_type=F32)


def _dot_tn(a, b):
    return lax.dot_general(a, b, (((0,), (0,)), ((), ())), preferred_element_type=F32)


def _dot_exact(a, b):
    return jnp.dot(a, b, preferred_element_type=F32, precision=lax.Precision.HIGHEST)


def _rms(x, g):
    return x * lax.rsqrt(jnp.mean(x * x, axis=-1, keepdims=True) + EPS) * g


def _sigmoid(x):
    return 1.0 / (1.0 + jnp.exp(-x))


def _log_sigmoid(x):
    return jnp.minimum(x, 0.0) - jnp.log1p(jnp.exp(-jnp.abs(x)))


def _params(sem):
    return pltpu.CompilerParams(dimension_semantics=sem, vmem_limit_bytes=VMEM_LIMIT_BYTES)


def _const_spec(shape):
    nd = len(shape)
    return pl.BlockSpec(shape, lambda *_: (0,) * nd, pipeline_mode=pl.Buffered(1))


def _inproj_kernel(x_ref, g_ref, wb_ref, wf_ref, wg_ref, wgt_ref, bgr_ref, bgc_ref,
                   pb_ref, pf_ref, gcol_ref, grow_ref):
    xn = _rms(x_ref[...], g_ref[...]).astype(BF16)
    for c in range(0, wb_ref.shape[1], 512):
        pb_ref[:, c:c + 512] = _dot(xn, wb_ref[:, c:c + 512]).astype(BF16)
    for c in range(0, wf_ref.shape[1], 512):
        pf_ref[:, c:c + 512] = _dot(xn, wf_ref[:, c:c + 512])
    gc = _dot(xn, wg_ref[...]) + bgr_ref[...]
    lane = lax.broadcasted_iota(jnp.int32, gc.shape, 1)
    gcol_ref[...] = jnp.where(lane >= M_HEADS, _log_sigmoid(gc), gc)
    gr = _dot_nt(wgt_ref[...], xn) + bgc_ref[...]
    row = lax.broadcasted_iota(jnp.int32, gr.shape, 0)
    grow_ref[...] = jnp.where(row >= M_HEADS, _log_sigmoid(gr), gr)[0:SUBLANES, :]


def _inproj(x, g, wb, wf, wg, wgt, bgr, bgc):
    n = x.shape[0]
    tm = TOK_TILE
    return pl.pallas_call(
        _inproj_kernel,
        grid=(n // tm,),
        in_specs=[
            pl.BlockSpec((tm, D_MODEL), lambda i: (i, 0)),
            _const_spec((1, D_MODEL)),
            _const_spec(wb.shape), _const_spec(wf.shape), _const_spec(wg.shape),
            _const_spec(wgt.shape), _const_spec(bgr.shape), _const_spec(bgc.shape),
        ],
        out_specs=[
            pl.BlockSpec((tm, wb.shape[1]), lambda i: (i, 0)),
            pl.BlockSpec((tm, wf.shape[1]), lambda i: (i, 0)),
            pl.BlockSpec((tm, LANES), lambda i: (i, 0)),
            pl.BlockSpec((SUBLANES, tm), lambda i: (0, i)),
        ],
        out_shape=[
            jax.ShapeDtypeStruct((n, wb.shape[1]), BF16),
            jax.ShapeDtypeStruct((n, wf.shape[1]), F32),
            jax.ShapeDtypeStruct((n, LANES), F32),
            jax.ShapeDtypeStruct((SUBLANES, n), F32),
        ],
        compiler_params=_params(("parallel",)),
        name="inproj",
    )(x, g, wb, wf, wg, wgt, bgr, bgc)


def _mlstm_head_out(hh, gnorm, o):
    hn = hh * lax.rsqrt(jnp.mean(hh * hh, axis=-1, keepdims=True) + EPS) * gnorm
    return (hn * _sigmoid(o)).astype(BF16)


def _mlstm_prompt_kernel(q_ref, k_ref, v_ref, o_ref, gcol_ref, grow_ref, gn_ref,
                         hm_ref, c_out_ref, n_out_ref, m_out_ref, c_s, n_s, m_s):
    c = pl.program_id(1)
    L = MLSTM_L

    @pl.when(c == 0)
    def _():
        c_s[...] = jnp.zeros_like(c_s)
        n_s[...] = jnp.zeros_like(n_s)
        m_s[...] = jnp.zeros_like(m_s)

    row = lax.broadcasted_iota(jnp.int32, (L, L), 0)
    col = lax.broadcasted_iota(jnp.int32, (L, L), 1)
    causal = col <= row
    tril = jnp.where(causal, 1.0, 0.0).astype(F32)
    triu = jnp.where(row <= col, 1.0, 0.0).astype(F32)
    gcol = gcol_ref[...]
    grow = grow_ref[...]
    bc_all = _dot_exact(tril, gcol)
    br_all = _dot_exact(grow, triu)

    heads = range(M_HEADS)
    sl = [slice(h * M_DH, (h + 1) * M_DH) for h in heads]
    q = [q_ref[:, sl[h]] for h in heads]
    k = [k_ref[:, sl[h]] for h in heads]
    v = [v_ref[:, sl[h]] for h in heads]
    li_r = [grow[h:h + 1, :] for h in heads]
    li_c = [gcol[:, h:h + 1] for h in heads]
    b_c = [bc_all[:, M_HEADS + h:M_HEADS + h + 1] for h in heads]
    b_r = [br_all[M_HEADS + h:M_HEADS + h + 1, :] for h in heads]
    m_prev = [m_s[h:h + 1, 0:1] for h in heads]
    cmat = [c_s[h] for h in heads]
    nrow = [n_s[h:h + 1, :] for h in heads]
    qk = [_dot_nt(q[h], k[h]) for h in heads]
    qc = [_dot_nt(q[h], cmat[h].astype(BF16)) for h in heads]
    log_d = [jnp.where(causal, b_c[h] - b_r[h] + li_r[h], -jnp.inf) for h in heads]
    log_inter = [b_c[h] + m_prev[h] for h in heads]
    m_t = [jnp.maximum(log_inter[h], jnp.max(log_d[h], axis=1, keepdims=True)) for h in heads]
    dm = [jnp.exp(log_d[h] - m_t[h]) for h in heads]
    inter = [jnp.exp(log_inter[h] - m_t[h]) for h in heads]
    s = [qk[h] * dm[h] for h in heads]
    num = [_dot(s[h].astype(BF16), v[h]) + inter[h] * qc[h] for h in heads]
    qn = [jnp.sum(q[h].astype(F32) * nrow[h], axis=1, keepdims=True) for h in heads]
    den = [jnp.sum(s[h], axis=1, keepdims=True) + inter[h] * qn[h] for h in heads]
    hh = [num[h] / jnp.maximum(jnp.abs(den[h]), jnp.exp(-m_t[h])) for h in heads]
    for h in heads:
        hm_ref[:, sl[h]] = _mlstm_head_out(hh[h], gn_ref[:, sl[h]], o_ref[:, sl[h]])

    m_new = [m_t[h][L - 1:L, :] for h in heads]
    b_last = [b_c[h][L - 1:L, :] for h in heads]
    w_r = [jnp.exp(b_last[h] - b_r[h] + li_r[h] - m_new[h]) for h in heads]
    w_c = [jnp.exp(b_last[h] - b_c[h] + li_c[h] - m_new[h]) for h in heads]
    decay = [jnp.exp(b_last[h] + m_prev[h] - m_new[h]) for h in heads]
    vw = [(v[h].astype(F32) * w_c[h]).astype(BF16) for h in heads]
    upd = [_dot_tn(vw[h], k[h]) for h in heads]
    wk = [_dot(jnp.broadcast_to(w_r[h], (SUBLANES, L)).astype(BF16), k[h]) for h in heads]
    for h in heads:
        c_s[h] = decay[h] * cmat[h] + upd[h]
        n_s[h:h + 1, :] = decay[h] * nrow[h] + wk[h][0:1, :]
        m_s[h:h + 1, :] = jnp.broadcast_to(m_new[h], (1, LANES))

    @pl.when(c == pl.num_programs(1) - 1)
    def _():
        c_out_ref[0] = c_s[...]
        n_out_ref[0] = n_s[0:M_HEADS, :]
        m_out_ref[0] = m_s[...]


def _mlstm_prompt(pb, pf, gcol, grow, gnorm, batch, seq):
    L = MLSTM_L
    nc = seq // L
    n = batch * seq
    rows = lambda b, c: b * nc + c
    return pl.pallas_call(
        _mlstm_prompt_kernel,
        grid=(batch, nc),
        in_specs=[
            pl.BlockSpec((L, M_W), lambda b, c: (rows(b, c), 0)),
            pl.BlockSpec((L, M_W), lambda b, c: (rows(b, c), 1)),
            pl.BlockSpec((L, M_W), lambda b, c: (rows(b, c), 2)),
            pl.BlockSpec((L, M_W), lambda b, c: (rows(b, c), 0)),
            pl.BlockSpec((L, LANES), lambda b, c: (rows(b, c), 0)),
            pl.BlockSpec((SUBLANES, L), lambda b, c: (0, rows(b, c))),
            _const_spec((1, M_W)),
        ],
        out_specs=[
            pl.BlockSpec((L, M_W), lambda b, c: (rows(b, c), 0)),
            pl.BlockSpec((1, M_HEADS, M_DH, M_DH), lambda b, c: (b, 0, 0, 0)),
            pl.BlockSpec((1, M_HEADS, M_DH), lambda b, c: (b, 0, 0)),
            pl.BlockSpec((1, SUBLANES, LANES), lambda b, c: (b, 0, 0)),
        ],
        out_shape=[
            jax.ShapeDtypeStruct((n, M_W), BF16),
            jax.ShapeDtypeStruct((batch, M_HEADS, M_DH, M_DH), F32),
            jax.ShapeDtypeStruct((batch, M_HEADS, M_DH), F32),
            jax.ShapeDtypeStruct((batch, SUBLANES, LANES), F32),
        ],
        scratch_shapes=[
            pltpu.VMEM((M_HEADS, M_DH, M_DH), F32),
            pltpu.VMEM((SUBLANES, M_DH), F32),
            pltpu.VMEM((SUBLANES, LANES), F32),
        ],
        compiler_params=_params(("parallel", "arbitrary")),
        name="mlstm_prompt",
    )(pb, pb, pb, pf, gcol, grow, gnorm)


def _seq_index(token, seq):
    assert seq & (seq - 1) == 0
    return token >> (seq.bit_length() - 1)


def _to_row(colvec, eye):
    return jnp.sum(jnp.where(eye, colvec, 0.0), axis=0, keepdims=True)


def _mlstm_sample_kernel(q_ref, k_ref, v_ref, o_ref, gcol_ref, mtok_ref, gn_ref, c_in_ref, n_in_ref,
                         hm_ref, c_out_ref, n_out_ref, mtok_out_ref, *, seq):
    R = GROUP_B * seq
    row = lax.broadcasted_iota(jnp.int32, (R, R), 0)
    col = lax.broadcasted_iota(jnp.int32, (R, R), 1)
    eye = row == col
    same = _seq_index(row, seq) == _seq_index(col, seq)
    causal = same & (col <= row)
    rblk = _seq_index(lax.broadcasted_iota(jnp.int32, (R, 1), 0), seq)
    lane = lax.broadcasted_iota(jnp.int32, (R, LANES), 1)
    gcol = gcol_ref[...]
    mtok = mtok_ref[...]
    m_out = jnp.zeros((R, LANES), F32)

    def per_block(vals):
        out = jnp.zeros((R, 1), F32)
        for j in range(GROUP_B):
            out = jnp.where(rblk == j, vals[j], out)
        return out

    heads = range(M_HEADS)
    group = range(GROUP_B)
    last = [j * seq + seq - 1 for j in group]
    sl = [slice(h * M_DH, (h + 1) * M_DH) for h in heads]
    q = [q_ref[:, sl[h]] for h in heads]
    k = [k_ref[:, sl[h]] for h in heads]
    v = [v_ref[:, sl[h]] for h in heads]
    qk = [_dot_nt(q[h], k[h]) for h in heads]
    qc_jh = [[_dot_nt(q[h], c_in_ref[j, h].astype(BF16)) for j in group] for h in heads]
    li_c = [gcol[:, h:h + 1] for h in heads]
    lf_c = [gcol[:, M_HEADS + h:M_HEADS + h + 1] for h in heads]
    li_r = [_to_row(li_c[h], eye) for h in heads]
    lf_r = [_to_row(lf_c[h], eye) for h in heads]
    m_prev = [mtok[:, h:h + 1] for h in heads]
    b_c = [jnp.sum(jnp.where(causal, lf_r[h], 0.0), axis=1, keepdims=True) for h in heads]
    b_r = [_to_row(b_c[h], eye) for h in heads]
    log_d = [jnp.where(causal, b_c[h] - b_r[h] + li_r[h], -jnp.inf) for h in heads]
    log_inter = [b_c[h] + m_prev[h] for h in heads]
    m_t = [jnp.maximum(log_inter[h], jnp.max(log_d[h], axis=1, keepdims=True)) for h in heads]
    dm = [jnp.exp(log_d[h] - m_t[h]) for h in heads]
    inter = [jnp.exp(log_inter[h] - m_t[h]) for h in heads]
    s = [qk[h] * dm[h] for h in heads]
    qc, ntok = [], []
    for h in heads:
        qc_h = jnp.zeros((R, M_DH), F32)
        ntok_h = jnp.zeros((R, M_DH), F32)
        for j in group:
            qc_h = jnp.where(rblk == j, qc_jh[h][j], qc_h)
            ntok_h = jnp.where(rblk == j, n_in_ref[j, h:h + 1, :], ntok_h)
        qc.append(qc_h)
        ntok.append(ntok_h)
    num = [_dot(s[h].astype(BF16), v[h]) + inter[h] * qc[h] for h in heads]
    qn = [jnp.sum(q[h].astype(F32) * ntok[h], axis=1, keepdims=True) for h in heads]
    den = [jnp.sum(s[h], axis=1, keepdims=True) + inter[h] * qn[h] for h in heads]
    hh = [num[h] / jnp.maximum(jnp.abs(den[h]), jnp.exp(-m_t[h])) for h in heads]
    for h in heads:
        hm_ref[:, sl[h]] = _mlstm_head_out(hh[h], gn_ref[:, sl[h]], o_ref[:, sl[h]])

    m_new = [per_block([m_t[h][r:r + 1, :] for r in last]) for h in heads]
    b_last = [per_block([b_c[h][r:r + 1, :] for r in last]) for h in heads]
    w_c = [jnp.exp(b_last[h] - b_c[h] + li_c[h] - m_new[h]) for h in heads]
    decay_c = [jnp.exp(b_last[h] + m_prev[h] - m_new[h]) for h in heads]
    vw = [v[h].astype(F32) * w_c[h] for h in heads]
    wk = [w_c[h] * k[h].astype(F32) for h in heads]
    upd = [[_dot_tn(jnp.where(rblk == j, vw[h], 0.0).astype(BF16), k[h]) for j in group] for h in heads]
    for h in heads:
        for j in group:
            decay = decay_c[h][last[j]:last[j] + 1, :]
            c_out_ref[j, h] = decay * c_in_ref[j, h] + upd[h][j]
            n_out_ref[j, h:h + 1, :] = (decay * n_in_ref[j, h:h + 1, :]
                                        + jnp.sum(jnp.where(rblk == j, wk[h], 0.0), axis=0, keepdims=True))
        m_out = jnp.where(lane == h, m_new[h], m_out)
    mtok_out_ref[...] = m_out


def _mlstm_sample(pb, pf, gcol, mtok, gnorm, state_c, state_n, seq):
    batch = state_c.shape[0]
    R = GROUP_B * seq
    n = batch * seq
    return pl.pallas_call(
        functools.partial(_mlstm_sample_kernel, seq=seq),
        grid=(batch // GROUP_B,),
        in_specs=[
            pl.BlockSpec((R, M_W), lambda i: (i, 0)),
            pl.BlockSpec((R, M_W), lambda i: (i, 1)),
            pl.BlockSpec((R, M_W), lambda i: (i, 2)),
            pl.BlockSpec((R, M_W), lambda i: (i, 0)),
            pl.BlockSpec((R, LANES), lambda i: (i, 0)),
            pl.BlockSpec((R, LANES), lambda i: (i, 0)),
            _const_spec((1, M_W)),
            pl.BlockSpec((GROUP_B, M_HEADS, M_DH, M_DH), lambda i: (i, 0, 0, 0)),
            pl.BlockSpec((GROUP_B, M_HEADS, M_DH), lambda i: (i, 0, 0)),
        ],
        out_specs=[
            pl.BlockSpec((R, M_W), lambda i: (i, 0)),
            pl.BlockSpec((GROUP_B, M_HEADS, M_DH, M_DH), lambda i: (i, 0, 0, 0)),
            pl.BlockSpec((GROUP_B, M_HEADS, M_DH), lambda i: (i, 0, 0)),
            pl.BlockSpec((R, LANES), lambda i: (i, 0)),
        ],
        out_shape=[
            jax.ShapeDtypeStruct((n, M_W), BF16),
            jax.ShapeDtypeStruct(state_c.shape, F32),
            jax.ShapeDtypeStruct(state_n.shape, F32),
            jax.ShapeDtypeStruct((n, LANES), F32),
        ],
        compiler_params=_params(("parallel",)),
        name="mlstm_sample",
    )(pb, pb, pb, pf, gcol, mtok, gnorm, state_c, state_n)


def _conv_out(y, lg, lb):
    mu = jnp.mean(y, axis=-1, keepdims=True)
    yc = y - mu
    var = jnp.mean(yc * yc, axis=-1, keepdims=True)
    z = yc * lax.rsqrt(var + EPS) * lg + lb
    return (z * _sigmoid(z)).astype(BF16)


def _conv_prompt_kernel(a_ref, g_ref, w_ref, b_ref, lg_ref, lb_ref, yc_ref, new_ref, xp_s, xsh_s):
    t = pl.program_id(1)
    T = CONV_T

    @pl.when(t == 0)
    def _():
        xp_s[0:CONV_OFF, :] = jnp.zeros((CONV_OFF, C_W), F32)

    xp_s[CONV_OFF:CONV_OFF + T, :] = a_ref[...] * _sigmoid(g_ref[...])
    base = CONV_OFF - CONV_PAST
    acc = jnp.zeros((T, C_W), F32) + b_ref[...]
    for b in range(SUBLANES):
        taps = [j for j in range(b, CONV_K, SUBLANES)]
        span = taps[-1] - b + T
        xsh_s[b, 0:span, :] = xp_s[base + b:base + b + span, :]
        for j in taps:
            acc = acc + w_ref[j:j + 1, :] * xsh_s[b, j - b:j - b + T, :]
    yc_ref[...] = _conv_out(acc, lg_ref[...], lb_ref[...])
    tail = xp_s[CONV_OFF + T - CONV_PAST:CONV_OFF + T, :]
    xp_s[base:CONV_OFF, :] = tail

    @pl.when(t == pl.num_programs(1) - 1)
    def _():
        new_ref[0] = tail


def _conv_prompt(pf, w, b, lg, lb, batch, seq):
    T = CONV_T
    nt = seq // T
    n = batch * seq
    return pl.pallas_call(
        _conv_prompt_kernel,
        grid=(batch, nt),
        in_specs=[
            pl.BlockSpec((T, C_W), lambda bi, t: (bi * nt + t, 2)),
            pl.BlockSpec((T, C_W), lambda bi, t: (bi * nt + t, 3)),
            _const_spec(w.shape), _const_spec((1, C_W)), _const_spec((1, C_W)), _const_spec((1, C_W)),
        ],
        out_specs=[
            pl.BlockSpec((T, C_W), lambda bi, t: (bi * nt + t, 0)),
            pl.BlockSpec((1, CONV_PAST, C_W), lambda bi, t: (bi, 0, 0)),
        ],
        out_shape=[
            jax.ShapeDtypeStruct((n, C_W), BF16),
            jax.ShapeDtypeStruct((batch, CONV_PAST, C_W), F32),
        ],
        scratch_shapes=[pltpu.VMEM((CONV_OFF + T, C_W), F32),
                        pltpu.VMEM((SUBLANES, CONV_OFF + T, C_W), F32)],
        compiler_params=_params(("parallel", "arbitrary")),
        name="conv_prompt",
    )(pf, pf, w, b, lg, lb)


def _conv_sample_kernel(a_ref, g_ref, past_ref, w_ref, b_ref, lg_ref, lb_ref, yc_ref, new_ref,
                        xp_s, y_s, *, seq, gb):
    u = a_ref[...] * _sigmoid(g_ref[...])
    for j in range(gb):
        xp_s[0:CONV_PAST, :] = past_ref[j]
        xp_s[CONV_PAST:CONV_PAST + seq, :] = u[j * seq:(j + 1) * seq, :]
        acc = jnp.zeros((seq, C_W), F32) + b_ref[...]
        for i in range(CONV_K):
            acc = acc + w_ref[i:i + 1, :] * xp_s[i:i + seq, :]
        y_s[j * seq:(j + 1) * seq, :] = acc
        new_ref[j] = xp_s[seq:seq + CONV_PAST, :]
    yc_ref[...] = _conv_out(y_s[...], lg_ref[...], lb_ref[...])


def _conv_sample(pf, past, w, b, lg, lb, seq):
    batch = past.shape[0]
    gb = 8
    R = gb * seq
    n = batch * seq
    return pl.pallas_call(
        functools.partial(_conv_sample_kernel, seq=seq, gb=gb),
        grid=(batch // gb,),
        in_specs=[
            pl.BlockSpec((R, C_W), lambda i: (i, 2)),
            pl.BlockSpec((R, C_W), lambda i: (i, 3)),
            pl.BlockSpec((gb, CONV_PAST, C_W), lambda i: (i, 0, 0)),
            _const_spec(w.shape), _const_spec((1, C_W)), _const_spec((1, C_W)), _const_spec((1, C_W)),
        ],
        out_specs=[
            pl.BlockSpec((R, C_W), lambda i: (i, 0)),
            pl.BlockSpec((gb, CONV_PAST, C_W), lambda i: (i, 0, 0)),
        ],
        out_shape=[
            jax.ShapeDtypeStruct((n, C_W), BF16),
            jax.ShapeDtypeStruct(past.shape, F32),
        ],
        scratch_shapes=[pltpu.VMEM((CONV_PAST + seq + 6, C_W), F32), pltpu.VMEM((R, C_W), F32)],
        compiler_params=_params(("parallel",)),
        name="conv_sample",
    )(pf, pf, past, w, b, lg, lb)


def _memkv_kernel(x_ref, g_ref, w_ref, k_ref, v_ref):
    xn = _rms(x_ref[...], g_ref[...]).astype(BF16)
    k_ref[...] = _dot(xn, w_ref[:, 0:X_W])
    v_ref[...] = _dot(xn, w_ref[:, X_W:2 * X_W])


def _memkv(mem, g, w):
    n = mem.shape[0]
    tm = TOK_TILE
    return pl.pallas_call(
        _memkv_kernel,
        grid=(n // tm,),
        in_specs=[pl.BlockSpec((tm, D_MODEL), lambda i: (i, 0)), _const_spec((1, D_MODEL)),
                  _const_spec(w.shape)],
        out_specs=[pl.BlockSpec((tm, X_W), lambda i: (i, 0)), pl.BlockSpec((tm, X_W), lambda i: (i, 0))],
        out_shape=[jax.ShapeDtypeStruct((n, X_W), F32), jax.ShapeDtypeStruct((n, X_W), F32)],
        compiler_params=_params(("parallel",)),
        name="memkv",
    )(mem, g, w)


def _xattn_prompt_kernel(q_ref, k_ref, v_ref, o_ref):
    heads = range(X_HEADS)
    sl = [slice(h * X_DH, (h + 1) * X_DH) for h in heads]
    s = [_dot_nt(q_ref[:, sl[h]], k_ref[:, sl[h]].astype(BF16)) * (X_DH ** -0.5) for h in heads]
    p = [jnp.exp(s[h] - jnp.max(s[h], axis=-1, keepdims=True)) for h in heads]
    p = [p[h] / jnp.sum(p[h], axis=-1, keepdims=True) for h in heads]
    for h in heads:
        o_ref[:, sl[h]] = _dot(p[h].astype(BF16), v_ref[:, sl[h]].astype(BF16)).astype(BF16)


def _xattn_prompt(pb, k, v, batch, seq):
    tq = TOK_TILE
    nt = seq // tq
    qblk = 3 * M_W // X_W
    return pl.pallas_call(
        _xattn_prompt_kernel,
        grid=(batch, nt),
        in_specs=[
            pl.BlockSpec((tq, X_W), lambda b, t: (b * nt + t, qblk)),
            pl.BlockSpec((N_MEM, X_W), lambda b, t: (b, 0)),
            pl.BlockSpec((N_MEM, X_W), lambda b, t: (b, 0)),
        ],
        out_specs=pl.BlockSpec((tq, X_W), lambda b, t: (b * nt + t, 0)),
        out_shape=jax.ShapeDtypeStruct((batch * seq, X_W), BF16),
        compiler_params=_params(("parallel", "parallel")),
        name="xattn_prompt",
    )(pb, k, v)


def _xattn_sample_kernel(q_ref, k_ref, v_ref, o_ref, *, seq):
    R = GROUP_B * seq
    assert R == 16 and X_HEADS == 4
    rblk = _seq_index(lax.broadcasted_iota(jnp.int32, (R, 1), 0), seq)
    nk = N_MEM * X_HEADS
    row_head = lax.broadcasted_iota(jnp.int32, (X_HEADS * R, nk), 0) >> 4
    col_head = lax.broadcasted_iota(jnp.int32, (X_HEADS * R, nk), 1) & (X_HEADS - 1)
    own = row_head == col_head
    q4 = jnp.concatenate([q_ref[:, h * X_DH:(h + 1) * X_DH] for h in range(X_HEADS)], axis=0)
    outs = [jnp.zeros((R, X_DH), F32) for _ in range(X_HEADS)]
    group = range(GROUP_B)
    s = [_dot_nt(q4, k_ref[j].astype(BF16)) * (X_DH ** -0.5) for j in group]
    s = [jnp.where(own, s[j], -jnp.inf) for j in group]
    p = [jnp.exp(s[j] - jnp.max(s[j], axis=-1, keepdims=True)) for j in group]
    p = [p[j] / jnp.sum(p[j], axis=-1, keepdims=True) for j in group]
    o4s = [_dot(p[j].astype(BF16), v_ref[j].astype(BF16)) for j in group]
    for j in group:
        o4 = o4s[j]
        outs = [jnp.where(rblk == j, o4[h * R:(h + 1) * R, :], outs[h]) for h in range(X_HEADS)]
    for h in range(X_HEADS):
        o_ref[:, h * X_DH:(h + 1) * X_DH] = outs[h].astype(BF16)


def _xattn_sample(pb, k, v, seq):
    batch = k.shape[0]
    R = GROUP_B * seq
    qblk = 3 * M_W // X_W
    return pl.pallas_call(
        functools.partial(_xattn_sample_kernel, seq=seq),
        grid=(batch // GROUP_B,),
        in_specs=[
            pl.BlockSpec((R, X_W), lambda i: (i, qblk)),
            pl.BlockSpec((GROUP_B, N_MEM * X_HEADS, X_DH), lambda i: (i, 0, 0)),
            pl.BlockSpec((GROUP_B, N_MEM * X_HEADS, X_DH), lambda i: (i, 0, 0)),
        ],
        out_specs=pl.BlockSpec((R, X_W), lambda i: (i, 0)),
        out_shape=jax.ShapeDtypeStruct((batch * seq, X_W), BF16),
        compiler_params=_params(("parallel",)),
        name="xattn_sample",
    )(pb, k, v)


def _outproj_kernel(x_ref, hm_ref, yc_ref, ox_ref, w_ref, g2_ref, wq_ref, x1_ref, xnt_ref, q_ref):
    acc = x_ref[...]
    acc = acc + _dot(hm_ref[...], w_ref[0:M_W, :])
    acc = acc + _dot(yc_ref[...], w_ref[M_W:M_W + C_W, :])
    acc = acc + _dot(ox_ref[...], w_ref[M_W + C_W:, :])
    x1_ref[...] = acc
    xn = _rms(acc, g2_ref[...])
    xnt_ref[...] = jnp.transpose(xn).astype(BF16)
    q_ref[...] = _dot(xn.astype(BF16), wq_ref[...]).astype(BF16)


def _outproj(x, hm, yc, ox, w, g2, wq):
    n = x.shape[0]
    tm = TOK_TILE
    return pl.pallas_call(
        _outproj_kernel,
        grid=(n // tm,),
        in_specs=[
            pl.BlockSpec((tm, D_MODEL), lambda i: (i, 0)),
            pl.BlockSpec((tm, M_W), lambda i: (i, 0)),
            pl.BlockSpec((tm, C_W), lambda i: (i, 0)),
            pl.BlockSpec((tm, X_W), lambda i: (i, 0)),
            _const_spec(w.shape), _const_spec((1, D_MODEL)), _const_spec(wq.shape),
        ],
        out_specs=[
            pl.BlockSpec((tm, D_MODEL), lambda i: (i, 0)),
            pl.BlockSpec((D_MODEL, tm), lambda i: (0, i)),
            pl.BlockSpec((tm, wq.shape[1]), lambda i: (i, 0)),
        ],
        out_shape=[
            jax.ShapeDtypeStruct((n, D_MODEL), F32),
            jax.ShapeDtypeStruct((D_MODEL, n), BF16),
            jax.ShapeDtypeStruct((n, wq.shape[1]), BF16),
        ],
        compiler_params=_params(("parallel",)),
        name="outproj",
    )(x, hm, yc, ox, w, g2, wq)


def _sort16_pairs():
    n, pairs, p = P_TOPK, [], 1
    while p < n:
        k = p
        while k >= 1:
            for j in range(k % p, n - k, 2 * k):
                for i in range(min(k, n - j - k)):
                    if (i + j) // (2 * p) == (i + j + k) // (2 * p):
                        pairs.append((i + j, i + j + k))
            k //= 2
        p *= 2
    return pairs


_SORT16 = _sort16_pairs()
_BITONIC16 = [(i, i + d) for d in (8, 4, 2, 1) for i in range(P_TOPK) if not i & d]
_CAND_ROW_LEN = [P_TOPK // (p + 1) for p in range(P_TOPK)]


def _exchange(xs, pairs):
    for a, b in pairs:
        hi = jnp.maximum(xs[a], xs[b])
        lo = jnp.minimum(xs[a], xs[b])
        xs[a], xs[b] = hi, lo
    return xs


def _merge_top16(xs, ys):
    xs = list(xs)
    for r, y in enumerate(ys):
        xs[P_TOPK - 1 - r] = jnp.maximum(xs[P_TOPK - 1 - r], y)
    return _exchange(xs, _BITONIC16)


def _top16_sorted(st):
    t = st.shape[1]
    x3 = st.reshape(P_NKEYS // SUBLANES, SUBLANES, t)
    xs = _exchange([x3[g] for g in range(P_TOPK)], _SORT16)
    for shift in (4, 2, 1):
        xs = _merge_top16(xs, [pltpu.roll(x, shift, 0) for x in xs])
    return xs


def _theta_and_z(a, b):
    top = [a[0] + b[q] for q in range(P_TOPK)]
    for p in range(1, P_TOPK):
        top = _merge_top16(top, [a[p] + b[q] for q in range(_CAND_ROW_LEN[p])])
    z = jnp.ones_like(top[0])
    for r in range(1, P_TOPK):
        z = z + jnp.exp(top[r] - top[0])
    return top[P_TOPK - 1], z


def _peer_kernel(x_ref, xnt_ref, q_ref, gf_ref, sk_ref, u_ref, vt_ref, y_ref,
                 s1_s, s2_s, f2_s, th_s, a0_s, rz_s, acc_s, wg_s, stage_s):
    e = pl.program_id(1)
    tm = x_ref.shape[0]
    nsb = u_ref.shape[0] // P_NKEYS

    def prologue():
        for h in range(P_HEADS):
            for c, s_s in enumerate((s1_s, s2_s)):
                r0 = (2 * h + c) * P_NKEYS
                st = _dot_nt(sk_ref[2 * h + c], q_ref[:, r0:r0 + P_NKEYS])
                for tc in range(tm // LANES):
                    s_s[h, tc] = st[:, tc * LANES:(tc + 1) * LANES]
        sub = lax.broadcasted_iota(jnp.int32, (SUBLANES, LANES), 0)
        for tc in range(tm // LANES):
            ls = slice(tc * LANES, (tc + 1) * LANES)
            tops = []
            for s_s in (s1_s, s2_s):
                top = [jnp.zeros((SUBLANES, LANES), F32) for _ in range(P_TOPK)]
                for h in range(P_HEADS):
                    srt = _top16_sorted(s_s[h, tc])
                    top = [jnp.where(sub == h, srt[r], top[r]) for r in range(P_TOPK)]
                tops.append(top)
            theta, z = _theta_and_z(tops[0], tops[1])
            th_s[:, ls] = theta
            a0_s[:, ls] = tops[0][0]
            rz_s[:, ls] = 0.5 / z
            b0 = tops[1][0]
            for h in range(P_HEADS):
                f2_s[h, tc] = jnp.exp(s2_s[h, tc] - b0[h:h + 1, :])
        acc_s[...] = jnp.zeros_like(acc_s)

    def gate_times_gelu(at, sb):
        i = e * nsb + sb
        rows = slice(sb * P_NKEYS, (sb + 1) * P_NKEYS)
        for tc in range(tm // LANES):
            ls = slice(tc * LANES, (tc + 1) * LANES)
            w = jnp.zeros((P_NKEYS, LANES), F32)
            for h in range(P_HEADS):
                s1row = s1_s[h, tc, pl.ds(i, 1), :]
                f1row = jnp.exp(s1row - a0_s[h:h + 1, ls]) * rz_s[h:h + 1, ls]
                hit = (s2_s[h, tc] + s1row) >= th_s[h:h + 1, ls]
                w = w + jnp.where(hit, f2_s[h, tc], 0.0) * f1row
            a = at[rows, ls]
            inner = a * (0.7978845608028654 + 0.035677408136300125 * (a * a))
            stage_s[:, ls] = w * (a + a * jnp.tanh(inner))
        wg_s[rows, :] = stage_s[...].astype(BF16)

    @pl.when(e == 0)
    def _():
        prologue()

    at = _dot(u_ref[...], xnt_ref[...])
    for sb in range(nsb):
        gate_times_gelu(at, sb)
    acc_s[...] += _dot(vt_ref[...], wg_s[...])

    @pl.when(e == pl.num_programs(1) - 1)
    def _():
        y = x_ref[...] + jnp.transpose(acc_s[...])
        y_ref[...] = _rms(y, gf_ref[...])


def _peer(x1, xnt, q, gf, sk, u, vt, tm):
    n = x1.shape[0]
    te = PEER_TE
    chunked = (P_HEADS, tm // LANES, P_NKEYS, LANES)
    once = pl.Buffered(1)
    return pl.pallas_call(
        _peer_kernel,
        grid=(n // tm, P_NEXP // te),
        in_specs=[
            pl.BlockSpec((tm, D_MODEL), lambda i, e: (i, 0), pipeline_mode=once),
            pl.BlockSpec((D_MODEL, tm), lambda i, e: (0, i), pipeline_mode=once),
            pl.BlockSpec((tm, q.shape[1]), lambda i, e: (i, 0), pipeline_mode=once),
            _const_spec((1, D_MODEL)), _const_spec(sk.shape),
            pl.BlockSpec((te, D_MODEL), lambda i, e: (e, 0)),
            pl.BlockSpec((D_MODEL, te), lambda i, e: (0, e)),
        ],
        out_specs=pl.BlockSpec((tm, D_MODEL), lambda i, e: (i, 0)),
        out_shape=jax.ShapeDtypeStruct((n, D_MODEL), F32),
        scratch_shapes=[
            pltpu.VMEM(chunked, F32),
            pltpu.VMEM(chunked, F32),
            pltpu.VMEM(chunked, F32),
            pltpu.VMEM((SUBLANES, tm), F32),
            pltpu.VMEM((SUBLANES, tm), F32),
            pltpu.VMEM((SUBLANES, tm), F32),
            pltpu.VMEM((D_MODEL, tm), F32),
            pltpu.VMEM((te, tm), BF16),
            pltpu.VMEM((P_NKEYS, tm), F32),
        ],
        compiler_params=_params(("parallel", "arbitrary")),
        name="peer",
    )(x1, xnt, q, gf, sk, u, vt)


def kernel(x_prompt, x_sample, mem_prompt, cache_mem_k, cache_mem_v, state_C, state_n, state_m, state_conv,
           norm1_g, w_in, b_gate, mlstm_norm_g, conv_w, conv_b, conv_ln_g, conv_ln_b, mem_norm_g, w_mk, w_mv,
           w_out, norm2_g, peer_wq, peer_subkeys, peer_u, peer_v, final_g):
    depth = w_in.shape[0]
    assert depth == 1
    bp, seq_p, _ = x_prompt.shape
    bs, seq_s, _ = x_sample.shape
    l = 0

    w = w_in[l]
    o0, g0, a0, gg0, qx0 = 3 * M_W, 4 * M_W, 4 * M_W + 2 * M_HEADS, 4 * M_W + 2 * M_HEADS + C_W, \
        4 * M_W + 2 * M_HEADS + 2 * C_W
    wb = jnp.concatenate([w[:, 0:M_W], w[:, M_W:2 * M_W] * (M_DH ** -0.5), w[:, 2 * M_W:3 * M_W],
                          w[:, qx0:qx0 + X_W]], axis=1).astype(BF16)
    wf = jnp.concatenate([w[:, o0:o0 + M_W], w[:, a0:a0 + C_W], w[:, gg0:gg0 + C_W]], axis=1).astype(BF16)
    wgate = w[:, g0:g0 + 2 * M_HEADS].astype(BF16)
    wg = jnp.pad(wgate, ((0, 0), (0, LANES - 2 * M_HEADS)))
    wgt = jnp.pad(wgate.T, ((0, 2 * SUBLANES - 2 * M_HEADS), (0, 0)))
    bgr = jnp.pad(b_gate[l][None, :], ((0, 0), (0, LANES - 2 * M_HEADS)))
    bgc = jnp.pad(b_gate[l][:, None], ((0, 2 * SUBLANES - 2 * M_HEADS), (0, 0)))
    g1 = norm1_g[l][None, :]
    gnorm = mlstm_norm_g[l].reshape(1, M_W)
    cw = jnp.pad(conv_w[l], ((0, 1), (0, 0)))
    cb, clg, clb = conv_b[l][None, :], conv_ln_g[l][None, :], conv_ln_b[l][None, :]
    wkv = jnp.concatenate([w_mk[l], w_mv[l]], axis=1).astype(BF16)
    wo = w_out[l].astype(BF16)
    wq = peer_wq[l].astype(BF16)
    sk = peer_subkeys[l].reshape(2 * P_HEADS, P_NKEYS, P_NKEYS).astype(BF16)
    u = peer_u[l].astype(BF16)
    vt = peer_v[l].T.astype(BF16)
    g2 = norm2_g[l][None, :]
    gf = final_g[None, :]

    xp = x_prompt.reshape(bp * seq_p, D_MODEL)
    xs = x_sample.reshape(bs * seq_s, D_MODEL)

    pb, pf, gcol, grow = _inproj(xp, g1, wb, wf, wg, wgt, bgr, bgc)
    hm, c_p, n_p, m_p = _mlstm_prompt(pb, pf, gcol, grow, gnorm, bp, seq_p)
    yc, conv_p = _conv_prompt(pf, cw, cb, clg, clb, bp, seq_p)
    mk, mv = _memkv(mem_prompt.reshape(bp * N_MEM, D_MODEL), mem_norm_g[l][None, :], wkv)
    ox = _xattn_prompt(pb, mk, mv, bp, seq_p)
    x1, xnt, q = _outproj(xp, hm, yc, ox, wo, g2, wq)
    y_p = _peer(x1, xnt, q, gf, sk, u, vt, TOK_TILE)

    pb, pf, gcol, _ = _inproj(xs, g1, wb, wf, wg, wgt, bgr, bgc)
    mtok = jnp.pad(jnp.repeat(state_m[l], seq_s, axis=0), ((0, 0), (0, LANES - M_HEADS)))
    hm, c_s, n_s, mtok_new = _mlstm_sample(pb, pf, gcol, mtok, gnorm, state_C[l], state_n[l], seq_s)
    yc, conv_s = _conv_sample(pf, state_conv[l], cw, cb, clg, clb, seq_s)
    ox = _xattn_sample(pb, cache_mem_k.reshape(bs, N_MEM * X_HEADS, X_DH),
                       cache_mem_v.reshape(bs, N_MEM * X_HEADS, X_DH), seq_s)
    x1, xnt, q = _outproj(xs, hm, yc, ox, wo, g2, wq)
    y_s = _peer(x1, xnt, q, gf, sk, u, vt, TOK_TILE)

    m_s = mtok_new.reshape(bs, seq_s, LANES)[:, seq_s - 1, :M_HEADS]
    return (y_p.reshape(bp, seq_p, D_MODEL),
            y_s.reshape(bs, seq_s, D_MODEL),
            mk.reshape(1, bp, N_MEM, X_HEADS, X_DH),
            mv.reshape(1, bp, N_MEM, X_HEADS, X_DH),
            c_p[None], n_p[None], m_p[None, :, :M_HEADS, 0], conv_p[None],
            c_s[None], n_s[None], m_s[None], conv_s[None])
```

```python
import functools

import jax
import jax.numpy as jnp
from jax import lax
from jax.experimental import pallas as pl
from jax.experimental.pallas import tpu as pltpu

F32 = jnp.float32
BF16 = jnp.bfloat16

D_MODEL = 2048
M_HEADS = 4
M_DH = 256
M_W = M_HEADS * M_DH
C_W = 512
CONV_K = 31
CONV_PAST = CONV_K - 1
X_HEADS = 4
X_DH = 128
X_W = X_HEADS * X_DH
N_MEM = 256
P_HEADS = 8
P_NKEYS = 128
P_NEXP = P_NKEYS * P_NKEYS
P_TOPK = 16
EPS = 1e-6

SUBLANES = 8
LANES = 128
VMEM_LIMIT_BYTES = 58 * 1024 * 1024

MLSTM_L = 256
GROUP_B = 4
TOK_TILE = 512
PEER_TE = 1024
PEER_PIECES = 1
CONV_T = 256
CONV_OFF = 32


def _dot(a, b):
    return jnp.dot(a, b, preferred_element_type=F32)


def _dot_nt(a, b):
    return lax.dot_general(a, b, (((1,), (1,)), ((), ())), preferred_element_type=F32)


def _dot_tn(a, b):
    return lax.dot_general(a, b, (((0,), (0,)), ((), ())), preferred_element_type=F32)


def _dot_exact(a, b):
    return jnp.dot(a, b, preferred_element_type=F32, precision=lax.Precision.HIGHEST)


def _rms(x, g):
    return x * lax.rsqrt(jnp.mean(x * x, axis=-1, keepdims=True) + EPS) * g


def _sigmoid(x):
    return 1.0 / (1.0 + jnp.exp(-x))


def _log_sigmoid(x):
    return jnp.minimum(x, 0.0) - jnp.log1p(jnp.exp(-jnp.abs(x)))


def _params(sem):
    return pltpu.CompilerParams(dimension_semantics=sem, vmem_limit_bytes=VMEM_LIMIT_BYTES)


def _const_spec(shape):
    nd = len(shape)
    return pl.BlockSpec(shape, lambda *_: (0,) * nd, pipeline_mode=pl.Buffered(1))


def _inproj_kernel(x_ref, g_ref, wb_ref, wf_ref, wg_ref, wgt_ref, bgr_ref, bgc_ref,
                   pb_ref, pf_ref, gcol_ref, grow_ref):
    xn = _rms(x_ref[...], g_ref[...]).astype(BF16)
    for c in range(0, wb_ref.shape[1], 512):
        pb_ref[:, c:c + 512] = _dot(xn, wb_ref[:, c:c + 512]).astype(BF16)
    for c in range(0, wf_ref.shape[1], 512):
        pf_ref[:, c:c + 512] = _dot(xn, wf_ref[:, c:c + 512])
    gc = _dot(xn, wg_ref[...]) + bgr_ref[...]
    lane = lax.broadcasted_iota(jnp.int32, gc.shape, 1)
    gcol_ref[...] = jnp.where(lane >= M_HEADS, _log_sigmoid(gc), gc)
    gr = _dot_nt(wgt_ref[...], xn) + bgc_ref[...]
    row = lax.broadcasted_iota(jnp.int32, gr.shape, 0)
    grow_ref[...] = jnp.where(row >= M_HEADS, _log_sigmoid(gr), gr)[0:SUBLANES, :]


def _inproj(x, g, wb, wf, wg, wgt, bgr, bgc):
    n = x.shape[0]
    tm = TOK_TILE
    return pl.pallas_call(
        _inproj_kernel,
        grid=(n // tm,),
        in_specs=[
            pl.BlockSpec((tm, D_MODEL), lambda i: (i, 0)),
            _const_spec((1, D_MODEL)),
            _const_spec(wb.shape), _const_spec(wf.shape), _const_spec(wg.shape),
            _const_spec(wgt.shape), _const_spec(bgr.shape), _const_spec(bgc.shape),
        ],
        out_specs=[
            pl.BlockSpec((tm, wb.shape[1]), lambda i: (i, 0)),
            pl.BlockSpec((tm, wf.shape[1]), lambda i: (i, 0)),
            pl.BlockSpec((tm, LANES), lambda i: (i, 0)),
            pl.BlockSpec((SUBLANES, tm), lambda i: (0, i)),
        ],
        out_shape=[
            jax.ShapeDtypeStruct((n, wb.shape[1]), BF16),
            jax.ShapeDtypeStruct((n, wf.shape[1]), F32),
            jax.ShapeDtypeStruct((n, LANES), F32),
            jax.ShapeDtypeStruct((SUBLANES, n), F32),
        ],
        compiler_params=_params(("parallel",)),
        name="inproj",
    )(x, g, wb, wf, wg, wgt, bgr, bgc)


def _mlstm_head_out(hh, gnorm, o):
    hn = hh * lax.rsqrt(jnp.mean(hh * hh, axis=-1, keepdims=True) + EPS) * gnorm
    return (hn * _sigmoid(o)).astype(BF16)


def _mlstm_prompt_kernel(q_ref, k_ref, v_ref, o_ref, gcol_ref, grow_ref, gn_ref,
                         hm_ref, c_out_ref, n_out_ref, m_out_ref, c_s, n_s, m_s):
    c = pl.program_id(1)
    L = MLSTM_L

    @pl.when(c == 0)
    def _():
        c_s[...] = jnp.zeros_like(c_s)
        n_s[...] = jnp.zeros_like(n_s)
        m_s[...] = jnp.zeros_like(m_s)

    row = lax.broadcasted_iota(jnp.int32, (L, L), 0)
    col = lax.broadcasted_iota(jnp.int32, (L, L), 1)
    causal = col <= row
    tril = jnp.where(causal, 1.0, 0.0).astype(F32)
    triu = jnp.where(row <= col, 1.0, 0.0).astype(F32)
    gcol = gcol_ref[...]
    grow = grow_ref[...]
    bc_all = _dot_exact(tril, gcol)
    br_all = _dot_exact(grow, triu)

    heads = range(M_HEADS)
    sl = [slice(h * M_DH, (h + 1) * M_DH) for h in heads]
    q = [q_ref[:, sl[h]] for h in heads]
    k = [k_ref[:, sl[h]] for h in heads]
    v = [v_ref[:, sl[h]] for h in heads]
    li_r = [grow[h:h + 1, :] for h in heads]
    li_c = [gcol[:, h:h + 1] for h in heads]
    b_c = [bc_all[:, M_HEADS + h:M_HEADS + h + 1] for h in heads]
    b_r = [br_all[M_HEADS + h:M_HEADS + h + 1, :] for h in heads]
    m_prev = [m_s[h:h + 1, 0:1] for h in heads]
    cmat = [c_s[h] for h in heads]
    nrow = [n_s[h:h + 1, :] for h in heads]
    qk = [_dot_nt(q[h], k[h]) for h in heads]
    qc = [_dot_nt(q[h], cmat[h].astype(BF16)) for h in heads]
    log_d = [jnp.where(causal, b_c[h] - b_r[h] + li_r[h], -jnp.inf) for h in heads]
    log_inter = [b_c[h] + m_prev[h] for h in heads]
    m_t = [jnp.maximum(log_inter[h], jnp.max(log_d[h], axis=1, keepdims=True)) for h in heads]
    dm = [jnp.exp(log_d[h] - m_t[h]) for h in heads]
    inter = [jnp.exp(log_inter[h] - m_t[h]) for h in heads]
    s = [qk[h] * dm[h] for h in heads]
    num = [_dot(s[h].astype(BF16), v[h]) + inter[h] * qc[h] for h in heads]
    qn = [jnp.sum(q[h].astype(F32) * nrow[h], axis=1, keepdims=True) for h in heads]
    den = [jnp.sum(s[h], axis=1, keepdims=True) + inter[h] * qn[h] for h in heads]
    hh = [num[h] / jnp.maximum(jnp.abs(den[h]), jnp.exp(-m_t[h])) for h in heads]
    for h in heads:
        hm_ref[:, sl[h]] = _mlstm_head_out(hh[h], gn_ref[:, sl[h]], o_ref[:, sl[h]])

    m_new = [m_t[h][L - 1:L, :] for h in heads]
    b_last = [b_c[h][L - 1:L, :] for h in heads]
    w_r = [jnp.exp(b_last[h] - b_r[h] + li_r[h] - m_new[h]) for h in heads]
    w_c = [jnp.exp(b_last[h] - b_c[h] + li_c[h] - m_new[h]) for h in heads]
    decay = [jnp.exp(b_last[h] + m_prev[h] - m_new[h]) for h in heads]
    vw = [(v[h].astype(F32) * w_c[h]).astype(BF16) for h in heads]
    upd = [_dot_tn(vw[h], k[h]) for h in heads]
    wk = [_dot(jnp.broadcast_to(w_r[h], (SUBLANES, L)).astype(BF16), k[h]) for h in heads]
    for h in heads:
        c_s[h] = decay[h] * cmat[h] + upd[h]
        n_s[h:h + 1, :] = decay[h] * nrow[h] + wk[h][0:1, :]
        m_s[h:h + 1, :] = jnp.broadcast_to(m_new[h], (1, LANES))

    @pl.when(c == pl.num_programs(1) - 1)
    def _():
        c_out_ref[0] = c_s[...]
        n_out_ref[0] = n_s[0:M_HEADS, :]
        m_out_ref[0] = m_s[...]


def _mlstm_prompt(pb, pf, gcol, grow, gnorm, batch, seq):
    L = MLSTM_L
    nc = seq // L
    n = batch * seq
    rows = lambda b, c: b * nc + c
    return pl.pallas_call(
        _mlstm_prompt_kernel,
        grid=(batch, nc),
        in_specs=[
            pl.BlockSpec((L, M_W), lambda b, c: (rows(b, c), 0)),
            pl.BlockSpec((L, M_W), lambda b, c: (rows(b, c), 1)),
            pl.BlockSpec((L, M_W), lambda b, c: (rows(b, c), 2)),
            pl.BlockSpec((L, M_W), lambda b, c: (rows(b, c), 0)),
            pl.BlockSpec((L, LANES), lambda b, c: (rows(b, c), 0)),
            pl.BlockSpec((SUBLANES, L), lambda b, c: (0, rows(b, c))),
            _const_spec((1, M_W)),
        ],
        out_specs=[
            pl.BlockSpec((L, M_W), lambda b, c: (rows(b, c), 0)),
            pl.BlockSpec((1, M_HEADS, M_DH, M_DH), lambda b, c: (b, 0, 0, 0)),
            pl.BlockSpec((1, M_HEADS, M_DH), lambda b, c: (b, 0, 0)),
            pl.BlockSpec((1, SUBLANES, LANES), lambda b, c: (b, 0, 0)),
        ],
        out_shape=[
            jax.ShapeDtypeStruct((n, M_W), BF16),
            jax.ShapeDtypeStruct((batch, M_HEADS, M_DH, M_DH), F32),
            jax.ShapeDtypeStruct((batch, M_HEADS, M_DH), F32),
            jax.ShapeDtypeStruct((batch, SUBLANES, LANES), F32),
        ],
        scratch_shapes=[
            pltpu.VMEM((M_HEADS, M_DH, M_DH), F32),
            pltpu.VMEM((SUBLANES, M_DH), F32),
            pltpu.VMEM((SUBLANES, LANES), F32),
        ],
        compiler_params=_params(("parallel", "arbitrary")),
        name="mlstm_prompt",
    )(pb, pb, pb, pf, gcol, grow, gnorm)


def _seq_index(token, seq):
    assert seq & (seq - 1) == 0
    return token >> (seq.bit_length() - 1)


def _to_row(colvec, eye):
    return jnp.sum(jnp.where(eye, colvec, 0.0), axis=0, keepdims=True)


def _mlstm_sample_kernel(q_ref, k_ref, v_ref, o_ref, gcol_ref, mtok_ref, gn_ref, c_in_ref, n_in_ref,
                         hm_ref, c_out_ref, n_out_ref, mtok_out_ref, *, seq):
    R = GROUP_B * seq
    row = lax.broadcasted_iota(jnp.int32, (R, R), 0)
    col = lax.broadcasted_iota(jnp.int32, (R, R), 1)
    eye = row == col
    same = _seq_index(row, seq) == _seq_index(col, seq)
    causal = same & (col <= row)
    rblk = _seq_index(lax.broadcasted_iota(jnp.int32, (R, 1), 0), seq)
    lane = lax.broadcasted_iota(jnp.int32, (R, LANES), 1)
    gcol = gcol_ref[...]
    mtok = mtok_ref[...]
    m_out = jnp.zeros((R, LANES), F32)

    def per_block(vals):
        out = jnp.zeros((R, 1), F32)
        for j in range(GROUP_B):
            out = jnp.where(rblk == j, vals[j], out)
        return out

    heads = range(M_HEADS)
    group = range(GROUP_B)
    last = [j * seq + seq - 1 for j in group]
    sl = [slice(h * M_DH, (h + 1) * M_DH) for h in heads]
    q = [q_ref[:, sl[h]] for h in heads]
    k = [k_ref[:, sl[h]] for h in heads]
    v = [v_ref[:, sl[h]] for h in heads]
    qk = [_dot_nt(q[h], k[h]) for h in heads]
    qc_jh = [[_dot_nt(q[h], c_in_ref[j, h].astype(BF16)) for j in group] for h in heads]
    li_c = [gcol[:, h:h + 1] for h in heads]
    lf_c = [gcol[:, M_HEADS + h:M_HEADS + h + 1] for h in heads]
    li_r = [_to_row(li_c[h], eye) for h in heads]
    lf_r = [_to_row(lf_c[h], eye) for h in heads]
    m_prev = [mtok[:, h:h + 1] for h in heads]
    b_c = [jnp.sum(jnp.where(causal, lf_r[h], 0.0), axis=1, keepdims=True) for h in heads]
    b_r = [_to_row(b_c[h], eye) for h in heads]
    log_d = [jnp.where(causal, b_c[h] - b_r[h] + li_r[h], -jnp.inf) for h in heads]
    log_inter = [b_c[h] + m_prev[h] for h in heads]
    m_t = [jnp.maximum(log_inter[h], jnp.max(log_d[h], axis=1, keepdims=True)) for h in heads]
    dm = [jnp.exp(log_d[h] - m_t[h]) for h in heads]
    inter = [jnp.exp(log_inter[h] - m_t[h]) for h in heads]
    s = [qk[h] * dm[h] for h in heads]
    qc, ntok = [], []
    for h in heads:
        qc_h = jnp.zeros((R, M_DH), F32)
        ntok_h = jnp.zeros((R, M_DH), F32)
        for j in group:
            qc_h = jnp.where(rblk == j, qc_jh[h][j], qc_h)
            ntok_h = jnp.where(rblk == j, n_in_ref[j, h:h + 1, :], ntok_h)
        qc.append(qc_h)
        ntok.append(ntok_h)
    num = [_dot(s[h].astype(BF16), v[h]) + inter[h] * qc[h] for h in heads]
    qn = [jnp.sum(q[h].astype(F32) * ntok[h], axis=1, keepdims=True) for h in heads]
    den = [jnp.sum(s[h], axis=1, keepdims=True) + inter[h] * qn[h] for h in heads]
    hh = [num[h] / jnp.maximum(jnp.abs(den[h]), jnp.exp(-m_t[h])) for h in heads]
    for h in heads:
        hm_ref[:, sl[h]] = _mlstm_head_out(hh[h], gn_ref[:, sl[h]], o_ref[:, sl[h]])

    m_new = [per_block([m_t[h][r:r + 1, :] for r in last]) for h in heads]
    b_last = [per_block([b_c[h][r:r + 1, :] for r in last]) for h in heads]
    w_c = [jnp.exp(b_last[h] - b_c[h] + li_c[h] - m_new[h]) for h in heads]
    decay_c = [jnp.exp(b_last[h] + m_prev[h] - m_new[h]) for h in heads]
    vw = [v[h].astype(F32) * w_c[h] for h in heads]
    wk = [w_c[h] * k[h].astype(F32) for h in heads]
    upd = [[_dot_tn(jnp.where(rblk == j, vw[h], 0.0).astype(BF16), k[h]) for j in group] for h in heads]
    for h in heads:
        for j in group:
            decay = decay_c[h][last[j]:last[j] + 1, :]
            c_out_ref[j, h] = decay * c_in_ref[j, h] + upd[h][j]
            n_out_ref[j, h:h + 1, :] = (decay * n_in_ref[j, h:h + 1, :]
                                        + jnp.sum(jnp.where(rblk == j, wk[h], 0.0), axis=0, keepdims=True))
        m_out = jnp.where(lane == h, m_new[h], m_out)
    mtok_out_ref[...] = m_out


def _mlstm_sample(pb, pf, gcol, mtok, gnorm, state_c, state_n, seq):
    batch = state_c.shape[0]
    R = GROUP_B * seq
    n = batch * seq
    return pl.pallas_call(
        functools.partial(_mlstm_sample_kernel, seq=seq),
        grid=(batch // GROUP_B,),
        in_specs=[
            pl.BlockSpec((R, M_W), lambda i: (i, 0)),
            pl.BlockSpec((R, M_W), lambda i: (i, 1)),
            pl.BlockSpec((R, M_W), lambda i: (i, 2)),
            pl.BlockSpec((R, M_W), lambda i: (i, 0)),
            pl.BlockSpec((R, LANES), lambda i: (i, 0)),
            pl.BlockSpec((R, LANES), lambda i: (i, 0)),
            _const_spec((1, M_W)),
            pl.BlockSpec((GROUP_B, M_HEADS, M_DH, M_DH), lambda i: (i, 0, 0, 0)),
            pl.BlockSpec((GROUP_B, M_HEADS, M_DH), lambda i: (i, 0, 0)),
        ],
        out_specs=[
            pl.BlockSpec((R, M_W), lambda i: (i, 0)),
            pl.BlockSpec((GROUP_B, M_HEADS, M_DH, M_DH), lambda i: (i, 0, 0, 0)),
            pl.BlockSpec((GROUP_B, M_HEADS, M_DH), lambda i: (i, 0, 0)),
            pl.BlockSpec((R, LANES), lambda i: (i, 0)),
        ],
        out_shape=[
            jax.ShapeDtypeStruct((n, M_W), BF16),
            jax.ShapeDtypeStruct(state_c.shape, F32),
            jax.ShapeDtypeStruct(state_n.shape, F32),
            jax.ShapeDtypeStruct((n, LANES), F32),
        ],
        compiler_params=_params(("parallel",)),
        name="mlstm_sample",
    )(pb, pb, pb, pf, gcol, mtok, gnorm, state_c, state_n)


def _conv_out(y, lg, lb):
    mu = jnp.mean(y, axis=-1, keepdims=True)
    yc = y - mu
    var = jnp.mean(yc * yc, axis=-1, keepdims=True)
    z = yc * lax.rsqrt(var + EPS) * lg + lb
    return (z * _sigmoid(z)).astype(BF16)


def _conv_prompt_kernel(a_ref, g_ref, w_ref, b_ref, lg_ref, lb_ref, yc_ref, new_ref, xp_s, xsh_s):
    t = pl.program_id(1)
    T = CONV_T

    @pl.when(t == 0)
    def _():
        xp_s[0:CONV_OFF, :] = jnp.zeros((CONV_OFF, C_W), F32)

    xp_s[CONV_OFF:CONV_OFF + T, :] = a_ref[...] * _sigmoid(g_ref[...])
    base = CONV_OFF - CONV_PAST
    acc = jnp.zeros((T, C_W), F32) + b_ref[...]
    for b in range(SUBLANES):
        taps = [j for j in range(b, CONV_K, SUBLANES)]
        span = taps[-1] - b + T
        xsh_s[b, 0:span, :] = xp_s[base + b:base + b + span, :]
        for j in taps:
            acc = acc + w_ref[j:j + 1, :] * xsh_s[b, j - b:j - b + T, :]
    yc_ref[...] = _conv_out(acc, lg_ref[...], lb_ref[...])
    tail = xp_s[CONV_OFF + T - CONV_PAST:CONV_OFF + T, :]
    xp_s[base:CONV_OFF, :] = tail

    @pl.when(t == pl.num_programs(1) - 1)
    def _():
        new_ref[0] = tail


def _conv_prompt(pf, w, b, lg, lb, batch, seq):
    T = CONV_T
    nt = seq // T
    n = batch * seq
    return pl.pallas_call(
        _conv_prompt_kernel,
        grid=(batch, nt),
        in_specs=[
            pl.BlockSpec((T, C_W), lambda bi, t: (bi * nt + t, 2)),
            pl.BlockSpec((T, C_W), lambda bi, t: (bi * nt + t, 3)),
            _const_spec(w.shape), _const_spec((1, C_W)), _const_spec((1, C_W)), _const_spec((1, C_W)),
        ],
        out_specs=[
            pl.BlockSpec((T, C_W), lambda bi, t: (bi * nt + t, 0)),
            pl.BlockSpec((1, CONV_PAST, C_W), lambda bi, t: (bi, 0, 0)),
        ],
        out_shape=[
            jax.ShapeDtypeStruct((n, C_W), BF16),
            jax.ShapeDtypeStruct((batch, CONV_PAST, C_W), F32),
        ],
        scratch_shapes=[pltpu.VMEM((CONV_OFF + T, C_W), F32),
                        pltpu.VMEM((SUBLANES, CONV_OFF + T, C_W), F32)],
        compiler_params=_params(("parallel", "arbitrary")),
        name="conv_prompt",
    )(pf, pf, w, b, lg, lb)


def _conv_sample_kernel(a_ref, g_ref, past_ref, w_ref, b_ref, lg_ref, lb_ref, yc_ref, new_ref,
                        xp_s, y_s, *, seq, gb):
    u = a_ref[...] * _sigmoid(g_ref[...])
    for j in range(gb):
        xp_s[0:CONV_PAST, :] = past_ref[j]
        xp_s[CONV_PAST:CONV_PAST + seq, :] = u[j * seq:(j + 1) * seq, :]
        acc = jnp.zeros((seq, C_W), F32) + b_ref[...]
        for i in range(CONV_K):
            acc = acc + w_ref[i:i + 1, :] * xp_s[i:i + seq, :]
        y_s[j * seq:(j + 1) * seq, :] = acc
        new_ref[j] = xp_s[seq:seq + CONV_PAST, :]
    yc_ref[...] = _conv_out(y_s[...], lg_ref[...], lb_ref[...])


def _conv_sample(pf, past, w, b, lg, lb, seq):
    batch = past.shape[0]
    gb = 8
    R = gb * seq
    n = batch * seq
    return pl.pallas_call(
        functools.partial(_conv_sample_kernel, seq=seq, gb=gb),
        grid=(batch // gb,),
        in_specs=[
            pl.BlockSpec((R, C_W), lambda i: (i, 2)),
            pl.BlockSpec((R, C_W), lambda i: (i, 3)),
            pl.BlockSpec((gb, CONV_PAST, C_W), lambda i: (i, 0, 0)),
            _const_spec(w.shape), _const_spec((1, C_W)), _const_spec((1, C_W)), _const_spec((1, C_W)),
        ],
        out_specs=[
            pl.BlockSpec((R, C_W), lambda i: (i, 0)),
            pl.BlockSpec((gb, CONV_PAST, C_W), lambda i: (i, 0, 0)),
        ],
        out_shape=[
            jax.ShapeDtypeStruct((n, C_W), BF16),
            jax.ShapeDtypeStruct(past.shape, F32),
        ],
        scratch_shapes=[pltpu.VMEM((CONV_PAST + seq + 6, C_W), F32), pltpu.VMEM((R, C_W), F32)],
        compiler_params=_params(("parallel",)),
        name="conv_sample",
    )(pf, pf, past, w, b, lg, lb)


def _memkv_kernel(x_ref, g_ref, w_ref, k_ref, v_ref):
    xn = _rms(x_ref[...], g_ref[...]).astype(BF16)
    k_ref[...] = _dot(xn, w_ref[:, 0:X_W])
    v_ref[...] = _dot(xn, w_ref[:, X_W:2 * X_W])


def _memkv(mem, g, w):
    n = mem.shape[0]
    tm = TOK_TILE
    return pl.pallas_call(
        _memkv_kernel,
        grid=(n // tm,),
        in_specs=[pl.BlockSpec((tm, D_MODEL), lambda i: (i, 0)), _const_spec((1, D_MODEL)),
                  _const_spec(w.shape)],
        out_specs=[pl.BlockSpec((tm, X_W), lambda i: (i, 0)), pl.BlockSpec((tm, X_W), lambda i: (i, 0))],
        out_shape=[jax.ShapeDtypeStruct((n, X_W), F32), jax.ShapeDtypeStruct((n, X_W), F32)],
        compiler_params=_params(("parallel",)),
        name="memkv",
    )(mem, g, w)


def _xattn_prompt_kernel(q_ref, k_ref, v_ref, o_ref):
    heads = range(X_HEADS)
    sl = [slice(h * X_DH, (h + 1) * X_DH) for h in heads]
    s = [_dot_nt(q_ref[:, sl[h]], k_ref[:, sl[h]].astype(BF16)) * (X_DH ** -0.5) for h in heads]
    p = [jnp.exp(s[h] - jnp.max(s[h], axis=-1, keepdims=True)) for h in heads]
    p = [p[h] / jnp.sum(p[h], axis=-1, keepdims=True) for h in heads]
    for h in heads:
        o_ref[:, sl[h]] = _dot(p[h].astype(BF16), v_ref[:, sl[h]].astype(BF16)).astype(BF16)


def _xattn_prompt(pb, k, v, batch, seq):
    tq = TOK_TILE
    nt = seq // tq
    qblk = 3 * M_W // X_W
    return pl.pallas_call(
        _xattn_prompt_kernel,
        grid=(batch, nt),
        in_specs=[
            pl.BlockSpec((tq, X_W), lambda b, t: (b * nt + t, qblk)),
            pl.BlockSpec((N_MEM, X_W), lambda b, t: (b, 0)),
            pl.BlockSpec((N_MEM, X_W), lambda b, t: (b, 0)),
        ],
        out_specs=pl.BlockSpec((tq, X_W), lambda b, t: (b * nt + t, 0)),
        out_shape=jax.ShapeDtypeStruct((batch * seq, X_W), BF16),
        compiler_params=_params(("parallel", "parallel")),
        name="xattn_prompt",
    )(pb, k, v)


def _xattn_sample_kernel(q_ref, k_ref, v_ref, o_ref, *, seq):
    R = GROUP_B * seq
    assert R == 16 and X_HEADS == 4
    rblk = _seq_index(lax.broadcasted_iota(jnp.int32, (R, 1), 0), seq)
    nk = N_MEM * X_HEADS
    row_head = lax.broadcasted_iota(jnp.int32, (X_HEADS * R, nk), 0) >> 4
    col_head = lax.broadcasted_iota(jnp.int32, (X_HEADS * R, nk), 1) & (X_HEADS - 1)
    own = row_head == col_head
    q4 = jnp.concatenate([q_ref[:, h * X_DH:(h + 1) * X_DH] for h in range(X_HEADS)], axis=0)
    outs = [jnp.zeros((R, X_DH), F32) for _ in range(X_HEADS)]
    group = range(GROUP_B)
    s = [_dot_nt(q4, k_ref[j].astype(BF16)) * (X_DH ** -0.5) for j in group]
    s = [jnp.where(own, s[j], -jnp.inf) for j in group]
    p = [jnp.exp(s[j] - jnp.max(s[j], axis=-1, keepdims=True)) for j in group]
    p = [p[j] / jnp.sum(p[j], axis=-1, keepdims=True) for j in group]
    o4s = [_dot(p[j].astype(BF16), v_ref[j].astype(BF16)) for j in group]
    for j in group:
        o4 = o4s[j]
        outs = [jnp.where(rblk == j, o4[h * R:(h + 1) * R, :], outs[h]) for h in range(X_HEADS)]
    for h in range(X_HEADS):
        o_ref[:, h * X_DH:(h + 1) * X_DH] = outs[h].astype(BF16)


def _xattn_sample(pb, k, v, seq):
    batch = k.shape[0]
    R = GROUP_B * seq
    qblk = 3 * M_W // X_W
    return pl.pallas_call(
        functools.partial(_xattn_sample_kernel, seq=seq),
        grid=(batch // GROUP_B,),
        in_specs=[
            pl.BlockSpec((R, X_W), lambda i: (i, qblk)),
            pl.BlockSpec((GROUP_B, N_MEM * X_HEADS, X_DH), lambda i: (i, 0, 0)),
            pl.BlockSpec((GROUP_B, N_MEM * X_HEADS, X_DH), lambda i: (i, 0, 0)),
        ],
        out_specs=pl.BlockSpec((R, X_W), lambda i: (i, 0)),
        out_shape=jax.ShapeDtypeStruct((batch * seq, X_W), BF16),
        compiler_params=_params(("parallel",)),
        name="xattn_sample",
    )(pb, k, v)


def _outproj_kernel(x_ref, hm_ref, yc_ref, ox_ref, w_ref, g2_ref, wq_ref, x1_ref, xnt_ref, q_ref):
    acc = x_ref[...]
    acc = acc + _dot(hm_ref[...], w_ref[0:M_W, :])
    acc = acc + _dot(yc_ref[...], w_ref[M_W:M_W + C_W, :])
    acc = acc + _dot(ox_ref[...], w_ref[M_W + C_W:, :])
    x1_ref[...] = acc
    xn = _rms(acc, g2_ref[...])
    xnt_ref[...] = jnp.transpose(xn).astype(BF16)
    q_ref[...] = _dot(xn.astype(BF16), wq_ref[...]).astype(BF16)


def _outproj(x, hm, yc, ox, w, g2, wq):
    n = x.shape[0]
    tm = TOK_TILE
    return pl.pallas_call(
        _outproj_kernel,
        grid=(n // tm,),
        in_specs=[
            pl.BlockSpec((tm, D_MODEL), lambda i: (i, 0)),
            pl.BlockSpec((tm, M_W), lambda i: (i, 0)),
            pl.BlockSpec((tm, C_W), lambda i: (i, 0)),
            pl.BlockSpec((tm, X_W), lambda i: (i, 0)),
            _const_spec(w.shape), _const_spec((1, D_MODEL)), _const_spec(wq.shape),
        ],
        out_specs=[
            pl.BlockSpec((tm, D_MODEL), lambda i: (i, 0)),
            pl.BlockSpec((D_MODEL, tm), lambda i: (0, i)),
            pl.BlockSpec((tm, wq.shape[1]), lambda i: (i, 0)),
        ],
        out_shape=[
            jax.ShapeDtypeStruct((n, D_MODEL), F32),
            jax.ShapeDtypeStruct((D_MODEL, n), BF16),
            jax.ShapeDtypeStruct((n, wq.shape[1]), BF16),
        ],
        compiler_params=_params(("parallel",)),
        name="outproj",
    )(x, hm, yc, ox, w, g2, wq)


def _sort16_pairs():
    n, pairs, p = P_TOPK, [], 1
    while p < n:
        k = p
        while k >= 1:
            for j in range(k % p, n - k, 2 * k):
                for i in range(min(k, n - j - k)):
                    if (i + j) // (2 * p) == (i + j + k) // (2 * p):
                        pairs.append((i + j, i + j + k))
            k //= 2
        p *= 2
    return pairs


_SORT16 = _sort16_pairs()
_BITONIC16 = [(i, i + d) for d in (8, 4, 2, 1) for i in range(P_TOPK) if not i & d]
_CAND_ROW_LEN = [P_TOPK // (p + 1) for p in range(P_TOPK)]


def _exchange(xs, pairs):
    for a, b in pairs:
        hi = jnp.maximum(xs[a], xs[b])
        lo = jnp.minimum(xs[a], xs[b])
        xs[a], xs[b] = hi, lo
    return xs


def _merge_top16(xs, ys):
    xs = list(xs)
    for r, y in enumerate(ys):
        xs[P_TOPK - 1 - r] = jnp.maximum(xs[P_TOPK - 1 - r], y)
    return _exchange(xs, _BITONIC16)


def _top16_sorted(st):
    t = st.shape[1]
    x3 = st.reshape(P_NKEYS // SUBLANES, SUBLANES, t)
    xs = _exchange([x3[g] for g in range(P_TOPK)], _SORT16)
    for shift in (4, 2, 1):
        xs = _merge_top16(xs, [pltpu.roll(x, shift, 0) for x in xs])
    return xs


def _theta_and_z(a, b):
    top = [a[0] + b[q] for q in range(P_TOPK)]
    for p in range(1, P_TOPK):
        top = _merge_top16(top, [a[p] + b[q] for q in range(_CAND_ROW_LEN[p])])
    z = jnp.ones_like(top[0])
    for r in range(1, P_TOPK):
        z = z + jnp.exp(top[r] - top[0])
    return top[P_TOPK - 1], z


def _peer_kernel(x_ref, xnt_ref, q_ref, gf_ref, sk_ref, u_ref, vt_ref, y_ref,
                 s1_s, s2_s, f2_s, th_s, a0_s, rz_s, acc_s, wg_s, stage_s, at0_s, at1_s):
    e = pl.program_id(1)
    tm = x_ref.shape[0]
    nsb = u_ref.shape[0] // P_NKEYS

    def prologue():
        for h in range(P_HEADS):
            for c, s_s in enumerate((s1_s, s2_s)):
                r0 = (2 * h + c) * P_NKEYS
                st = _dot_nt(sk_ref[2 * h + c], q_ref[:, r0:r0 + P_NKEYS])
                for tc in range(tm // LANES):
                    s_s[h, tc] = st[:, tc * LANES:(tc + 1) * LANES]
        sub = lax.broadcasted_iota(jnp.int32, (SUBLANES, LANES), 0)
        for tc in range(tm // LANES):
            ls = slice(tc * LANES, (tc + 1) * LANES)
            tops = []
            for s_s in (s1_s, s2_s):
                top = [jnp.zeros((SUBLANES, LANES), F32) for _ in range(P_TOPK)]
                for h in range(P_HEADS):
                    srt = _top16_sorted(s_s[h, tc])
                    top = [jnp.where(sub == h, srt[r], top[r]) for r in range(P_TOPK)]
                tops.append(top)
            theta, z = _theta_and_z(tops[0], tops[1])
            th_s[:, ls] = theta
            a0_s[:, ls] = tops[0][0]
            rz_s[:, ls] = 0.5 / z
            b0 = tops[1][0]
            for h in range(P_HEADS):
                f2_s[h, tc] = jnp.exp(s2_s[h, tc] - b0[h:h + 1, :])
        acc_s[...] = jnp.zeros_like(acc_s)

    def gate_times_gelu(at, sb):
        i = (e - 1) * nsb + sb
        rows = slice(sb * P_NKEYS, (sb + 1) * P_NKEYS)
        for tc in range(tm // LANES):
            ls = slice(tc * LANES, (tc + 1) * LANES)
            w = jnp.zeros((P_NKEYS, LANES), F32)
            for h in range(P_HEADS):
                s1row = s1_s[h, tc, pl.ds(i, 1), :]
                f1row = jnp.exp(s1row - a0_s[h:h + 1, ls]) * rz_s[h:h + 1, ls]
                hit = (s2_s[h, tc] + s1row) >= th_s[h:h + 1, ls]
                w = w + jnp.where(hit, f2_s[h, tc], 0.0) * f1row
            a = at[rows, ls]
            inner = a * (0.7978845608028654 + 0.035677408136300125 * (a * a))
            stage_s[:, ls] = w * (a + a * jnp.tanh(inner))
        wg_s[rows, :] = stage_s[...].astype(BF16)

    def step(at_read, at_write):
        if at_read is not None:
            for sb in range(0, nsb // 2):
                gate_times_gelu(at_read, sb)
        if at_write is not None:
            at_write[...] = _dot(u_ref[...], xnt_ref[...])
        if at_read is not None:
            for sb in range(nsb // 2, nsb):
                gate_times_gelu(at_read, sb)
            acc_s[...] += _dot(vt_ref[...], wg_s[...])

    last = pl.num_programs(1) - 1
    steady = (e > 0) & (e < last)
    odd = (e & 1) == 1

    @pl.when(e == 0)
    def _():
        prologue()
        step(None, at0_s)

    @pl.when(steady & odd)
    def _():
        step(at0_s, at1_s)

    @pl.when(steady & jnp.logical_not(odd))
    def _():
        step(at1_s, at0_s)

    @pl.when(e == last)
    def _():
        step(at1_s if ((P_NEXP // u_ref.shape[0]) - 1) % 2 else at0_s, None)
        y = x_ref[...] + jnp.transpose(acc_s[...])
        y_ref[...] = _rms(y, gf_ref[...])


def _peer(x1, xnt, q, gf, sk, u, vt, tm):
    n = x1.shape[0]
    te = PEER_TE
    ne = P_NEXP // te
    chunked = (P_HEADS, tm // LANES, P_NKEYS, LANES)
    once = pl.Buffered(1)
    return pl.pallas_call(
        _peer_kernel,
        grid=(n // tm, ne + 1),
        in_specs=[
            pl.BlockSpec((tm, D_MODEL), lambda i, e: (i, 0), pipeline_mode=once),
            pl.BlockSpec((D_MODEL, tm), lambda i, e: (0, i), pipeline_mode=once),
            pl.BlockSpec((tm, q.shape[1]), lambda i, e: (i, 0), pipeline_mode=once),
            _const_spec((1, D_MODEL)), _const_spec(sk.shape),
            pl.BlockSpec((te, D_MODEL), lambda i, e: (jnp.minimum(e, ne - 1), 0)),
            pl.BlockSpec((D_MODEL, te), lambda i, e: (0, jnp.maximum(e - 1, 0))),
        ],
        out_specs=pl.BlockSpec((tm, D_MODEL), lambda i, e: (i, 0)),
        out_shape=jax.ShapeDtypeStruct((n, D_MODEL), F32),
        scratch_shapes=[
            pltpu.VMEM(chunked, F32),
            pltpu.VMEM(chunked, F32),
            pltpu.VMEM(chunked, F32),
            pltpu.VMEM((SUBLANES, tm), F32),
            pltpu.VMEM((SUBLANES, tm), F32),
            pltpu.VMEM((SUBLANES, tm), F32),
            pltpu.VMEM((D_MODEL, tm), F32),
            pltpu.VMEM((te, tm), BF16),
            pltpu.VMEM((P_NKEYS, tm), F32),
            pltpu.VMEM((te, tm), F32), pltpu.VMEM((te, tm), F32),
        ],
        compiler_params=_params(("parallel", "arbitrary")),
        name="peer",
    )(x1, xnt, q, gf, sk, u, vt)


def kernel(x_prompt, x_sample, mem_prompt, cache_mem_k, cache_mem_v, state_C, state_n, state_m, state_conv,
           norm1_g, w_in, b_gate, mlstm_norm_g, conv_w, conv_b, conv_ln_g, conv_ln_b, mem_norm_g, w_mk, w_mv,
           w_out, norm2_g, peer_wq, peer_subkeys, peer_u, peer_v, final_g):
    depth = w_in.shape[0]
    assert depth == 1
    bp, seq_p, _ = x_prompt.shape
    bs, seq_s, _ = x_sample.shape
    l = 0

    w = w_in[l]
    o0, g0, a0, gg0, qx0 = 3 * M_W, 4 * M_W, 4 * M_W + 2 * M_HEADS, 4 * M_W + 2 * M_HEADS + C_W, \
        4 * M_W + 2 * M_HEADS + 2 * C_W
    wb = jnp.concatenate([w[:, 0:M_W], w[:, M_W:2 * M_W] * (M_DH ** -0.5), w[:, 2 * M_W:3 * M_W],
                          w[:, qx0:qx0 + X_W]], axis=1).astype(BF16)
    wf = jnp.concatenate([w[:, o0:o0 + M_W], w[:, a0:a0 + C_W], w[:, gg0:gg0 + C_W]], axis=1).astype(BF16)
    wgate = w[:, g0:g0 + 2 * M_HEADS].astype(BF16)
    wg = jnp.pad(wgate, ((0, 0), (0, LANES - 2 * M_HEADS)))
    wgt = jnp.pad(wgate.T, ((0, 2 * SUBLANES - 2 * M_HEADS), (0, 0)))
    bgr = jnp.pad(b_gate[l][None, :], ((0, 0), (0, LANES - 2 * M_HEADS)))
    bgc = jnp.pad(b_gate[l][:, None], ((0, 2 * SUBLANES - 2 * M_HEADS), (0, 0)))
    g1 = norm1_g[l][None, :]
    gnorm = mlstm_norm_g[l].reshape(1, M_W)
    cw = jnp.pad(conv_w[l], ((0, 1), (0, 0)))
    cb, clg, clb = conv_b[l][None, :], conv_ln_g[l][None, :], conv_ln_b[l][None, :]
    wkv = jnp.concatenate([w_mk[l], w_mv[l]], axis=1).astype(BF16)
    wo = w_out[l].astype(BF16)
    wq = peer_wq[l].astype(BF16)
    sk = peer_subkeys[l].reshape(2 * P_HEADS, P_NKEYS, P_NKEYS).astype(BF16)
    u = peer_u[l].astype(BF16)
    vt = peer_v[l].T.astype(BF16)
    g2 = norm2_g[l][None, :]
    gf = final_g[None, :]

    xp = x_prompt.reshape(bp * seq_p, D_MODEL)
    xs = x_sample.reshape(bs * seq_s, D_MODEL)

    pb, pf, gcol, grow = _inproj(xp, g1, wb, wf, wg, wgt, bgr, bgc)
    hm, c_p, n_p, m_p = _mlstm_prompt(pb, pf, gcol, grow, gnorm, bp, seq_p)
    yc, conv_p = _conv_prompt(pf, cw, cb, clg, clb, bp, seq_p)
    mk, mv = _memkv(mem_prompt.reshape(bp * N_MEM, D_MODEL), mem_norm_g[l][None, :], wkv)
    ox = _xattn_prompt(pb, mk, mv, bp, seq_p)
    x1, xnt, q = _outproj(xp, hm, yc, ox, wo, g2, wq)
    y_p = _peer(x1, xnt, q, gf, sk, u, vt, TOK_TILE)

    pb, pf, gcol, _ = _inproj(xs, g1, wb, wf, wg, wgt, bgr, bgc)
    mtok = jnp.pad(jnp.repeat(state_m[l], seq_s, axis=0), ((0, 0), (0, LANES - M_HEADS)))
    hm, c_s, n_s, mtok_new = _mlstm_sample(pb, pf, gcol, mtok, gnorm, state_C[l], state_n[l], seq_s)
    yc, conv_s = _conv_sample(pf, state_conv[l], cw, cb, clg, clb, seq_s)
    ox = _xattn_sample(pb, cache_mem_k.reshape(bs, N_MEM * X_HEADS, X_DH),
                       cache_mem_v.reshape(bs, N_MEM * X_HEADS, X_DH), seq_s)
    x1, xnt, q = _outproj(xs, hm, yc, ox, wo, g2, wq)
    y_s = _peer(x1, xnt, q, gf, sk, u, vt, TOK_TILE)

    m_s = mtok_new.reshape(bs, seq_s, LANES)[:, seq_s - 1, :M_HEADS]
    return (y_p.reshape(bp, seq_p, D_MODEL),
            y_s.reshape(bs, seq_s, D_MODEL),
            mk.reshape(1, bp, N_MEM, X_HEADS, X_DH),
            mv.reshape(1, bp, N_MEM, X_HEADS, X_DH),
            c_p[None], n_p[None], m_p[None, :, :M_HEADS, 0], conv_p[None],
            c_s[None], n_s[None], m_s[None], conv_s[None])
```

```python
import functools

import jax
import jax.numpy as jnp
from jax import lax
from jax.experimental import pallas as pl
from jax.experimental.pallas import tpu as pltpu

F32 = jnp.float32
BF16 = jnp.bfloat16

D_MODEL = 2048
M_HEADS = 4
M_DH = 256
M_W = M_HEADS * M_DH
C_W = 512
CONV_K = 31
CONV_PAST = CONV_K - 1
X_HEADS = 4
X_DH = 128
X_W = X_HEADS * X_DH
N_MEM = 256
P_HEADS = 8
P_NKEYS = 128
P_NEXP = P_NKEYS * P_NKEYS
P_TOPK = 16
EPS = 1e-6

SUBLANES = 8
LANES = 128
VMEM_LIMIT_BYTES = 58 * 1024 * 1024

MLSTM_L = 256
GROUP_B = 4
TOK_TILE = 512
PEER_TE = 1024
CONV_T = 256
CONV_OFF = 32


def _dot(a, b):
    return jnp.dot(a, b, preferred_element_type=F32)


def _dot_nt(a, b):
    return lax.dot_general(a, b, (((1,), (1,)), ((), ())), preferred_element_type=F32)


def _dot_tn(a, b):
    return lax.dot_general(a, b, (((0,), (0,)), ((), ())), preferred_element_type=F32)


def _dot_exact(a, b):
    return jnp.dot(a, b, preferred_element_type=F32, precision=lax.Precision.HIGHEST)


def _rms(x, g):
    return x * lax.rsqrt(jnp.mean(x * x, axis=-1, keepdims=True) + EPS) * g


def _sigmoid(x):
    return 1.0 / (1.0 + jnp.exp(-x))


def _log_sigmoid(x):
    return jnp.minimum(x, 0.0) - jnp.log1p(jnp.exp(-jnp.abs(x)))


def _params(sem):
    return pltpu.CompilerParams(dimension_semantics=sem, vmem_limit_bytes=VMEM_LIMIT_BYTES)


def _const_spec(shape):
    nd = len(shape)
    return pl.BlockSpec(shape, lambda *_: (0,) * nd, pipeline_mode=pl.Buffered(1))


def _inproj_kernel(x_ref, g_ref, wb_ref, wf_ref, wg_ref, wgt_ref, bgr_ref, bgc_ref,
                   pb_ref, pf_ref, gcol_ref, grow_ref):
    xn = _rms(x_ref[...], g_ref[...]).astype(BF16)
    for c in range(0, wb_ref.shape[1], 512):
        p = _dot(xn, wb_ref[:, c:c + 512])
        if M_W <= c < 2 * M_W:
            p = p * (M_DH ** -0.5)
        pb_ref[:, c:c + 512] = p.astype(BF16)
    for c in range(0, wf_ref.shape[1], 512):
        pf_ref[:, c:c + 512] = _dot(xn, wf_ref[:, c:c + 512])
    gc = _dot(xn, wg_ref[...]) + bgr_ref[...]
    lane = lax.broadcasted_iota(jnp.int32, gc.shape, 1)
    gcol_ref[...] = jnp.where(lane >= M_HEADS, _log_sigmoid(gc), gc)
    gr = _dot_nt(wgt_ref[...], xn) + bgc_ref[...]
    row = lax.broadcasted_iota(jnp.int32, gr.shape, 0)
    grow_ref[...] = jnp.where(row >= M_HEADS, _log_sigmoid(gr), gr)[0:SUBLANES, :]


def _inproj(x, g, wb, wf, wg, wgt, bgr, bgc):
    n = x.shape[0]
    tm = TOK_TILE
    return pl.pallas_call(
        _inproj_kernel,
        grid=(n // tm,),
        in_specs=[
            pl.BlockSpec((tm, D_MODEL), lambda i: (i, 0)),
            _const_spec((1, D_MODEL)),
            _const_spec(wb.shape), _const_spec(wf.shape), _const_spec(wg.shape),
            _const_spec(wgt.shape), _const_spec(bgr.shape), _const_spec(bgc.shape),
        ],
        out_specs=[
            pl.BlockSpec((tm, wb.shape[1]), lambda i: (i, 0)),
            pl.BlockSpec((tm, wf.shape[1]), lambda i: (i, 0)),
            pl.BlockSpec((tm, LANES), lambda i: (i, 0)),
            pl.BlockSpec((SUBLANES, tm), lambda i: (0, i)),
        ],
        out_shape=[
            jax.ShapeDtypeStruct((n, wb.shape[1]), BF16),
            jax.ShapeDtypeStruct((n, wf.shape[1]), F32),
            jax.ShapeDtypeStruct((n, LANES), F32),
            jax.ShapeDtypeStruct((SUBLANES, n), F32),
        ],
        compiler_params=_params(("parallel",)),
        name="inproj",
    )(x, g, wb, wf, wg, wgt, bgr, bgc)


def _mlstm_head_out(hh, gnorm, o):
    hn = hh * lax.rsqrt(jnp.mean(hh * hh, axis=-1, keepdims=True) + EPS) * gnorm
    return (hn * _sigmoid(o)).astype(BF16)


def _mlstm_prompt_kernel(q_ref, k_ref, v_ref, o_ref, gcol_ref, grow_ref, gn_ref,
                         hm_ref, c_out_ref, n_out_ref, m_out_ref, c_s, n_s, m_s):
    c = pl.program_id(1)
    L = MLSTM_L

    @pl.when(c == 0)
    def _():
        c_s[...] = jnp.zeros_like(c_s)
        n_s[...] = jnp.zeros_like(n_s)
        m_s[...] = jnp.zeros_like(m_s)

    row = lax.broadcasted_iota(jnp.int32, (L, L), 0)
    col = lax.broadcasted_iota(jnp.int32, (L, L), 1)
    causal = col <= row
    tril = jnp.where(causal, 1.0, 0.0).astype(F32)
    triu = jnp.where(row <= col, 1.0, 0.0).astype(F32)
    gcol = gcol_ref[...]
    grow = grow_ref[...]
    bc_all = _dot_exact(tril, gcol)
    br_all = _dot_exact(grow, triu)

    heads = range(M_HEADS)
    sl = [slice(h * M_DH, (h + 1) * M_DH) for h in heads]
    q = [q_ref[:, sl[h]] for h in heads]
    k = [k_ref[:, sl[h]] for h in heads]
    v = [v_ref[:, sl[h]] for h in heads]
    li_r = [grow[h:h + 1, :] for h in heads]
    li_c = [gcol[:, h:h + 1] for h in heads]
    b_c = [bc_all[:, M_HEADS + h:M_HEADS + h + 1] for h in heads]
    b_r = [br_all[M_HEADS + h:M_HEADS + h + 1, :] for h in heads]
    m_prev = [m_s[h:h + 1, 0:1] for h in heads]
    cmat = [c_s[h] for h in heads]
    nrow = [n_s[h:h + 1, :] for h in heads]
    qk = [_dot_nt(q[h], k[h]) for h in heads]
    qc = [_dot_nt(q[h], cmat[h].astype(BF16)) for h in heads]
    log_d = [jnp.where(causal, b_c[h] - b_r[h] + li_r[h], -jnp.inf) for h in heads]
    log_inter = [b_c[h] + m_prev[h] for h in heads]
    m_t = [jnp.maximum(log_inter[h], jnp.max(log_d[h], axis=1, keepdims=True)) for h in heads]
    dm = [jnp.exp(log_d[h] - m_t[h]) for h in heads]
    inter = [jnp.exp(log_inter[h] - m_t[h]) for h in heads]
    s = [qk[h] * dm[h] for h in heads]
    num = [_dot(s[h].astype(BF16), v[h]) + inter[h] * qc[h] for h in heads]
    qn = [jnp.sum(q[h].astype(F32) * nrow[h], axis=1, keepdims=True) for h in heads]
    den = [jnp.sum(s[h], axis=1, keepdims=True) + inter[h] * qn[h] for h in heads]
    hh = [num[h] / jnp.maximum(jnp.abs(den[h]), jnp.exp(-m_t[h])) for h in heads]
    for h in heads:
        hm_ref[:, sl[h]] = _mlstm_head_out(hh[h], gn_ref[:, sl[h]], o_ref[:, sl[h]])

    m_new = [m_t[h][L - 1:L, :] for h in heads]
    b_last = [b_c[h][L - 1:L, :] for h in heads]
    w_r = [jnp.exp(b_last[h] - b_r[h] + li_r[h] - m_new[h]) for h in heads]
    w_c = [jnp.exp(b_last[h] - b_c[h] + li_c[h] - m_new[h]) for h in heads]
    decay = [jnp.exp(b_last[h] + m_prev[h] - m_new[h]) for h in heads]
    vw = [(v[h].astype(F32) * w_c[h]).astype(BF16) for h in heads]
    upd = [_dot_tn(vw[h], k[h]) for h in heads]
    wk = [_dot(jnp.broadcast_to(w_r[h], (SUBLANES, L)).astype(BF16), k[h]) for h in heads]
    for h in heads:
        c_s[h] = decay[h] * cmat[h] + upd[h]
        n_s[h:h + 1, :] = decay[h] * nrow[h] + wk[h][0:1, :]
        m_s[h:h + 1, :] = jnp.broadcast_to(m_new[h], (1, LANES))

    @pl.when(c == pl.num_programs(1) - 1)
    def _():
        c_out_ref[0] = c_s[...]
        n_out_ref[0] = n_s[0:M_HEADS, :]
        m_out_ref[0] = m_s[...]


def _mlstm_prompt(pb, pf, gcol, grow, gnorm, batch, seq):
    L = MLSTM_L
    nc = seq // L
    n = batch * seq
    rows = lambda b, c: b * nc + c
    return pl.pallas_call(
        _mlstm_prompt_kernel,
        grid=(batch, nc),
        in_specs=[
            pl.BlockSpec((L, M_W), lambda b, c: (rows(b, c), 0)),
            pl.BlockSpec((L, M_W), lambda b, c: (rows(b, c), 1)),
            pl.BlockSpec((L, M_W), lambda b, c: (rows(b, c), 2)),
            pl.BlockSpec((L, M_W), lambda b, c: (rows(b, c), 0)),
            pl.BlockSpec((L, LANES), lambda b, c: (rows(b, c), 0)),
            pl.BlockSpec((SUBLANES, L), lambda b, c: (0, rows(b, c))),
            _const_spec((1, M_W)),
        ],
        out_specs=[
            pl.BlockSpec((L, M_W), lambda b, c: (rows(b, c), 0)),
            pl.BlockSpec((1, M_HEADS, M_DH, M_DH), lambda b, c: (b, 0, 0, 0)),
            pl.BlockSpec((1, M_HEADS, M_DH), lambda b, c: (b, 0, 0)),
            pl.BlockSpec((1, SUBLANES, LANES), lambda b, c: (b, 0, 0)),
        ],
        out_shape=[
            jax.ShapeDtypeStruct((n, M_W), BF16),
            jax.ShapeDtypeStruct((batch, M_HEADS, M_DH, M_DH), F32),
            jax.ShapeDtypeStruct((batch, M_HEADS, M_DH), F32),
            jax.ShapeDtypeStruct((batch, SUBLANES, LANES), F32),
        ],
        scratch_shapes=[
            pltpu.VMEM((M_HEADS, M_DH, M_DH), F32),
            pltpu.VMEM((SUBLANES, M_DH), F32),
            pltpu.VMEM((SUBLANES, LANES), F32),
        ],
        compiler_params=_params(("parallel", "arbitrary")),
        name="mlstm_prompt",
    )(pb, pb, pb, pf, gcol, grow, gnorm)


def _seq_index(token, seq):
    assert seq & (seq - 1) == 0
    return token >> (seq.bit_length() - 1)


def _to_row(colvec, eye):
    return jnp.sum(jnp.where(eye, colvec, 0.0), axis=0, keepdims=True)


def _mlstm_sample_kernel(q_ref, k_ref, v_ref, o_ref, gcol_ref, mtok_ref, gn_ref, c_in_ref, n_in_ref,
                         hm_ref, c_out_ref, n_out_ref, mtok_out_ref, *, seq):
    R = GROUP_B * seq
    row = lax.broadcasted_iota(jnp.int32, (R, R), 0)
    col = lax.broadcasted_iota(jnp.int32, (R, R), 1)
    eye = row == col
    same = _seq_index(row, seq) == _seq_index(col, seq)
    causal = same & (col <= row)
    rblk = _seq_index(lax.broadcasted_iota(jnp.int32, (R, 1), 0), seq)
    lane = lax.broadcasted_iota(jnp.int32, (R, LANES), 1)
    gcol = gcol_ref[...]
    mtok = mtok_ref[...]
    m_out = jnp.zeros((R, LANES), F32)

    def per_block(vals):
        out = jnp.zeros((R, 1), F32)
        for j in range(GROUP_B):
            out = jnp.where(rblk == j, vals[j], out)
        return out

    heads = range(M_HEADS)
    group = range(GROUP_B)
    last = [j * seq + seq - 1 for j in group]
    sl = [slice(h * M_DH, (h + 1) * M_DH) for h in heads]
    q = [q_ref[:, sl[h]] for h in heads]
    k = [k_ref[:, sl[h]] for h in heads]
    v = [v_ref[:, sl[h]] for h in heads]
    qk = [_dot_nt(q[h], k[h]) for h in heads]
    qc_jh = [[_dot_nt(q[h], c_in_ref[j, h].astype(BF16)) for j in group] for h in heads]
    li_c = [gcol[:, h:h + 1] for h in heads]
    lf_c = [gcol[:, M_HEADS + h:M_HEADS + h + 1] for h in heads]
    li_r = [_to_row(li_c[h], eye) for h in heads]
    lf_r = [_to_row(lf_c[h], eye) for h in heads]
    m_prev = [mtok[:, h:h + 1] for h in heads]
    b_c = [jnp.sum(jnp.where(causal, lf_r[h], 0.0), axis=1, keepdims=True) for h in heads]
    b_r = [_to_row(b_c[h], eye) for h in heads]
    log_d = [jnp.where(causal, b_c[h] - b_r[h] + li_r[h], -jnp.inf) for h in heads]
    log_inter = [b_c[h] + m_prev[h] for h in heads]
    m_t = [jnp.maximum(log_inter[h], jnp.max(log_d[h], axis=1, keepdims=True)) for h in heads]
    dm = [jnp.exp(log_d[h] - m_t[h]) for h in heads]
    inter = [jnp.exp(log_inter[h] - m_t[h]) for h in heads]
    s = [qk[h] * dm[h] for h in heads]
    qc, ntok = [], []
    for h in heads:
        qc_h = jnp.zeros((R, M_DH), F32)
        ntok_h = jnp.zeros((R, M_DH), F32)
        for j in group:
            qc_h = jnp.where(rblk == j, qc_jh[h][j], qc_h)
            ntok_h = jnp.where(rblk == j, n_in_ref[j, h:h + 1, :], ntok_h)
        qc.append(qc_h)
        ntok.append(ntok_h)
    num = [_dot(s[h].astype(BF16), v[h]) + inter[h] * qc[h] for h in heads]
    qn = [jnp.sum(q[h].astype(F32) * ntok[h], axis=1, keepdims=True) for h in heads]
    den = [jnp.sum(s[h], axis=1, keepdims=True) + inter[h] * qn[h] for h in heads]
    hh = [num[h] / jnp.maximum(jnp.abs(den[h]), jnp.exp(-m_t[h])) for h in heads]
    for h in heads:
        hm_ref[:, sl[h]] = _mlstm_head_out(hh[h], gn_ref[:, sl[h]], o_ref[:, sl[h]])

    m_new = [per_block([m_t[h][r:r + 1, :] for r in last]) for h in heads]
    b_last = [per_block([b_c[h][r:r + 1, :] for r in last]) for h in heads]
    w_c = [jnp.exp(b_last[h] - b_c[h] + li_c[h] - m_new[h]) for h in heads]
    decay_c = [jnp.exp(b_last[h] + m_prev[h] - m_new[h]) for h in heads]
    vw = [v[h].astype(F32) * w_c[h] for h in heads]
    wk = [w_c[h] * k[h].astype(F32) for h in heads]
    upd = [[_dot_tn(jnp.where(rblk == j, vw[h], 0.0).astype(BF16), k[h]) for j in group] for h in heads]
    for h in heads:
        for j in group:
            decay = decay_c[h][last[j]:last[j] + 1, :]
            c_out_ref[j, h] = decay * c_in_ref[j, h] + upd[h][j]
            n_out_ref[j, h:h + 1, :] = (decay * n_in_ref[j, h:h + 1, :]
                                        + jnp.sum(jnp.where(rblk == j, wk[h], 0.0), axis=0, keepdims=True))
        m_out = jnp.where(lane == h, m_new[h], m_out)
    mtok_out_ref[...] = m_out


def _mlstm_sample(pb, pf, gcol, mtok, gnorm, state_c, state_n, seq):
    batch = state_c.shape[0]
    R = GROUP_B * seq
    n = batch * seq
    return pl.pallas_call(
        functools.partial(_mlstm_sample_kernel, seq=seq),
        grid=(batch // GROUP_B,),
        in_specs=[
            pl.BlockSpec((R, M_W), lambda i: (i, 0)),
            pl.BlockSpec((R, M_W), lambda i: (i, 1)),
            pl.BlockSpec((R, M_W), lambda i: (i, 2)),
            pl.BlockSpec((R, M_W), lambda i: (i, 0)),
            pl.BlockSpec((R, LANES), lambda i: (i, 0)),
            pl.BlockSpec((R, LANES), lambda i: (i, 0)),
            _const_spec((1, M_W)),
            pl.BlockSpec((GROUP_B, M_HEADS, M_DH, M_DH), lambda i: (i, 0, 0, 0)),
            pl.BlockSpec((GROUP_B, M_HEADS, M_DH), lambda i: (i, 0, 0)),
        ],
        out_specs=[
            pl.BlockSpec((R, M_W), lambda i: (i, 0)),
            pl.BlockSpec((GROUP_B, M_HEADS, M_DH, M_DH), lambda i: (i, 0, 0, 0)),
            pl.BlockSpec((GROUP_B, M_HEADS, M_DH), lambda i: (i, 0, 0)),
            pl.BlockSpec((R, LANES), lambda i: (i, 0)),
        ],
        out_shape=[
            jax.ShapeDtypeStruct((n, M_W), BF16),
            jax.ShapeDtypeStruct(state_c.shape, F32),
            jax.ShapeDtypeStruct(state_n.shape, F32),
            jax.ShapeDtypeStruct((n, LANES), F32),
        ],
        compiler_params=_params(("parallel",)),
        name="mlstm_sample",
    )(pb, pb, pb, pf, gcol, mtok, gnorm, state_c, state_n)


def _conv_out(y, lg, lb):
    mu = jnp.mean(y, axis=-1, keepdims=True)
    yc = y - mu
    var = jnp.mean(yc * yc, axis=-1, keepdims=True)
    z = yc * lax.rsqrt(var + EPS) * lg + lb
    return (z * _sigmoid(z)).astype(BF16)


def _conv_prompt_kernel(a_ref, g_ref, w_ref, b_ref, lg_ref, lb_ref, yc_ref, new_ref, xp_s, xsh_s):
    t = pl.program_id(1)
    T = CONV_T

    @pl.when(t == 0)
    def _():
        xp_s[0:CONV_OFF, :] = jnp.zeros((CONV_OFF, C_W), F32)

    xp_s[CONV_OFF:CONV_OFF + T, :] = a_ref[...] * _sigmoid(g_ref[...])
    base = CONV_OFF - CONV_PAST
    acc = jnp.zeros((T, C_W), F32) + b_ref[...]
    for b in range(SUBLANES):
        taps = [j for j in range(b, CONV_K, SUBLANES)]
        span = taps[-1] - b + T
        xsh_s[b, 0:span, :] = xp_s[base + b:base + b + span, :]
        for j in taps:
            acc = acc + w_ref[j:j + 1, :] * xsh_s[b, j - b:j - b + T, :]
    yc_ref[...] = _conv_out(acc, lg_ref[...], lb_ref[...])
    tail = xp_s[CONV_OFF + T - CONV_PAST:CONV_OFF + T, :]
    xp_s[base:CONV_OFF, :] = tail

    @pl.when(t == pl.num_programs(1) - 1)
    def _():
        new_ref[0] = tail


def _conv_prompt(pf, w, b, lg, lb, batch, seq):
    T = CONV_T
    nt = seq // T
    n = batch * seq
    return pl.pallas_call(
        _conv_prompt_kernel,
        grid=(batch, nt),
        in_specs=[
            pl.BlockSpec((T, C_W), lambda bi, t: (bi * nt + t, 2)),
            pl.BlockSpec((T, C_W), lambda bi, t: (bi * nt + t, 3)),
            _const_spec(w.shape), _const_spec((1, C_W)), _const_spec((1, C_W)), _const_spec((1, C_W)),
        ],
        out_specs=[
            pl.BlockSpec((T, C_W), lambda bi, t: (bi * nt + t, 0)),
            pl.BlockSpec((1, CONV_PAST, C_W), lambda bi, t: (bi, 0, 0)),
        ],
        out_shape=[
            jax.ShapeDtypeStruct((n, C_W), BF16),
            jax.ShapeDtypeStruct((batch, CONV_PAST, C_W), F32),
        ],
        scratch_shapes=[pltpu.VMEM((CONV_OFF + T, C_W), F32),
                        pltpu.VMEM((SUBLANES, CONV_OFF + T, C_W), F32)],
        compiler_params=_params(("parallel", "arbitrary")),
        name="conv_prompt",
    )(pf, pf, w, b, lg, lb)


def _conv_sample_kernel(a_ref, g_ref, past_ref, w_ref, b_ref, lg_ref, lb_ref, yc_ref, new_ref,
                        xp_s, y_s, *, seq, gb):
    u = a_ref[...] * _sigmoid(g_ref[...])
    for j in range(gb):
        xp_s[0:CONV_PAST, :] = past_ref[j]
        xp_s[CONV_PAST:CONV_PAST + seq, :] = u[j * seq:(j + 1) * seq, :]
        acc = jnp.zeros((seq, C_W), F32) + b_ref[...]
        for i in range(CONV_K):
            acc = acc + w_ref[i:i + 1, :] * xp_s[i:i + seq, :]
        y_s[j * seq:(j + 1) * seq, :] = acc
        new_ref[j] = xp_s[seq:seq + CONV_PAST, :]
    yc_ref[...] = _conv_out(y_s[...], lg_ref[...], lb_ref[...])


def _conv_sample(pf, past, w, b, lg, lb, seq):
    batch = past.shape[0]
    gb = 8
    R = gb * seq
    n = batch * seq
    return pl.pallas_call(
        functools.partial(_conv_sample_kernel, seq=seq, gb=gb),
        grid=(batch // gb,),
        in_specs=[
            pl.BlockSpec((R, C_W), lambda i: (i, 2)),
            pl.BlockSpec((R, C_W), lambda i: (i, 3)),
            pl.BlockSpec((gb, CONV_PAST, C_W), lambda i: (i, 0, 0)),
            _const_spec(w.shape), _const_spec((1, C_W)), _const_spec((1, C_W)), _const_spec((1, C_W)),
        ],
        out_specs=[
            pl.BlockSpec((R, C_W), lambda i: (i, 0)),
            pl.BlockSpec((gb, CONV_PAST, C_W), lambda i: (i, 0, 0)),
        ],
        out_shape=[
            jax.ShapeDtypeStruct((n, C_W), BF16),
            jax.ShapeDtypeStruct(past.shape, F32),
        ],
        scratch_shapes=[pltpu.VMEM((CONV_PAST + seq + 6, C_W), F32), pltpu.VMEM((R, C_W), F32)],
        compiler_params=_params(("parallel",)),
        name="conv_sample",
    )(pf, pf, past, w, b, lg, lb)


def _memkv_kernel(x_ref, g_ref, w_ref, k_ref, v_ref):
    xn = _rms(x_ref[...], g_ref[...]).astype(BF16)
    k_ref[...] = _dot(xn, w_ref[:, 0:X_W])
    v_ref[...] = _dot(xn, w_ref[:, X_W:2 * X_W])


def _memkv(mem, g, w):
    n = mem.shape[0]
    tm = TOK_TILE
    return pl.pallas_call(
        _memkv_kernel,
        grid=(n // tm,),
        in_specs=[pl.BlockSpec((tm, D_MODEL), lambda i: (i, 0)), _const_spec((1, D_MODEL)),
                  _const_spec(w.shape)],
        out_specs=[pl.BlockSpec((tm, X_W), lambda i: (i, 0)), pl.BlockSpec((tm, X_W), lambda i: (i, 0))],
        out_shape=[jax.ShapeDtypeStruct((n, X_W), F32), jax.ShapeDtypeStruct((n, X_W), F32)],
        compiler_params=_params(("parallel",)),
        name="memkv",
    )(mem, g, w)


def _xattn_prompt_kernel(q_ref, k_ref, v_ref, o_ref):
    heads = range(X_HEADS)
    sl = [slice(h * X_DH, (h + 1) * X_DH) for h in heads]
    s = [_dot_nt(q_ref[:, sl[h]], k_ref[:, sl[h]].astype(BF16)) * (X_DH ** -0.5) for h in heads]
    p = [jnp.exp(s[h] - jnp.max(s[h], axis=-1, keepdims=True)) for h in heads]
    p = [p[h] / jnp.sum(p[h], axis=-1, keepdims=True) for h in heads]
    for h in heads:
        o_ref[:, sl[h]] = _dot(p[h].astype(BF16), v_ref[:, sl[h]].astype(BF16)).astype(BF16)


def _xattn_prompt(pb, k, v, batch, seq):
    tq = TOK_TILE
    nt = seq // tq
    qblk = 3 * M_W // X_W
    return pl.pallas_call(
        _xattn_prompt_kernel,
        grid=(batch, nt),
        in_specs=[
            pl.BlockSpec((tq, X_W), lambda b, t: (b * nt + t, qblk)),
            pl.BlockSpec((N_MEM, X_W), lambda b, t: (b, 0)),
            pl.BlockSpec((N_MEM, X_W), lambda b, t: (b, 0)),
        ],
        out_specs=pl.BlockSpec((tq, X_W), lambda b, t: (b * nt + t, 0)),
        out_shape=jax.ShapeDtypeStruct((batch * seq, X_W), BF16),
        compiler_params=_params(("parallel", "parallel")),
        name="xattn_prompt",
    )(pb, k, v)


def _xattn_sample_kernel(q_ref, k_ref, v_ref, o_ref, *, seq):
    R = GROUP_B * seq
    assert R == 16 and X_HEADS == 4
    rblk = _seq_index(lax.broadcasted_iota(jnp.int32, (R, 1), 0), seq)
    nk = N_MEM * X_HEADS
    row_head = lax.broadcasted_iota(jnp.int32, (X_HEADS * R, nk), 0) >> 4
    col_head = lax.broadcasted_iota(jnp.int32, (X_HEADS * R, nk), 1) & (X_HEADS - 1)
    own = row_head == col_head
    q4 = jnp.concatenate([q_ref[:, h * X_DH:(h + 1) * X_DH] for h in range(X_HEADS)], axis=0)
    outs = [jnp.zeros((R, X_DH), F32) for _ in range(X_HEADS)]
    group = range(GROUP_B)
    s = [_dot_nt(q4, k_ref[j].astype(BF16)) * (X_DH ** -0.5) for j in group]
    s = [jnp.where(own, s[j], -jnp.inf) for j in group]
    p = [jnp.exp(s[j] - jnp.max(s[j], axis=-1, keepdims=True)) for j in group]
    p = [p[j] / jnp.sum(p[j], axis=-1, keepdims=True) for j in group]
    o4s = [_dot(p[j].astype(BF16), v_ref[j].astype(BF16)) for j in group]
    for j in group:
        o4 = o4s[j]
        outs = [jnp.where(rblk == j, o4[h * R:(h + 1) * R, :], outs[h]) for h in range(X_HEADS)]
    for h in range(X_HEADS):
        o_ref[:, h * X_DH:(h + 1) * X_DH] = outs[h].astype(BF16)


def _xattn_sample(pb, k, v, seq):
    batch = k.shape[0]
    R = GROUP_B * seq
    qblk = 3 * M_W // X_W
    return pl.pallas_call(
        functools.partial(_xattn_sample_kernel, seq=seq),
        grid=(batch // GROUP_B,),
        in_specs=[
            pl.BlockSpec((R, X_W), lambda i: (i, qblk)),
            pl.BlockSpec((GROUP_B, N_MEM * X_HEADS, X_DH), lambda i: (i, 0, 0)),
            pl.BlockSpec((GROUP_B, N_MEM * X_HEADS, X_DH), lambda i: (i, 0, 0)),
        ],
        out_specs=pl.BlockSpec((R, X_W), lambda i: (i, 0)),
        out_shape=jax.ShapeDtypeStruct((batch * seq, X_W), BF16),
        compiler_params=_params(("parallel",)),
        name="xattn_sample",
    )(pb, k, v)


def _outproj_kernel(x_ref, hm_ref, yc_ref, ox_ref, w_ref, g2_ref, wq_ref, x1_ref, xnt_ref, q_ref):
    acc = x_ref[...]
    acc = acc + _dot(hm_ref[...], w_ref[0:M_W, :])
    acc = acc + _dot(yc_ref[...], w_ref[M_W:M_W + C_W, :])
    acc = acc + _dot(ox_ref[...], w_ref[M_W + C_W:, :])
    x1_ref[...] = acc
    xn = _rms(acc, g2_ref[...])
    xnt_ref[...] = jnp.transpose(xn).astype(BF16)
    q_ref[...] = _dot(xn.astype(BF16), wq_ref[...]).astype(BF16)


def _outproj(x, hm, yc, ox, w, g2, wq):
    n = x.shape[0]
    tm = TOK_TILE
    return pl.pallas_call(
        _outproj_kernel,
        grid=(n // tm,),
        in_specs=[
            pl.BlockSpec((tm, D_MODEL), lambda i: (i, 0)),
            pl.BlockSpec((tm, M_W), lambda i: (i, 0)),
            pl.BlockSpec((tm, C_W), lambda i: (i, 0)),
            pl.BlockSpec((tm, X_W), lambda i: (i, 0)),
            _const_spec(w.shape), _const_spec((1, D_MODEL)), _const_spec(wq.shape),
        ],
        out_specs=[
            pl.BlockSpec((tm, D_MODEL), lambda i: (i, 0)),
            pl.BlockSpec((D_MODEL, tm), lambda i: (0, i)),
            pl.BlockSpec((tm, wq.shape[1]), lambda i: (i, 0)),
        ],
        out_shape=[
            jax.ShapeDtypeStruct((n, D_MODEL), F32),
            jax.ShapeDtypeStruct((D_MODEL, n), BF16),
            jax.ShapeDtypeStruct((n, wq.shape[1]), BF16),
        ],
        compiler_params=_params(("parallel",)),
        name="outproj",
    )(x, hm, yc, ox, w, g2, wq)


def _sort16_pairs():
    n, pairs, p = P_TOPK, [], 1
    while p < n:
        k = p
        while k >= 1:
            for j in range(k % p, n - k, 2 * k):
                for i in range(min(k, n - j - k)):
                    if (i + j) // (2 * p) == (i + j + k) // (2 * p):
                        pairs.append((i + j, i + j + k))
            k //= 2
        p *= 2
    return pairs


_SORT16 = _sort16_pairs()
_BITONIC16 = [(i, i + d) for d in (8, 4, 2, 1) for i in range(P_TOPK) if not i & d]
_CAND_ROW_LEN = [P_TOPK // (p + 1) for p in range(P_TOPK)]


def _exchange(xs, pairs):
    for a, b in pairs:
        hi = jnp.maximum(xs[a], xs[b])
        lo = jnp.minimum(xs[a], xs[b])
        xs[a], xs[b] = hi, lo
    return xs


def _merge_top16(xs, ys):
    xs = list(xs)
    for r, y in enumerate(ys):
        xs[P_TOPK - 1 - r] = jnp.maximum(xs[P_TOPK - 1 - r], y)
    return _exchange(xs, _BITONIC16)


def _top16_sorted(st):
    t = st.shape[1]
    x3 = st.reshape(P_NKEYS // SUBLANES, SUBLANES, t)
    xs = _exchange([x3[g] for g in range(P_TOPK)], _SORT16)
    for shift in (4, 2, 1):
        xs = _merge_top16(xs, [pltpu.roll(x, shift, 0) for x in xs])
    return xs


def _theta_and_z(a, b):
    top = [a[0] + b[q] for q in range(P_TOPK)]
    for p in range(1, P_TOPK):
        top = _merge_top16(top, [a[p] + b[q] for q in range(_CAND_ROW_LEN[p])])
    z = jnp.ones_like(top[0])
    for r in range(1, P_TOPK):
        z = z + jnp.exp(top[r] - top[0])
    return top[P_TOPK - 1], z


def _peer_kernel(x_ref, xnt_ref, q_ref, gf_ref, sk_ref, u_ref, vt_ref, y_ref,
                 s1_s, s2_s, f2_s, th_s, a0_s, rz_s, acc_s, wg_s):
    e = pl.program_id(1)
    tm = x_ref.shape[0]
    nsb = u_ref.shape[0] // P_NKEYS

    def prologue():
        for h in range(P_HEADS):
            for c, s_s in enumerate((s1_s, s2_s)):
                r0 = (2 * h + c) * P_NKEYS
                st = _dot_nt(sk_ref[2 * h + c], q_ref[:, r0:r0 + P_NKEYS])
                for tc in range(tm // LANES):
                    s_s[h, tc] = st[:, tc * LANES:(tc + 1) * LANES]
        sub = lax.broadcasted_iota(jnp.int32, (SUBLANES, LANES), 0)
        for tc in range(tm // LANES):
            ls = slice(tc * LANES, (tc + 1) * LANES)
            tops = []
            for s_s in (s1_s, s2_s):
                top = [jnp.zeros((SUBLANES, LANES), F32) for _ in range(P_TOPK)]
                for h in range(P_HEADS):
                    srt = _top16_sorted(s_s[h, tc])
                    top = [jnp.where(sub == h, srt[r], top[r]) for r in range(P_TOPK)]
                tops.append(top)
            theta, z = _theta_and_z(tops[0], tops[1])
            th_s[:, ls] = theta
            a0_s[:, ls] = tops[0][0]
            rz_s[:, ls] = 0.5 / z
            b0 = tops[1][0]
            for h in range(P_HEADS):
                f2_s[h, tc] = jnp.exp(s2_s[h, tc] - b0[h:h + 1, :])
        acc_s[...] = jnp.zeros_like(acc_s)

    def gate_times_gelu(at, sb):
        i = e * nsb + sb
        rows = slice(sb * P_NKEYS, (sb + 1) * P_NKEYS)
        for tc in range(tm // LANES):
            ls = slice(tc * LANES, (tc + 1) * LANES)
            w = jnp.zeros((P_NKEYS, LANES), F32)
            for h in range(P_HEADS):
                s1row = s1_s[h, tc, pl.ds(i, 1), :]
                f1row = jnp.exp(s1row - a0_s[h:h + 1, ls]) * rz_s[h:h + 1, ls]
                hit = (s2_s[h, tc] + s1row) >= th_s[h:h + 1, ls]
                w = w + jnp.where(hit, f2_s[h, tc], 0.0) * f1row
            a = at[rows, ls]
            inner = a * (0.7978845608028654 + 0.035677408136300125 * (a * a))
            wg_s[rows, ls] = (w * (a + a * jnp.tanh(inner))).astype(BF16)

    @pl.when(e == 0)
    def _():
        prologue()

    at = _dot(u_ref[...], xnt_ref[...])
    for sb in range(nsb):
        gate_times_gelu(at, sb)
    acc_s[...] += _dot(vt_ref[...], wg_s[...])

    @pl.when(e == pl.num_programs(1) - 1)
    def _():
        y = x_ref[...] + jnp.transpose(acc_s[...])
        y_ref[...] = _rms(y, gf_ref[...])


def _peer(x1, xnt, q, gf, sk, u, vt, tm):
    n = x1.shape[0]
    te = PEER_TE
    chunked = (P_HEADS, tm // LANES, P_NKEYS, LANES)
    once = pl.Buffered(1)
    return pl.pallas_call(
        _peer_kernel,
        grid=(n // tm, P_NEXP // te),
        in_specs=[
            pl.BlockSpec((tm, D_MODEL), lambda i, e: (i, 0), pipeline_mode=once),
            pl.BlockSpec((D_MODEL, tm), lambda i, e: (0, i), pipeline_mode=once),
            pl.BlockSpec((tm, q.shape[1]), lambda i, e: (i, 0), pipeline_mode=once),
            _const_spec((1, D_MODEL)), _const_spec(sk.shape),
            pl.BlockSpec((te, D_MODEL), lambda i, e: (e, 0)),
            pl.BlockSpec((D_MODEL, te), lambda i, e: (0, e)),
        ],
        out_specs=pl.BlockSpec((tm, D_MODEL), lambda i, e: (i, 0)),
        out_shape=jax.ShapeDtypeStruct((n, D_MODEL), F32),
        scratch_shapes=[
            pltpu.VMEM(chunked, F32),
            pltpu.VMEM(chunked, F32),
            pltpu.VMEM(chunked, F32),
            pltpu.VMEM((SUBLANES, tm), F32),
            pltpu.VMEM((SUBLANES, tm), F32),
            pltpu.VMEM((SUBLANES, tm), F32),
            pltpu.VMEM((D_MODEL, tm), F32),
            pltpu.VMEM((te, tm), BF16),
        ],
        compiler_params=_params(("parallel", "arbitrary")),
        name="peer",
    )(x1, xnt, q, gf, sk, u, vt)


def kernel(x_prompt, x_sample, mem_prompt, cache_mem_k, cache_mem_v, state_C, state_n, state_m, state_conv,
           norm1_g, w_in, b_gate, mlstm_norm_g, conv_w, conv_b, conv_ln_g, conv_ln_b, mem_norm_g, w_mk, w_mv,
           w_out, norm2_g, peer_wq, peer_subkeys, peer_u, peer_v, final_g):
    depth = w_in.shape[0]
    assert depth == 1
    bp, seq_p, _ = x_prompt.shape
    bs, seq_s, _ = x_sample.shape
    l = 0

    w = w_in[l]
    o0, g0, a0, gg0, qx0 = 3 * M_W, 4 * M_W, 4 * M_W + 2 * M_HEADS, 4 * M_W + 2 * M_HEADS + C_W, \
        4 * M_W + 2 * M_HEADS + 2 * C_W
    wb = jnp.concatenate([w[:, 0:3 * M_W], w[:, qx0:qx0 + X_W]], axis=1).astype(BF16)
    wf = jnp.concatenate([w[:, o0:o0 + M_W], w[:, a0:gg0 + C_W]], axis=1).astype(BF16)
    wgate = w[:, g0:g0 + 2 * M_HEADS]
    wg = jnp.pad(wgate.astype(BF16), ((0, 0), (0, LANES - 2 * M_HEADS)))
    wgt = jnp.pad(wgate.T.astype(BF16), ((0, 2 * SUBLANES - 2 * M_HEADS), (0, 0)))
    bgr = jnp.pad(b_gate[l][None, :], ((0, 0), (0, LANES - 2 * M_HEADS)))
    bgc = jnp.pad(b_gate[l][:, None], ((0, 2 * SUBLANES - 2 * M_HEADS), (0, 0)))
    g1 = norm1_g[l][None, :]
    gnorm = mlstm_norm_g[l].reshape(1, M_W)
    cw = jnp.pad(conv_w[l], ((0, 1), (0, 0)))
    cb, clg, clb = conv_b[l][None, :], conv_ln_g[l][None, :], conv_ln_b[l][None, :]
    wkv = jnp.concatenate([w_mk[l], w_mv[l]], axis=1).astype(BF16)
    wo = w_out[l].astype(BF16)
    wq = peer_wq[l].astype(BF16)
    sk = peer_subkeys[l].reshape(2 * P_HEADS, P_NKEYS, P_NKEYS).astype(BF16)
    u = peer_u[l].astype(BF16)
    vt = peer_v[l].T.astype(BF16)
    g2 = norm2_g[l][None, :]
    gf = final_g[None, :]

    xp = x_prompt.reshape(bp * seq_p, D_MODEL)
    xs = x_sample.reshape(bs * seq_s, D_MODEL)

    pb, pf, gcol, grow = _inproj(xp, g1, wb, wf, wg, wgt, bgr, bgc)
    hm, c_p, n_p, m_p = _mlstm_prompt(pb, pf, gcol, grow, gnorm, bp, seq_p)
    yc, conv_p = _conv_prompt(pf, cw, cb, clg, clb, bp, seq_p)
    mk, mv = _memkv(mem_prompt.reshape(bp * N_MEM, D_MODEL), mem_norm_g[l][None, :], wkv)
    ox = _xattn_prompt(pb, mk, mv, bp, seq_p)
    x1, xnt, q = _outproj(xp, hm, yc, ox, wo, g2, wq)
    y_p = _peer(x1, xnt, q, gf, sk, u, vt, TOK_TILE)

    pb, pf, gcol, _ = _inproj(xs, g1, wb, wf, wg, wgt, bgr, bgc)
    mtok = jnp.pad(jnp.repeat(state_m[l], seq_s, axis=0), ((0, 0), (0, LANES - M_HEADS)))
    hm, c_s, n_s, mtok_new = _mlstm_sample(pb, pf, gcol, mtok, gnorm, state_C[l], state_n[l], seq_s)
    yc, conv_s = _conv_sample(pf, state_conv[l], cw, cb, clg, clb, seq_s)
    ox = _xattn_sample(pb, cache_mem_k.reshape(bs, N_MEM * X_HEADS, X_DH),
                       cache_mem_v.reshape(bs, N_MEM * X_HEADS, X_DH), seq_s)
    x1, xnt, q = _outproj(xs, hm, yc, ox, wo, g2, wq)
    y_s = _peer(x1, xnt, q, gf, sk, u, vt, TOK_TILE)

    m_s = mtok_new.reshape(bs, seq_s, LANES)[:, seq_s - 1, :M_HEADS]
    return (y_p.reshape(bp, seq_p, D_MODEL),
            y_s.reshape(bs, seq_s, D_MODEL),
            mk.reshape(1, bp, N_MEM, X_HEADS, X_DH),
            mv.reshape(1, bp, N_MEM, X_HEADS, X_DH),
            c_p[None], n_p[None], m_p[None, :, :M_HEADS, 0], conv_p[None],
            c_s[None], n_s[None], m_s[None], conv_s[None])
```

```python
import functools

import jax
import jax.numpy as jnp
from jax import lax
from jax.experimental import pallas as pl
from jax.experimental.pallas import tpu as pltpu

F32 = jnp.float32
BF16 = jnp.bfloat16

D_MODEL = 2048
M_HEADS = 4
M_DH = 256
M_W = M_HEADS * M_DH
C_W = 512
CONV_K = 31
CONV_PAST = CONV_K - 1
X_HEADS = 4
X_DH = 128
X_W = X_HEADS * X_DH
N_MEM = 256
P_HEADS = 8
P_NKEYS = 128
P_NEXP = P_NKEYS * P_NKEYS
P_TOPK = 16
EPS = 1e-6

SUBLANES = 8
LANES = 128
VMEM_LIMIT_BYTES = 58 * 1024 * 1024

MLSTM_L = 256
GROUP_B = 4
TOK_TILE = 512
PEER_TE = 1024
CONV_T = 256
CONV_OFF = 32


def _dot(a, b):
    return jnp.dot(a, b, preferred_element_type=F32)


def _dot_nt(a, b):
    return lax.dot_general(a, b, (((1,), (1,)), ((), ())), preferred_element_type=F32)


def _dot_tn(a, b):
    return lax.dot_general(a, b, (((0,), (0,)), ((), ())), preferred_element_type=F32)


def _dot_exact(a, b):
    return jnp.dot(a, b, preferred_element_type=F32, precision=lax.Precision.HIGHEST)


def _rms(x, g):
    return x * lax.rsqrt(jnp.mean(x * x, axis=-1, keepdims=True) + EPS) * g


def _sigmoid(x):
    return 1.0 / (1.0 + jnp.exp(-x))


def _log_sigmoid(x):
    return jnp.minimum(x, 0.0) - jnp.log1p(jnp.exp(-jnp.abs(x)))


def _params(sem):
    return pltpu.CompilerParams(dimension_semantics=sem, vmem_limit_bytes=VMEM_LIMIT_BYTES)


def _const_spec(shape):
    nd = len(shape)
    return pl.BlockSpec(shape, lambda *_: (0,) * nd, pipeline_mode=pl.Buffered(1))


def _inproj_kernel(x_ref, g_ref, wb_ref, wf_ref, wg_ref, wgt_ref, bgr_ref, bgc_ref,
                   pb_ref, pf_ref, gcol_ref, grow_ref):
    xn = _rms(x_ref[...], g_ref[...]).astype(BF16)
    for c in range(0, wb_ref.shape[1], 512):
        p = _dot(xn, wb_ref[:, c:c + 512])
        if M_W <= c < 2 * M_W:
            p = p * (M_DH ** -0.5)
        pb_ref[:, c:c + 512] = p.astype(BF16)
    for c in range(0, wf_ref.shape[1], 512):
        pf_ref[:, c:c + 512] = _dot(xn, wf_ref[:, c:c + 512])
    gc = _dot(xn, wg_ref[...]) + bgr_ref[...]
    lane = lax.broadcasted_iota(jnp.int32, gc.shape, 1)
    gcol_ref[...] = jnp.where(lane >= M_HEADS, _log_sigmoid(gc), gc)
    gr = _dot_nt(wgt_ref[...], xn) + bgc_ref[...]
    row = lax.broadcasted_iota(jnp.int32, gr.shape, 0)
    grow_ref[...] = jnp.where(row >= M_HEADS, _log_sigmoid(gr), gr)[0:SUBLANES, :]


def _inproj(x, g, wb, wf, wg, wgt, bgr, bgc):
    n = x.shape[0]
    tm = TOK_TILE
    return pl.pallas_call(
        _inproj_kernel,
        grid=(n // tm,),
        in_specs=[
            pl.BlockSpec((tm, D_MODEL), lambda i: (i, 0)),
            _const_spec((1, D_MODEL)),
            _const_spec(wb.shape), _const_spec(wf.shape), _const_spec(wg.shape),
            _const_spec(wgt.shape), _const_spec(bgr.shape), _const_spec(bgc.shape),
        ],
        out_specs=[
            pl.BlockSpec((tm, wb.shape[1]), lambda i: (i, 0)),
            pl.BlockSpec((tm, wf.shape[1]), lambda i: (i, 0)),
            pl.BlockSpec((tm, LANES), lambda i: (i, 0)),
            pl.BlockSpec((SUBLANES, tm), lambda i: (0, i)),
        ],
        out_shape=[
            jax.ShapeDtypeStruct((n, wb.shape[1]), BF16),
            jax.ShapeDtypeStruct((n, wf.shape[1]), F32),
            jax.ShapeDtypeStruct((n, LANES), F32),
            jax.ShapeDtypeStruct((SUBLANES, n), F32),
        ],
        compiler_params=_params(("parallel",)),
        name="inproj",
    )(x, g, wb, wf, wg, wgt, bgr, bgc)


def _mlstm_head_out(hh, gnorm, o):
    hn = hh * lax.rsqrt(jnp.mean(hh * hh, axis=-1, keepdims=True) + EPS) * gnorm
    return (hn * _sigmoid(o)).astype(BF16)


def _mlstm_prompt_kernel(q_ref, k_ref, v_ref, o_ref, gcol_ref, grow_ref, gn_ref,
                         hm_ref, c_out_ref, n_out_ref, m_out_ref, c_s, n_s, m_s):
    c = pl.program_id(1)
    L = MLSTM_L

    @pl.when(c == 0)
    def _():
        c_s[...] = jnp.zeros_like(c_s)
        n_s[...] = jnp.zeros_like(n_s)
        m_s[...] = jnp.zeros_like(m_s)

    row = lax.broadcasted_iota(jnp.int32, (L, L), 0)
    col = lax.broadcasted_iota(jnp.int32, (L, L), 1)
    causal = col <= row
    tril = jnp.where(causal, 1.0, 0.0).astype(F32)
    triu = jnp.where(row <= col, 1.0, 0.0).astype(F32)
    gcol = gcol_ref[...]
    grow = grow_ref[...]
    bc_all = _dot_exact(tril, gcol)
    br_all = _dot_exact(grow, triu)

    heads = range(M_HEADS)
    sl = [slice(h * M_DH, (h + 1) * M_DH) for h in heads]
    q = [q_ref[:, sl[h]] for h in heads]
    k = [k_ref[:, sl[h]] for h in heads]
    v = [v_ref[:, sl[h]] for h in heads]
    li_r = [grow[h:h + 1, :] for h in heads]
    li_c = [gcol[:, h:h + 1] for h in heads]
    b_c = [bc_all[:, M_HEADS + h:M_HEADS + h + 1] for h in heads]
    b_r = [br_all[M_HEADS + h:M_HEADS + h + 1, :] for h in heads]
    m_prev = [m_s[h:h + 1, 0:1] for h in heads]
    cmat = [c_s[h] for h in heads]
    nrow = [n_s[h:h + 1, :] for h in heads]
    qk = [_dot_nt(q[h], k[h]) for h in heads]
    qc = [_dot_nt(q[h], cmat[h].astype(BF16)) for h in heads]
    log_d = [jnp.where(causal, b_c[h] - b_r[h] + li_r[h], -jnp.inf) for h in heads]
    log_inter = [b_c[h] + m_prev[h] for h in heads]
    m_t = [jnp.maximum(log_inter[h], jnp.max(log_d[h], axis=1, keepdims=True)) for h in heads]
    dm = [jnp.exp(log_d[h] - m_t[h]) for h in heads]
    inter = [jnp.exp(log_inter[h] - m_t[h]) for h in heads]
    s = [qk[h] * dm[h] for h in heads]
    num = [_dot(s[h].astype(BF16), v[h]) + inter[h] * qc[h] for h in heads]
    qn = [jnp.sum(q[h].astype(F32) * nrow[h], axis=1, keepdims=True) for h in heads]
    den = [jnp.sum(s[h], axis=1, keepdims=True) + inter[h] * qn[h] for h in heads]
    hh = [num[h] / jnp.maximum(jnp.abs(den[h]), jnp.exp(-m_t[h])) for h in heads]
    for h in heads:
        hm_ref[:, sl[h]] = _mlstm_head_out(hh[h], gn_ref[:, sl[h]], o_ref[:, sl[h]])

    m_new = [m_t[h][L - 1:L, :] for h in heads]
    b_last = [b_c[h][L - 1:L, :] for h in heads]
    w_r = [jnp.exp(b_last[h] - b_r[h] + li_r[h] - m_new[h]) for h in heads]
    w_c = [jnp.exp(b_last[h] - b_c[h] + li_c[h] - m_new[h]) for h in heads]
    decay = [jnp.exp(b_last[h] + m_prev[h] - m_new[h]) for h in heads]
    vw = [(v[h].astype(F32) * w_c[h]).astype(BF16) for h in heads]
    upd = [_dot_tn(vw[h], k[h]) for h in heads]
    wk = [_dot(jnp.broadcast_to(w_r[h], (SUBLANES, L)).astype(BF16), k[h]) for h in heads]
    for h in heads:
        c_s[h] = decay[h] * cmat[h] + upd[h]
        n_s[h:h + 1, :] = decay[h] * nrow[h] + wk[h][0:1, :]
        m_s[h:h + 1, :] = jnp.broadcast_to(m_new[h], (1, LANES))

    @pl.when(c == pl.num_programs(1) - 1)
    def _():
        c_out_ref[0] = c_s[...]
        n_out_ref[0] = n_s[0:M_HEADS, :]
        m_out_ref[0] = m_s[...]


def _mlstm_prompt(pb, pf, gcol, grow, gnorm, batch, seq):
    L = MLSTM_L
    nc = seq // L
    n = batch * seq
    rows = lambda b, c: b * nc + c
    return pl.pallas_call(
        _mlstm_prompt_kernel,
        grid=(batch, nc),
        in_specs=[
            pl.BlockSpec((L, M_W), lambda b, c: (rows(b, c), 0)),
            pl.BlockSpec((L, M_W), lambda b, c: (rows(b, c), 1)),
            pl.BlockSpec((L, M_W), lambda b, c: (rows(b, c), 2)),
            pl.BlockSpec((L, M_W), lambda b, c: (rows(b, c), 0)),
            pl.BlockSpec((L, LANES), lambda b, c: (rows(b, c), 0)),
            pl.BlockSpec((SUBLANES, L), lambda b, c: (0, rows(b, c))),
            _const_spec((1, M_W)),
        ],
        out_specs=[
            pl.BlockSpec((L, M_W), lambda b, c: (rows(b, c), 0)),
            pl.BlockSpec((1, M_HEADS, M_DH, M_DH), lambda b, c: (b, 0, 0, 0)),
            pl.BlockSpec((1, M_HEADS, M_DH), lambda b, c: (b, 0, 0)),
            pl.BlockSpec((1, SUBLANES, LANES), lambda b, c: (b, 0, 0)),
        ],
        out_shape=[
            jax.ShapeDtypeStruct((n, M_W), BF16),
            jax.ShapeDtypeStruct((batch, M_HEADS, M_DH, M_DH), F32),
            jax.ShapeDtypeStruct((batch, M_HEADS, M_DH), F32),
            jax.ShapeDtypeStruct((batch, SUBLANES, LANES), F32),
        ],
        scratch_shapes=[
            pltpu.VMEM((M_HEADS, M_DH, M_DH), F32),
            pltpu.VMEM((SUBLANES, M_DH), F32),
            pltpu.VMEM((SUBLANES, LANES), F32),
        ],
        compiler_params=_params(("parallel", "arbitrary")),
        name="mlstm_prompt",
    )(pb, pb, pb, pf, gcol, grow, gnorm)


def _seq_index(token, seq):
    assert seq & (seq - 1) == 0
    return token >> (seq.bit_length() - 1)


def _to_row(colvec, eye):
    return jnp.sum(jnp.where(eye, colvec, 0.0), axis=0, keepdims=True)


def _mlstm_sample_kernel(q_ref, k_ref, v_ref, o_ref, gcol_ref, mtok_ref, gn_ref, c_in_ref, n_in_ref,
                         hm_ref, c_out_ref, n_out_ref, mtok_out_ref, *, seq):
    R = GROUP_B * seq
    row = lax.broadcasted_iota(jnp.int32, (R, R), 0)
    col = lax.broadcasted_iota(jnp.int32, (R, R), 1)
    eye = row == col
    same = _seq_index(row, seq) == _seq_index(col, seq)
    causal = same & (col <= row)
    rblk = _seq_index(lax.broadcasted_iota(jnp.int32, (R, 1), 0), seq)
    lane = lax.broadcasted_iota(jnp.int32, (R, LANES), 1)
    gcol = gcol_ref[...]
    mtok = mtok_ref[...]
    m_out = jnp.zeros((R, LANES), F32)

    def per_block(vals):
        out = jnp.zeros((R, 1), F32)
        for j in range(GROUP_B):
            out = jnp.where(rblk == j, vals[j], out)
        return out

    heads = range(M_HEADS)
    group = range(GROUP_B)
    last = [j * seq + seq - 1 for j in group]
    sl = [slice(h * M_DH, (h + 1) * M_DH) for h in heads]
    q = [q_ref[:, sl[h]] for h in heads]
    k = [k_ref[:, sl[h]] for h in heads]
    v = [v_ref[:, sl[h]] for h in heads]
    qk = [_dot_nt(q[h], k[h]) for h in heads]
    qc_jh = [[_dot_nt(q[h], c_in_ref[j, h].astype(BF16)) for j in group] for h in heads]
    li_c = [gcol[:, h:h + 1] for h in heads]
    lf_c = [gcol[:, M_HEADS + h:M_HEADS + h + 1] for h in heads]
    li_r = [_to_row(li_c[h], eye) for h in heads]
    lf_r = [_to_row(lf_c[h], eye) for h in heads]
    m_prev = [mtok[:, h:h + 1] for h in heads]
    b_c = [jnp.sum(jnp.where(causal, lf_r[h], 0.0), axis=1, keepdims=True) for h in heads]
    b_r = [_to_row(b_c[h], eye) for h in heads]
    log_d = [jnp.where(causal, b_c[h] - b_r[h] + li_r[h], -jnp.inf) for h in heads]
    log_inter = [b_c[h] + m_prev[h] for h in heads]
    m_t = [jnp.maximum(log_inter[h], jnp.max(log_d[h], axis=1, keepdims=True)) for h in heads]
    dm = [jnp.exp(log_d[h] - m_t[h]) for h in heads]
    inter = [jnp.exp(log_inter[h] - m_t[h]) for h in heads]
    s = [qk[h] * dm[h] for h in heads]
    qc, ntok = [], []
    for h in heads:
        qc_h = jnp.zeros((R, M_DH), F32)
        ntok_h = jnp.zeros((R, M_DH), F32)
        for j in group:
            qc_h = jnp.where(rblk == j, qc_jh[h][j], qc_h)
            ntok_h = jnp.where(rblk == j, n_in_ref[j, h:h + 1, :], ntok_h)
        qc.append(qc_h)
        ntok.append(ntok_h)
    num = [_dot(s[h].astype(BF16), v[h]) + inter[h] * qc[h] for h in heads]
    qn = [jnp.sum(q[h].astype(F32) * ntok[h], axis=1, keepdims=True) for h in heads]
    den = [jnp.sum(s[h], axis=1, keepdims=True) + inter[h] * qn[h] for h in heads]
    hh = [num[h] / jnp.maximum(jnp.abs(den[h]), jnp.exp(-m_t[h])) for h in heads]
    for h in heads:
        hm_ref[:, sl[h]] = _mlstm_head_out(hh[h], gn_ref[:, sl[h]], o_ref[:, sl[h]])

    m_new = [per_block([m_t[h][r:r + 1, :] for r in last]) for h in heads]
    b_last = [per_block([b_c[h][r:r + 1, :] for r in last]) for h in heads]
    w_c = [jnp.exp(b_last[h] - b_c[h] + li_c[h] - m_new[h]) for h in heads]
    decay_c = [jnp.exp(b_last[h] + m_prev[h] - m_new[h]) for h in heads]
    vw = [v[h].astype(F32) * w_c[h] for h in heads]
    wk = [w_c[h] * k[h].astype(F32) for h in heads]
    upd = [[_dot_tn(jnp.where(rblk == j, vw[h], 0.0).astype(BF16), k[h]) for j in group] for h in heads]
    for h in heads:
        for j in group:
            decay = decay_c[h][last[j]:last[j] + 1, :]
            c_out_ref[j, h] = decay * c_in_ref[j, h] + upd[h][j]
            n_out_ref[j, h:h + 1, :] = (decay * n_in_ref[j, h:h + 1, :]
                                        + jnp.sum(jnp.where(rblk == j, wk[h], 0.0), axis=0, keepdims=True))
        m_out = jnp.where(lane == h, m_new[h], m_out)
    mtok_out_ref[...] = m_out


def _mlstm_sample(pb, pf, gcol, mtok, gnorm, state_c, state_n, seq):
    batch = state_c.shape[0]
    R = GROUP_B * seq
    n = batch * seq
    return pl.pallas_call(
        functools.partial(_mlstm_sample_kernel, seq=seq),
        grid=(batch // GROUP_B,),
        in_specs=[
            pl.BlockSpec((R, M_W), lambda i: (i, 0)),
            pl.BlockSpec((R, M_W), lambda i: (i, 1)),
            pl.BlockSpec((R, M_W), lambda i: (i, 2)),
            pl.BlockSpec((R, M_W), lambda i: (i, 0)),
            pl.BlockSpec((R, LANES), lambda i: (i, 0)),
            pl.BlockSpec((R, LANES), lambda i: (i, 0)),
            _const_spec((1, M_W)),
            pl.BlockSpec((GROUP_B, M_HEADS, M_DH, M_DH), lambda i: (i, 0, 0, 0)),
            pl.BlockSpec((GROUP_B, M_HEADS, M_DH), lambda i: (i, 0, 0)),
        ],
        out_specs=[
            pl.BlockSpec((R, M_W), lambda i: (i, 0)),
            pl.BlockSpec((GROUP_B, M_HEADS, M_DH, M_DH), lambda i: (i, 0, 0, 0)),
            pl.BlockSpec((GROUP_B, M_HEADS, M_DH), lambda i: (i, 0, 0)),
            pl.BlockSpec((R, LANES), lambda i: (i, 0)),
        ],
        out_shape=[
            jax.ShapeDtypeStruct((n, M_W), BF16),
            jax.ShapeDtypeStruct(state_c.shape, F32),
            jax.ShapeDtypeStruct(state_n.shape, F32),
            jax.ShapeDtypeStruct((n, LANES), F32),
        ],
        compiler_params=_params(("parallel",)),
        name="mlstm_sample",
    )(pb, pb, pb, pf, gcol, mtok, gnorm, state_c, state_n)


def _conv_out(y, lg, lb):
    mu = jnp.mean(y, axis=-1, keepdims=True)
    yc = y - mu
    var = jnp.mean(yc * yc, axis=-1, keepdims=True)
    z = yc * lax.rsqrt(var + EPS) * lg + lb
    return (z * _sigmoid(z)).astype(BF16)


def _conv_prompt_kernel(a_ref, g_ref, w_ref, b_ref, lg_ref, lb_ref, yc_ref, new_ref, xp_s, xsh_s):
    t = pl.program_id(1)
    T = CONV_T

    @pl.when(t == 0)
    def _():
        xp_s[0:CONV_OFF, :] = jnp.zeros((CONV_OFF, C_W), F32)

    xp_s[CONV_OFF:CONV_OFF + T, :] = a_ref[...] * _sigmoid(g_ref[...])
    base = CONV_OFF - CONV_PAST
    acc = jnp.zeros((T, C_W), F32) + b_ref[...]
    for b in range(SUBLANES):
        taps = [j for j in range(b, CONV_K, SUBLANES)]
        span = taps[-1] - b + T
        xsh_s[b, 0:span, :] = xp_s[base + b:base + b + span, :]
        for j in taps:
            acc = acc + w_ref[j:j + 1, :] * xsh_s[b, j - b:j - b + T, :]
    yc_ref[...] = _conv_out(acc, lg_ref[...], lb_ref[...])
    tail = xp_s[CONV_OFF + T - CONV_PAST:CONV_OFF + T, :]
    xp_s[base:CONV_OFF, :] = tail

    @pl.when(t == pl.num_programs(1) - 1)
    def _():
        new_ref[0] = tail


def _conv_prompt(pf, w, b, lg, lb, batch, seq):
    T = CONV_T
    nt = seq // T
    n = batch * seq
    return pl.pallas_call(
        _conv_prompt_kernel,
        grid=(batch, nt),
        in_specs=[
            pl.BlockSpec((T, C_W), lambda bi, t: (bi * nt + t, 2)),
            pl.BlockSpec((T, C_W), lambda bi, t: (bi * nt + t, 3)),
            _const_spec(w.shape), _const_spec((1, C_W)), _const_spec((1, C_W)), _const_spec((1, C_W)),
        ],
        out_specs=[
            pl.BlockSpec((T, C_W), lambda bi, t: (bi * nt + t, 0)),
            pl.BlockSpec((1, CONV_PAST, C_W), lambda bi, t: (bi, 0, 0)),
        ],
        out_shape=[
            jax.ShapeDtypeStruct((n, C_W), BF16),
            jax.ShapeDtypeStruct((batch, CONV_PAST, C_W), F32),
        ],
        scratch_shapes=[pltpu.VMEM((CONV_OFF + T, C_W), F32),
                        pltpu.VMEM((SUBLANES, CONV_OFF + T, C_W), F32)],
        compiler_params=_params(("parallel", "arbitrary")),
        name="conv_prompt",
    )(pf, pf, w, b, lg, lb)


def _conv_sample_kernel(a_ref, g_ref, past_ref, w_ref, b_ref, lg_ref, lb_ref, yc_ref, new_ref,
                        xp_s, y_s, *, seq, gb):
    u = a_ref[...] * _sigmoid(g_ref[...])
    for j in range(gb):
        xp_s[0:CONV_PAST, :] = past_ref[j]
        xp_s[CONV_PAST:CONV_PAST + seq, :] = u[j * seq:(j + 1) * seq, :]
        acc = jnp.zeros((seq, C_W), F32) + b_ref[...]
        for i in range(CONV_K):
            acc = acc + w_ref[i:i + 1, :] * xp_s[i:i + seq, :]
        y_s[j * seq:(j + 1) * seq, :] = acc
        new_ref[j] = xp_s[seq:seq + CONV_PAST, :]
    yc_ref[...] = _conv_out(y_s[...], lg_ref[...], lb_ref[...])


def _conv_sample(pf, past, w, b, lg, lb, seq):
    batch = past.shape[0]
    gb = 8
    R = gb * seq
    n = batch * seq
    return pl.pallas_call(
        functools.partial(_conv_sample_kernel, seq=seq, gb=gb),
        grid=(batch // gb,),
        in_specs=[
            pl.BlockSpec((R, C_W), lambda i: (i, 2)),
            pl.BlockSpec((R, C_W), lambda i: (i, 3)),
            pl.BlockSpec((gb, CONV_PAST, C_W), lambda i: (i, 0, 0)),
            _const_spec(w.shape), _const_spec((1, C_W)), _const_spec((1, C_W)), _const_spec((1, C_W)),
        ],
        out_specs=[
            pl.BlockSpec((R, C_W), lambda i: (i, 0)),
            pl.BlockSpec((gb, CONV_PAST, C_W), lambda i: (i, 0, 0)),
        ],
        out_shape=[
            jax.ShapeDtypeStruct((n, C_W), BF16),
            jax.ShapeDtypeStruct(past.shape, F32),
        ],
        scratch_shapes=[pltpu.VMEM((CONV_PAST + seq + 6, C_W), F32), pltpu.VMEM((R, C_W), F32)],
        compiler_params=_params(("parallel",)),
        name="conv_sample",
    )(pf, pf, past, w, b, lg, lb)


def _memkv_kernel(x_ref, g_ref, w_ref, k_ref, v_ref):
    xn = _rms(x_ref[...], g_ref[...]).astype(BF16)
    k_ref[...] = _dot(xn, w_ref[:, 0:X_W])
    v_ref[...] = _dot(xn, w_ref[:, X_W:2 * X_W])


def _memkv(mem, g, w):
    n = mem.shape[0]
    tm = TOK_TILE
    return pl.pallas_call(
        _memkv_kernel,
        grid=(n // tm,),
        in_specs=[pl.BlockSpec((tm, D_MODEL), lambda i: (i, 0)), _const_spec((1, D_MODEL)),
                  _const_spec(w.shape)],
        out_specs=[pl.BlockSpec((tm, X_W), lambda i: (i, 0)), pl.BlockSpec((tm, X_W), lambda i: (i, 0))],
        out_shape=[jax.ShapeDtypeStruct((n, X_W), F32), jax.ShapeDtypeStruct((n, X_W), F32)],
        compiler_params=_params(("parallel",)),
        name="memkv",
    )(mem, g, w)


def _xattn_prompt_kernel(q_ref, k_ref, v_ref, o_ref):
    heads = range(X_HEADS)
    sl = [slice(h * X_DH, (h + 1) * X_DH) for h in heads]
    s = [_dot_nt(q_ref[:, sl[h]], k_ref[:, sl[h]].astype(BF16)) * (X_DH ** -0.5) for h in heads]
    p = [jnp.exp(s[h] - jnp.max(s[h], axis=-1, keepdims=True)) for h in heads]
    p = [p[h] / jnp.sum(p[h], axis=-1, keepdims=True) for h in heads]
    for h in heads:
        o_ref[:, sl[h]] = _dot(p[h].astype(BF16), v_ref[:, sl[h]].astype(BF16)).astype(BF16)


def _xattn_prompt(pb, k, v, batch, seq):
    tq = TOK_TILE
    nt = seq // tq
    qblk = 3 * M_W // X_W
    return pl.pallas_call(
        _xattn_prompt_kernel,
        grid=(batch, nt),
        in_specs=[
            pl.BlockSpec((tq, X_W), lambda b, t: (b * nt + t, qblk)),
            pl.BlockSpec((N_MEM, X_W), lambda b, t: (b, 0)),
            pl.BlockSpec((N_MEM, X_W), lambda b, t: (b, 0)),
        ],
        out_specs=pl.BlockSpec((tq, X_W), lambda b, t: (b * nt + t, 0)),
        out_shape=jax.ShapeDtypeStruct((batch * seq, X_W), BF16),
        compiler_params=_params(("parallel", "parallel")),
        name="xattn_prompt",
    )(pb, k, v)


def _xattn_sample_kernel(q_ref, k_ref, v_ref, o_ref, *, seq):
    R = GROUP_B * seq
    assert R == 16 and X_HEADS == 4
    rblk = _seq_index(lax.broadcasted_iota(jnp.int32, (R, 1), 0), seq)
    nk = N_MEM * X_HEADS
    row_head = lax.broadcasted_iota(jnp.int32, (X_HEADS * R, nk), 0) >> 4
    col_head = lax.broadcasted_iota(jnp.int32, (X_HEADS * R, nk), 1) & (X_HEADS - 1)
    own = row_head == col_head
    q4 = jnp.concatenate([q_ref[:, h * X_DH:(h + 1) * X_DH] for h in range(X_HEADS)], axis=0)
    outs = [jnp.zeros((R, X_DH), F32) for _ in range(X_HEADS)]
    group = range(GROUP_B)
    s = [_dot_nt(q4, k_ref[j].astype(BF16)) * (X_DH ** -0.5) for j in group]
    s = [jnp.where(own, s[j], -jnp.inf) for j in group]
    p = [jnp.exp(s[j] - jnp.max(s[j], axis=-1, keepdims=True)) for j in group]
    p = [p[j] / jnp.sum(p[j], axis=-1, keepdims=True) for j in group]
    o4s = [_dot(p[j].astype(BF16), v_ref[j].astype(BF16)) for j in group]
    for j in group:
        o4 = o4s[j]
        outs = [jnp.where(rblk == j, o4[h * R:(h + 1) * R, :], outs[h]) for h in range(X_HEADS)]
    for h in range(X_HEADS):
        o_ref[:, h * X_DH:(h + 1) * X_DH] = outs[h].astype(BF16)


def _xattn_sample(pb, k, v, seq):
    batch = k.shape[0]
    R = GROUP_B * seq
    qblk = 3 * M_W // X_W
    return pl.pallas_call(
        functools.partial(_xattn_sample_kernel, seq=seq),
        grid=(batch // GROUP_B,),
        in_specs=[
            pl.BlockSpec((R, X_W), lambda i: (i, qblk)),
            pl.BlockSpec((GROUP_B, N_MEM * X_HEADS, X_DH), lambda i: (i, 0, 0)),
            pl.BlockSpec((GROUP_B, N_MEM * X_HEADS, X_DH), lambda i: (i, 0, 0)),
        ],
        out_specs=pl.BlockSpec((R, X_W), lambda i: (i, 0)),
        out_shape=jax.ShapeDtypeStruct((batch * seq, X_W), BF16),
        compiler_params=_params(("parallel",)),
        name="xattn_sample",
    )(pb, k, v)


def _outproj_kernel(x_ref, hm_ref, yc_ref, ox_ref, w_ref, g2_ref, wq_ref, x1_ref, xnt_ref, q_ref):
    acc = x_ref[...]
    acc = acc + _dot(hm_ref[...], w_ref[0:M_W, :])
    acc = acc + _dot(yc_ref[...], w_ref[M_W:M_W + C_W, :])
    acc = acc + _dot(ox_ref[...], w_ref[M_W + C_W:, :])
    x1_ref[...] = acc
    xn = _rms(acc, g2_ref[...])
    xnt_ref[...] = jnp.transpose(xn).astype(BF16)
    q_ref[...] = _dot(xn.astype(BF16), wq_ref[...]).astype(BF16)


def _outproj(x, hm, yc, ox, w, g2, wq):
    n = x.shape[0]
    tm = TOK_TILE
    return pl.pallas_call(
        _outproj_kernel,
        grid=(n // tm,),
        in_specs=[
            pl.BlockSpec((tm, D_MODEL), lambda i: (i, 0)),
            pl.BlockSpec((tm, M_W), lambda i: (i, 0)),
            pl.BlockSpec((tm, C_W), lambda i: (i, 0)),
            pl.BlockSpec((tm, X_W), lambda i: (i, 0)),
            _const_spec(w.shape), _const_spec((1, D_MODEL)), _const_spec(wq.shape),
        ],
        out_specs=[
            pl.BlockSpec((tm, D_MODEL), lambda i: (i, 0)),
            pl.BlockSpec((D_MODEL, tm), lambda i: (0, i)),
            pl.BlockSpec((tm, wq.shape[1]), lambda i: (i, 0)),
        ],
        out_shape=[
            jax.ShapeDtypeStruct((n, D_MODEL), F32),
            jax.ShapeDtypeStruct((D_MODEL, n), BF16),
            jax.ShapeDtypeStruct((n, wq.shape[1]), BF16),
        ],
        compiler_params=_params(("parallel",)),
        name="outproj",
    )(x, hm, yc, ox, w, g2, wq)


def _sort16_pairs():
    n, pairs, p = P_TOPK, [], 1
    while p < n:
        k = p
        while k >= 1:
            for j in range(k % p, n - k, 2 * k):
                for i in range(min(k, n - j - k)):
                    if (i + j) // (2 * p) == (i + j + k) // (2 * p):
                        pairs.append((i + j, i + j + k))
            k //= 2
        p *= 2
    return pairs


_SORT16 = _sort16_pairs()
_BITONIC16 = [(i, i + d) for d in (8, 4, 2, 1) for i in range(P_TOPK) if not i & d]
_CAND_ROW_LEN = [P_TOPK // (p + 1) for p in range(P_TOPK)]


def _exchange(xs, pairs):
    for a, b in pairs:
        hi = jnp.maximum(xs[a], xs[b])
        lo = jnp.minimum(xs[a], xs[b])
        xs[a], xs[b] = hi, lo
    return xs


def _merge_top16(xs, ys):
    xs = list(xs)
    for r, y in enumerate(ys):
        xs[P_TOPK - 1 - r] = jnp.maximum(xs[P_TOPK - 1 - r], y)
    return _exchange(xs, _BITONIC16)


def _top16_sorted(st):
    t = st.shape[1]
    x3 = st.reshape(P_NKEYS // SUBLANES, SUBLANES, t)
    xs = _exchange([x3[g] for g in range(P_TOPK)], _SORT16)
    for shift in (4, 2, 1):
        xs = _merge_top16(xs, [pltpu.roll(x, shift, 0) for x in xs])
    return xs


def _theta_and_z(a, b):
    top = [a[0] + b[q] for q in range(P_TOPK)]
    for p in range(1, P_TOPK):
        top = _merge_top16(top, [a[p] + b[q] for q in range(_CAND_ROW_LEN[p])])
    z = jnp.ones_like(top[0])
    for r in range(1, P_TOPK):
        z = z + jnp.exp(top[r] - top[0])
    return top[P_TOPK - 1], z


def _peer_kernel(x_ref, xnt_ref, q_ref, gf_ref, sk_ref, u_ref, vt_ref, y_ref,
                 s1_s, s2_s, f2_s, th_s, a0_s, rz_s, acc_s, wg_s, stage_s):
    e = pl.program_id(1)
    tm = x_ref.shape[0]
    nsb = u_ref.shape[0] // P_NKEYS

    def prologue():
        for h in range(P_HEADS):
            for c, s_s in enumerate((s1_s, s2_s)):
                r0 = (2 * h + c) * P_NKEYS
                st = _dot_nt(sk_ref[2 * h + c], q_ref[:, r0:r0 + P_NKEYS])
                for tc in range(tm // LANES):
                    s_s[h, tc] = st[:, tc * LANES:(tc + 1) * LANES]
        sub = lax.broadcasted_iota(jnp.int32, (SUBLANES, LANES), 0)
        for tc in range(tm // LANES):
            ls = slice(tc * LANES, (tc + 1) * LANES)
            tops = []
            for s_s in (s1_s, s2_s):
                top = [jnp.zeros((SUBLANES, LANES), F32) for _ in range(P_TOPK)]
                for h in range(P_HEADS):
                    srt = _top16_sorted(s_s[h, tc])
                    top = [jnp.where(sub == h, srt[r], top[r]) for r in range(P_TOPK)]
                tops.append(top)
            theta, z = _theta_and_z(tops[0], tops[1])
            th_s[:, ls] = theta
            a0_s[:, ls] = tops[0][0]
            rz_s[:, ls] = 0.5 / z
            b0 = tops[1][0]
            for h in range(P_HEADS):
                f2_s[h, tc] = jnp.exp(s2_s[h, tc] - b0[h:h + 1, :])
        acc_s[...] = jnp.zeros_like(acc_s)

    def gate_times_gelu(at, sb):
        i = e * nsb + sb
        rows = slice(sb * P_NKEYS, (sb + 1) * P_NKEYS)
        for tc in range(tm // LANES):
            ls = slice(tc * LANES, (tc + 1) * LANES)
            w = jnp.zeros((P_NKEYS, LANES), F32)
            for h in range(P_HEADS):
                s1row = s1_s[h, tc, pl.ds(i, 1), :]
                f1row = jnp.exp(s1row - a0_s[h:h + 1, ls]) * rz_s[h:h + 1, ls]
                hit = (s2_s[h, tc] + s1row) >= th_s[h:h + 1, ls]
                w = w + jnp.where(hit, f2_s[h, tc], 0.0) * f1row
            a = at[rows, ls]
            inner = a * (0.7978845608028654 + 0.035677408136300125 * (a * a))
            stage_s[:, ls] = w * (a + a * jnp.tanh(inner))
        wg_s[rows, :] = stage_s[...].astype(BF16)

    @pl.when(e == 0)
    def _():
        prologue()

    at = _dot(u_ref[...], xnt_ref[...])
    for sb in range(nsb):
        gate_times_gelu(at, sb)
    acc_s[...] += _dot(vt_ref[...], wg_s[...])

    @pl.when(e == pl.num_programs(1) - 1)
    def _():
        y = x_ref[...] + jnp.transpose(acc_s[...])
        y_ref[...] = _rms(y, gf_ref[...])


def _peer(x1, xnt, q, gf, sk, u, vt, tm):
    n = x1.shape[0]
    te = PEER_TE
    chunked = (P_HEADS, tm // LANES, P_NKEYS, LANES)
    once = pl.Buffered(1)
    return pl.pallas_call(
        _peer_kernel,
        grid=(n // tm, P_NEXP // te),
        in_specs=[
            pl.BlockSpec((tm, D_MODEL), lambda i, e: (i, 0), pipeline_mode=once),
            pl.BlockSpec((D_MODEL, tm), lambda i, e: (0, i), pipeline_mode=once),
            pl.BlockSpec((tm, q.shape[1]), lambda i, e: (i, 0), pipeline_mode=once),
            _const_spec((1, D_MODEL)), _const_spec(sk.shape),
            pl.BlockSpec((te, D_MODEL), lambda i, e: (e, 0)),
            pl.BlockSpec((D_MODEL, te), lambda i, e: (0, e)),
        ],
        out_specs=pl.BlockSpec((tm, D_MODEL), lambda i, e: (i, 0)),
        out_shape=jax.ShapeDtypeStruct((n, D_MODEL), F32),
        scratch_shapes=[
            pltpu.VMEM(chunked, F32),
            pltpu.VMEM(chunked, F32),
            pltpu.VMEM(chunked, F32),
            pltpu.VMEM((SUBLANES, tm), F32),
            pltpu.VMEM((SUBLANES, tm), F32),
            pltpu.VMEM((SUBLANES, tm), F32),
            pltpu.VMEM((D_MODEL, tm), F32),
            pltpu.VMEM((te, tm), BF16),
            pltpu.VMEM((P_NKEYS, tm), F32),
        ],
        compiler_params=_params(("parallel", "arbitrary")),
        name="peer",
    )(x1, xnt, q, gf, sk, u, vt)


def _transpose_cast_kernel(x_ref, o_ref):
    o_ref[...] = jnp.transpose(x_ref[...]).astype(BF16)


def _transpose_cast(x):
    r, c = x.shape
    tr = PEER_TE
    return pl.pallas_call(
        _transpose_cast_kernel,
        grid=(r // tr,),
        in_specs=[pl.BlockSpec((tr, c), lambda i: (i, 0))],
        out_specs=pl.BlockSpec((c, tr), lambda i: (0, i)),
        out_shape=jax.ShapeDtypeStruct((c, r), BF16),
        compiler_params=_params(("parallel",)),
        name="transpose_cast",
    )(x)


def kernel(x_prompt, x_sample, mem_prompt, cache_mem_k, cache_mem_v, state_C, state_n, state_m, state_conv,
           norm1_g, w_in, b_gate, mlstm_norm_g, conv_w, conv_b, conv_ln_g, conv_ln_b, mem_norm_g, w_mk, w_mv,
           w_out, norm2_g, peer_wq, peer_subkeys, peer_u, peer_v, final_g):
    depth = w_in.shape[0]
    assert depth == 1
    bp, seq_p, _ = x_prompt.shape
    bs, seq_s, _ = x_sample.shape
    l = 0

    w = w_in[l]
    o0, g0, a0, gg0, qx0 = 3 * M_W, 4 * M_W, 4 * M_W + 2 * M_HEADS, 4 * M_W + 2 * M_HEADS + C_W, \
        4 * M_W + 2 * M_HEADS + 2 * C_W
    wb = jnp.concatenate([w[:, 0:3 * M_W], w[:, qx0:qx0 + X_W]], axis=1).astype(BF16)
    wf = jnp.concatenate([w[:, o0:o0 + M_W], w[:, a0:gg0 + C_W]], axis=1).astype(BF16)
    wgate = w[:, g0:g0 + 2 * M_HEADS]
    wg = jnp.pad(wgate.astype(BF16), ((0, 0), (0, LANES - 2 * M_HEADS)))
    wgt = jnp.pad(wgate.T.astype(BF16), ((0, 2 * SUBLANES - 2 * M_HEADS), (0, 0)))
    bgr = jnp.pad(b_gate[l][None, :], ((0, 0), (0, LANES - 2 * M_HEADS)))
    bgc = jnp.pad(b_gate[l][:, None], ((0, 2 * SUBLANES - 2 * M_HEADS), (0, 0)))
    g1 = norm1_g[l][None, :]
    gnorm = mlstm_norm_g[l].reshape(1, M_W)
    cw = jnp.pad(conv_w[l], ((0, 1), (0, 0)))
    cb, clg, clb = conv_b[l][None, :], conv_ln_g[l][None, :], conv_ln_b[l][None, :]
    wkv = jnp.concatenate([w_mk[l], w_mv[l]], axis=1).astype(BF16)
    wo = w_out[l].astype(BF16)
    wq = peer_wq[l].astype(BF16)
    sk = peer_subkeys[l].reshape(2 * P_HEADS, P_NKEYS, P_NKEYS).astype(BF16)
    u = peer_u[l].astype(BF16)
    vt = _transpose_cast(peer_v[l])
    g2 = norm2_g[l][None, :]
    gf = final_g[None, :]

    xp = x_prompt.reshape(bp * seq_p, D_MODEL)
    xs = x_sample.reshape(bs * seq_s, D_MODEL)

    pb, pf, gcol, grow = _inproj(xp, g1, wb, wf, wg, wgt, bgr, bgc)
    hm, c_p, n_p, m_p = _mlstm_prompt(pb, pf, gcol, grow, gnorm, bp, seq_p)
    yc, conv_p = _conv_prompt(pf, cw, cb, clg, clb, bp, seq_p)
    mk, mv = _memkv(mem_prompt.reshape(bp * N_MEM, D_MODEL), mem_norm_g[l][None, :], wkv)
    ox = _xattn_prompt(pb, mk, mv, bp, seq_p)
    x1, xnt, q = _outproj(xp, hm, yc, ox, wo, g2, wq)
    y_p = _peer(x1, xnt, q, gf, sk, u, vt, TOK_TILE)

    pb, pf, gcol, _ = _inproj(xs, g1, wb, wf, wg, wgt, bgr, bgc)
    mtok = jnp.pad(jnp.repeat(state_m[l], seq_s, axis=0), ((0, 0), (0, LANES - M_HEADS)))
    hm, c_s, n_s, mtok_new = _mlstm_sample(pb, pf, gcol, mtok, gnorm, state_C[l], state_n[l], seq_s)
    yc, conv_s = _conv_sample(pf, state_conv[l], cw, cb, clg, clb, seq_s)
    ox = _xattn_sample(pb, cache_mem_k.reshape(bs, N_MEM * X_HEADS, X_DH),
                       cache_mem_v.reshape(bs, N_MEM * X_HEADS, X_DH), seq_s)
    x1, xnt, q = _outproj(xs, hm, yc, ox, wo, g2, wq)
    y_s = _peer(x1, xnt, q, gf, sk, u, vt, TOK_TILE)

    m_s = mtok_new.reshape(bs, seq_s, LANES)[:, seq_s - 1, :M_HEADS]
    return (y_p.reshape(bp, seq_p, D_MODEL),
            y_s.reshape(bs, seq_s, D_MODEL),
            mk.reshape(1, bp, N_MEM, X_HEADS, X_DH),
            mv.reshape(1, bp, N_MEM, X_HEADS, X_DH),
            c_p[None], n_p[None], m_p[None, :, :M_HEADS, 0], conv_p[None],
            c_s[None], n_s[None], m_s[None], conv_s[None])
```

```python
import functools

import jax
import jax.numpy as jnp
from jax import lax
from jax.experimental import pallas as pl
from jax.experimental.pallas import tpu as pltpu

F32 = jnp.float32
BF16 = jnp.bfloat16

D_MODEL = 2048
M_HEADS = 4
M_DH = 256
M_W = M_HEADS * M_DH
C_W = 512
CONV_K = 31
CONV_PAST = CONV_K - 1
X_HEADS = 4
X_DH = 128
X_W = X_HEADS * X_DH
N_MEM = 256
P_HEADS = 8
P_NKEYS = 128
P_NEXP = P_NKEYS * P_NKEYS
P_TOPK = 16
EPS = 1e-6

SUBLANES = 8
LANES = 128
VMEM_LIMIT_BYTES = 58 * 1024 * 1024

MLSTM_L = 256
GROUP_B = 4
TOK_TILE = 512
PEER_TE = 1024
CONV_T = 256
CONV_OFF = 32


def _dot(a, b):
    return jnp.dot(a, b, preferred_element_type=F32)


def _dot_nt(a, b):
    return lax.dot_general(a, b, (((1,), (1,)), ((), ())), preferred_element_type=F32)


def _dot_tn(a, b):
    return lax.dot_general(a, b, (((0,), (0,)), ((), ())), preferred_element_type=F32)


def _dot_exact(a, b):
    return jnp.dot(a, b, preferred_element_type=F32, precision=lax.Precision.HIGHEST)


def _rms(x, g):
    return x * lax.rsqrt(jnp.mean(x * x, axis=-1, keepdims=True) + EPS) * g


def _sigmoid(x):
    return 1.0 / (1.0 + jnp.exp(-x))


def _log_sigmoid(x):
    return jnp.minimum(x, 0.0) - jnp.log1p(jnp.exp(-jnp.abs(x)))


def _params(sem):
    return pltpu.CompilerParams(dimension_semantics=sem, vmem_limit_bytes=VMEM_LIMIT_BYTES)


def _const_spec(shape):
    nd = len(shape)
    return pl.BlockSpec(shape, lambda *_: (0,) * nd, pipeline_mode=pl.Buffered(1))


def _inproj_kernel(x_ref, g_ref, wb_ref, wf_ref, wg_ref, wgt_ref, bgr_ref, bgc_ref,
                   pb_ref, pf_ref, gcol_ref, grow_ref):
    xn = _rms(x_ref[...], g_ref[...]).astype(BF16)
    for c in range(0, wb_ref.shape[1], 512):
        p = _dot(xn, wb_ref[:, c:c + 512])
        if M_W <= c < 2 * M_W:
            p = p * (M_DH ** -0.5)
        pb_ref[:, c:c + 512] = p.astype(BF16)
    for c in range(0, wf_ref.shape[1], 512):
        pf_ref[:, c:c + 512] = _dot(xn, wf_ref[:, c:c + 512])
    gc = _dot(xn, wg_ref[...]) + bgr_ref[...]
    lane = lax.broadcasted_iota(jnp.int32, gc.shape, 1)
    gcol_ref[...] = jnp.where(lane >= M_HEADS, _log_sigmoid(gc), gc)
    gr = _dot_nt(wgt_ref[...], xn) + bgc_ref[...]
    row = lax.broadcasted_iota(jnp.int32, gr.shape, 0)
    grow_ref[...] = jnp.where(row >= M_HEADS, _log_sigmoid(gr), gr)[0:SUBLANES, :]


def _inproj(x, g, wb, wf, wg, wgt, bgr, bgc):
    n = x.shape[0]
    tm = TOK_TILE
    return pl.pallas_call(
        _inproj_kernel,
        grid=(n // tm,),
        in_specs=[
            pl.BlockSpec((tm, D_MODEL), lambda i: (i, 0)),
            _const_spec((1, D_MODEL)),
            _const_spec(wb.shape), _const_spec(wf.shape), _const_spec(wg.shape),
            _const_spec(wgt.shape), _const_spec(bgr.shape), _const_spec(bgc.shape),
        ],
        out_specs=[
            pl.BlockSpec((tm, wb.shape[1]), lambda i: (i, 0)),
            pl.BlockSpec((tm, wf.shape[1]), lambda i: (i, 0)),
            pl.BlockSpec((tm, LANES), lambda i: (i, 0)),
            pl.BlockSpec((SUBLANES, tm), lambda i: (0, i)),
        ],
        out_shape=[
            jax.ShapeDtypeStruct((n, wb.shape[1]), BF16),
            jax.ShapeDtypeStruct((n, wf.shape[1]), F32),
            jax.ShapeDtypeStruct((n, LANES), F32),
            jax.ShapeDtypeStruct((SUBLANES, n), F32),
        ],
        compiler_params=_params(("parallel",)),
        name="inproj",
    )(x, g, wb, wf, wg, wgt, bgr, bgc)


def _mlstm_head_out(hh, gnorm, o):
    hn = hh * lax.rsqrt(jnp.mean(hh * hh, axis=-1, keepdims=True) + EPS) * gnorm
    return (hn * _sigmoid(o)).astype(BF16)


def _mlstm_prompt_kernel(q_ref, k_ref, v_ref, o_ref, gcol_ref, grow_ref, gn_ref,
                         hm_ref, c_out_ref, n_out_ref, m_out_ref, c_s, n_s, m_s):
    c = pl.program_id(1)
    L = MLSTM_L

    @pl.when(c == 0)
    def _():
        c_s[...] = jnp.zeros_like(c_s)
        n_s[...] = jnp.zeros_like(n_s)
        m_s[...] = jnp.zeros_like(m_s)

    row = lax.broadcasted_iota(jnp.int32, (L, L), 0)
    col = lax.broadcasted_iota(jnp.int32, (L, L), 1)
    causal = col <= row
    tril = jnp.where(causal, 1.0, 0.0).astype(F32)
    triu = jnp.where(row <= col, 1.0, 0.0).astype(F32)
    gcol = gcol_ref[...]
    grow = grow_ref[...]
    bc_all = _dot_exact(tril, gcol)
    br_all = _dot_exact(grow, triu)

    heads = range(M_HEADS)
    sl = [slice(h * M_DH, (h + 1) * M_DH) for h in heads]
    q = [q_ref[:, sl[h]] for h in heads]
    k = [k_ref[:, sl[h]] for h in heads]
    v = [v_ref[:, sl[h]] for h in heads]
    li_r = [grow[h:h + 1, :] for h in heads]
    li_c = [gcol[:, h:h + 1] for h in heads]
    b_c = [bc_all[:, M_HEADS + h:M_HEADS + h + 1] for h in heads]
    b_r = [br_all[M_HEADS + h:M_HEADS + h + 1, :] for h in heads]
    m_prev = [m_s[h:h + 1, 0:1] for h in heads]
    cmat = [c_s[h] for h in heads]
    nrow = [n_s[h:h + 1, :] for h in heads]
    qk = [_dot_nt(q[h], k[h]) for h in heads]
    qc = [_dot_nt(q[h], cmat[h].astype(BF16)) for h in heads]
    log_d = [jnp.where(causal, b_c[h] - b_r[h] + li_r[h], -jnp.inf) for h in heads]
    log_inter = [b_c[h] + m_prev[h] for h in heads]
    m_t = [jnp.maximum(log_inter[h], jnp.max(log_d[h], axis=1, keepdims=True)) for h in heads]
    dm = [jnp.exp(log_d[h] - m_t[h]) for h in heads]
    inter = [jnp.exp(log_inter[h] - m_t[h]) for h in heads]
    s = [qk[h] * dm[h] for h in heads]
    num = [_dot(s[h].astype(BF16), v[h]) + inter[h] * qc[h] for h in heads]
    qn = [jnp.sum(q[h].astype(F32) * nrow[h], axis=1, keepdims=True) for h in heads]
    den = [jnp.sum(s[h], axis=1, keepdims=True) + inter[h] * qn[h] for h in heads]
    hh = [num[h] / jnp.maximum(jnp.abs(den[h]), jnp.exp(-m_t[h])) for h in heads]
    for h in heads:
        hm_ref[:, sl[h]] = _mlstm_head_out(hh[h], gn_ref[:, sl[h]], o_ref[:, sl[h]])

    m_new = [m_t[h][L - 1:L, :] for h in heads]
    b_last = [b_c[h][L - 1:L, :] for h in heads]
    w_r = [jnp.exp(b_last[h] - b_r[h] + li_r[h] - m_new[h]) for h in heads]
    w_c = [jnp.exp(b_last[h] - b_c[h] + li_c[h] - m_new[h]) for h in heads]
    decay = [jnp.exp(b_last[h] + m_prev[h] - m_new[h]) for h in heads]
    vw = [(v[h].astype(F32) * w_c[h]).astype(BF16) for h in heads]
    upd = [_dot_tn(vw[h], k[h]) for h in heads]
    wk = [_dot(jnp.broadcast_to(w_r[h], (SUBLANES, L)).astype(BF16), k[h]) for h in heads]
    for h in heads:
        c_s[h] = decay[h] * cmat[h] + upd[h]
        n_s[h:h + 1, :] = decay[h] * nrow[h] + wk[h][0:1, :]
        m_s[h:h + 1, :] = jnp.broadcast_to(m_new[h], (1, LANES))

    @pl.when(c == pl.num_programs(1) - 1)
    def _():
        c_out_ref[0] = c_s[...]
        n_out_ref[0] = n_s[0:M_HEADS, :]
        m_out_ref[0] = m_s[...]


def _mlstm_prompt(pb, pf, gcol, grow, gnorm, batch, seq):
    L = MLSTM_L
    nc = seq // L
    n = batch * seq
    rows = lambda b, c: b * nc + c
    return pl.pallas_call(
        _mlstm_prompt_kernel,
        grid=(batch, nc),
        in_specs=[
            pl.BlockSpec((L, M_W), lambda b, c: (rows(b, c), 0)),
            pl.BlockSpec((L, M_W), lambda b, c: (rows(b, c), 1)),
            pl.BlockSpec((L, M_W), lambda b, c: (rows(b, c), 2)),
            pl.BlockSpec((L, M_W), lambda b, c: (rows(b, c), 0)),
            pl.BlockSpec((L, LANES), lambda b, c: (rows(b, c), 0)),
            pl.BlockSpec((SUBLANES, L), lambda b, c: (0, rows(b, c))),
            _const_spec((1, M_W)),
        ],
        out_specs=[
            pl.BlockSpec((L, M_W), lambda b, c: (rows(b, c), 0)),
            pl.BlockSpec((1, M_HEADS, M_DH, M_DH), lambda b, c: (b, 0, 0, 0)),
            pl.BlockSpec((1, M_HEADS, M_DH), lambda b, c: (b, 0, 0)),
            pl.BlockSpec((1, SUBLANES, LANES), lambda b, c: (b, 0, 0)),
        ],
        out_shape=[
            jax.ShapeDtypeStruct((n, M_W), BF16),
            jax.ShapeDtypeStruct((batch, M_HEADS, M_DH, M_DH), F32),
            jax.ShapeDtypeStruct((batch, M_HEADS, M_DH), F32),
            jax.ShapeDtypeStruct((batch, SUBLANES, LANES), F32),
        ],
        scratch_shapes=[
            pltpu.VMEM((M_HEADS, M_DH, M_DH), F32),
            pltpu.VMEM((SUBLANES, M_DH), F32),
            pltpu.VMEM((SUBLANES, LANES), F32),
        ],
        compiler_params=_params(("parallel", "arbitrary")),
        name="mlstm_prompt",
    )(pb, pb, pb, pf, gcol, grow, gnorm)


def _seq_index(token, seq):
    assert seq & (seq - 1) == 0
    return token >> (seq.bit_length() - 1)


def _to_row(colvec, eye):
    return jnp.sum(jnp.where(eye, colvec, 0.0), axis=0, keepdims=True)


def _mlstm_sample_kernel(q_ref, k_ref, v_ref, o_ref, gcol_ref, mtok_ref, gn_ref, c_in_ref, n_in_ref,
                         hm_ref, c_out_ref, n_out_ref, mtok_out_ref, *, seq):
    R = GROUP_B * seq
    row = lax.broadcasted_iota(jnp.int32, (R, R), 0)
    col = lax.broadcasted_iota(jnp.int32, (R, R), 1)
    eye = row == col
    same = _seq_index(row, seq) == _seq_index(col, seq)
    causal = same & (col <= row)
    rblk = _seq_index(lax.broadcasted_iota(jnp.int32, (R, 1), 0), seq)
    lane = lax.broadcasted_iota(jnp.int32, (R, LANES), 1)
    gcol = gcol_ref[...]
    mtok = mtok_ref[...]
    m_out = jnp.zeros((R, LANES), F32)

    def per_block(vals):
        out = jnp.zeros((R, 1), F32)
        for j in range(GROUP_B):
            out = jnp.where(rblk == j, vals[j], out)
        return out

    heads = range(M_HEADS)
    group = range(GROUP_B)
    last = [j * seq + seq - 1 for j in group]
    sl = [slice(h * M_DH, (h + 1) * M_DH) for h in heads]
    q = [q_ref[:, sl[h]] for h in heads]
    k = [k_ref[:, sl[h]] for h in heads]
    v = [v_ref[:, sl[h]] for h in heads]
    qk = [_dot_nt(q[h], k[h]) for h in heads]
    qc_jh = [[_dot_nt(q[h], c_in_ref[j, h].astype(BF16)) for j in group] for h in heads]
    li_c = [gcol[:, h:h + 1] for h in heads]
    lf_c = [gcol[:, M_HEADS + h:M_HEADS + h + 1] for h in heads]
    li_r = [_to_row(li_c[h], eye) for h in heads]
    lf_r = [_to_row(lf_c[h], eye) for h in heads]
    m_prev = [mtok[:, h:h + 1] for h in heads]
    b_c = [jnp.sum(jnp.where(causal, lf_r[h], 0.0), axis=1, keepdims=True) for h in heads]
    b_r = [_to_row(b_c[h], eye) for h in heads]
    log_d = [jnp.where(causal, b_c[h] - b_r[h] + li_r[h], -jnp.inf) for h in heads]
    log_inter = [b_c[h] + m_prev[h] for h in heads]
    m_t = [jnp.maximum(log_inter[h], jnp.max(log_d[h], axis=1, keepdims=True)) for h in heads]
    dm = [jnp.exp(log_d[h] - m_t[h]) for h in heads]
    inter = [jnp.exp(log_inter[h] - m_t[h]) for h in heads]
    s = [qk[h] * dm[h] for h in heads]
    qc, ntok = [], []
    for h in heads:
        qc_h = jnp.zeros((R, M_DH), F32)
        ntok_h = jnp.zeros((R, M_DH), F32)
        for j in group:
            qc_h = jnp.where(rblk == j, qc_jh[h][j], qc_h)
            ntok_h = jnp.where(rblk == j, n_in_ref[j, h:h + 1, :], ntok_h)
        qc.append(qc_h)
        ntok.append(ntok_h)
    num = [_dot(s[h].astype(BF16), v[h]) + inter[h] * qc[h] for h in heads]
    qn = [jnp.sum(q[h].astype(F32) * ntok[h], axis=1, keepdims=True) for h in heads]
    den = [jnp.sum(s[h], axis=1, keepdims=True) + inter[h] * qn[h] for h in heads]
    hh = [num[h] / jnp.maximum(jnp.abs(den[h]), jnp.exp(-m_t[h])) for h in heads]
    for h in heads:
        hm_ref[:, sl[h]] = _mlstm_head_out(hh[h], gn_ref[:, sl[h]], o_ref[:, sl[h]])

    m_new = [per_block([m_t[h][r:r + 1, :] for r in last]) for h in heads]
    b_last = [per_block([b_c[h][r:r + 1, :] for r in last]) for h in heads]
    w_c = [jnp.exp(b_last[h] - b_c[h] + li_c[h] - m_new[h]) for h in heads]
    decay_c = [jnp.exp(b_last[h] + m_prev[h] - m_new[h]) for h in heads]
    vw = [v[h].astype(F32) * w_c[h] for h in heads]
    wk = [w_c[h] * k[h].astype(F32) for h in heads]
    upd = [[_dot_tn(jnp.where(rblk == j, vw[h], 0.0).astype(BF16), k[h]) for j in group] for h in heads]
    for h in heads:
        for j in group:
            decay = decay_c[h][last[j]:last[j] + 1, :]
            c_out_ref[j, h] = decay * c_in_ref[j, h] + upd[h][j]
            n_out_ref[j, h:h + 1, :] = (decay * n_in_ref[j, h:h + 1, :]
                                        + jnp.sum(jnp.where(rblk == j, wk[h], 0.0), axis=0, keepdims=True))
        m_out = jnp.where(lane == h, m_new[h], m_out)
    mtok_out_ref[...] = m_out


def _mlstm_sample(pb, pf, gcol, mtok, gnorm, state_c, state_n, seq):
    batch = state_c.shape[0]
    R = GROUP_B * seq
    n = batch * seq
    return pl.pallas_call(
        functools.partial(_mlstm_sample_kernel, seq=seq),
        grid=(batch // GROUP_B,),
        in_specs=[
            pl.BlockSpec((R, M_W), lambda i: (i, 0)),
            pl.BlockSpec((R, M_W), lambda i: (i, 1)),
            pl.BlockSpec((R, M_W), lambda i: (i, 2)),
            pl.BlockSpec((R, M_W), lambda i: (i, 0)),
            pl.BlockSpec((R, LANES), lambda i: (i, 0)),
            pl.BlockSpec((R, LANES), lambda i: (i, 0)),
            _const_spec((1, M_W)),
            pl.BlockSpec((GROUP_B, M_HEADS, M_DH, M_DH), lambda i: (i, 0, 0, 0)),
            pl.BlockSpec((GROUP_B, M_HEADS, M_DH), lambda i: (i, 0, 0)),
        ],
        out_specs=[
            pl.BlockSpec((R, M_W), lambda i: (i, 0)),
            pl.BlockSpec((GROUP_B, M_HEADS, M_DH, M_DH), lambda i: (i, 0, 0, 0)),
            pl.BlockSpec((GROUP_B, M_HEADS, M_DH), lambda i: (i, 0, 0)),
            pl.BlockSpec((R, LANES), lambda i: (i, 0)),
        ],
        out_shape=[
            jax.ShapeDtypeStruct((n, M_W), BF16),
            jax.ShapeDtypeStruct(state_c.shape, F32),
            jax.ShapeDtypeStruct(state_n.shape, F32),
            jax.ShapeDtypeStruct((n, LANES), F32),
        ],
        compiler_params=_params(("parallel",)),
        name="mlstm_sample",
    )(pb, pb, pb, pf, gcol, mtok, gnorm, state_c, state_n)


def _conv_out(y, lg, lb):
    mu = jnp.mean(y, axis=-1, keepdims=True)
    yc = y - mu
    var = jnp.mean(yc * yc, axis=-1, keepdims=True)
    z = yc * lax.rsqrt(var + EPS) * lg + lb
    return (z * _sigmoid(z)).astype(BF16)


def _conv_prompt_kernel(a_ref, g_ref, w_ref, b_ref, lg_ref, lb_ref, yc_ref, new_ref, xp_s, xsh_s):
    t = pl.program_id(1)
    T = CONV_T

    @pl.when(t == 0)
    def _():
        xp_s[0:CONV_OFF, :] = jnp.zeros((CONV_OFF, C_W), F32)

    xp_s[CONV_OFF:CONV_OFF + T, :] = a_ref[...] * _sigmoid(g_ref[...])
    base = CONV_OFF - CONV_PAST
    acc = jnp.zeros((T, C_W), F32) + b_ref[...]
    for b in range(SUBLANES):
        taps = [j for j in range(b, CONV_K, SUBLANES)]
        span = taps[-1] - b + T
        xsh_s[b, 0:span, :] = xp_s[base + b:base + b + span, :]
        for j in taps:
            acc = acc + w_ref[j:j + 1, :] * xsh_s[b, j - b:j - b + T, :]
    yc_ref[...] = _conv_out(acc, lg_ref[...], lb_ref[...])
    tail = xp_s[CONV_OFF + T - CONV_PAST:CONV_OFF + T, :]
    xp_s[base:CONV_OFF, :] = tail

    @pl.when(t == pl.num_programs(1) - 1)
    def _():
        new_ref[0] = tail


def _conv_prompt(pf, w, b, lg, lb, batch, seq):
    T = CONV_T
    nt = seq // T
    n = batch * seq
    return pl.pallas_call(
        _conv_prompt_kernel,
        grid=(batch, nt),
        in_specs=[
            pl.BlockSpec((T, C_W), lambda bi, t: (bi * nt + t, 2)),
            pl.BlockSpec((T, C_W), lambda bi, t: (bi * nt + t, 3)),
            _const_spec(w.shape), _const_spec((1, C_W)), _const_spec((1, C_W)), _const_spec((1, C_W)),
        ],
        out_specs=[
            pl.BlockSpec((T, C_W), lambda bi, t: (bi * nt + t, 0)),
            pl.BlockSpec((1, CONV_PAST, C_W), lambda bi, t: (bi, 0, 0)),
        ],
        out_shape=[
            jax.ShapeDtypeStruct((n, C_W), BF16),
            jax.ShapeDtypeStruct((batch, CONV_PAST, C_W), F32),
        ],
        scratch_shapes=[pltpu.VMEM((CONV_OFF + T, C_W), F32),
                        pltpu.VMEM((SUBLANES, CONV_OFF + T, C_W), F32)],
        compiler_params=_params(("parallel", "arbitrary")),
        name="conv_prompt",
    )(pf, pf, w, b, lg, lb)


def _conv_sample_kernel(a_ref, g_ref, past_ref, w_ref, b_ref, lg_ref, lb_ref, yc_ref, new_ref,
                        xp_s, y_s, *, seq, gb):
    u = a_ref[...] * _sigmoid(g_ref[...])
    for j in range(gb):
        xp_s[0:CONV_PAST, :] = past_ref[j]
        xp_s[CONV_PAST:CONV_PAST + seq, :] = u[j * seq:(j + 1) * seq, :]
        acc = jnp.zeros((seq, C_W), F32) + b_ref[...]
        for i in range(CONV_K):
            acc = acc + w_ref[i:i + 1, :] * xp_s[i:i + seq, :]
        y_s[j * seq:(j + 1) * seq, :] = acc
        new_ref[j] = xp_s[seq:seq + CONV_PAST, :]
    yc_ref[...] = _conv_out(y_s[...], lg_ref[...], lb_ref[...])


def _conv_sample(pf, past, w, b, lg, lb, seq):
    batch = past.shape[0]
    gb = 8
    R = gb * seq
    n = batch * seq
    return pl.pallas_call(
        functools.partial(_conv_sample_kernel, seq=seq, gb=gb),
        grid=(batch // gb,),
        in_specs=[
            pl.BlockSpec((R, C_W), lambda i: (i, 2)),
            pl.BlockSpec((R, C_W), lambda i: (i, 3)),
            pl.BlockSpec((gb, CONV_PAST, C_W), lambda i: (i, 0, 0)),
            _const_spec(w.shape), _const_spec((1, C_W)), _const_spec((1, C_W)), _const_spec((1, C_W)),
        ],
        out_specs=[
            pl.BlockSpec((R, C_W), lambda i: (i, 0)),
            pl.BlockSpec((gb, CONV_PAST, C_W), lambda i: (i, 0, 0)),
        ],
        out_shape=[
            jax.ShapeDtypeStruct((n, C_W), BF16),
            jax.ShapeDtypeStruct(past.shape, F32),
        ],
        scratch_shapes=[pltpu.VMEM((CONV_PAST + seq + 6, C_W), F32), pltpu.VMEM((R, C_W), F32)],
        compiler_params=_params(("parallel",)),
        name="conv_sample",
    )(pf, pf, past, w, b, lg, lb)


def _memkv_kernel(x_ref, g_ref, w_ref, k_ref, v_ref):
    xn = _rms(x_ref[...], g_ref[...]).astype(BF16)
    k_ref[...] = _dot(xn, w_ref[:, 0:X_W])
    v_ref[...] = _dot(xn, w_ref[:, X_W:2 * X_W])


def _memkv(mem, g, w):
    n = mem.shape[0]
    tm = TOK_TILE
    return pl.pallas_call(
        _memkv_kernel,
        grid=(n // tm,),
        in_specs=[pl.BlockSpec((tm, D_MODEL), lambda i: (i, 0)), _const_spec((1, D_MODEL)),
                  _const_spec(w.shape)],
        out_specs=[pl.BlockSpec((tm, X_W), lambda i: (i, 0)), pl.BlockSpec((tm, X_W), lambda i: (i, 0))],
        out_shape=[jax.ShapeDtypeStruct((n, X_W), F32), jax.ShapeDtypeStruct((n, X_W), F32)],
        compiler_params=_params(("parallel",)),
        name="memkv",
    )(mem, g, w)


def _xattn_prompt_kernel(q_ref, k_ref, v_ref, o_ref):
    heads = range(X_HEADS)
    sl = [slice(h * X_DH, (h + 1) * X_DH) for h in heads]
    s = [_dot_nt(q_ref[:, sl[h]], k_ref[:, sl[h]].astype(BF16)) * (X_DH ** -0.5) for h in heads]
    p = [jnp.exp(s[h] - jnp.max(s[h], axis=-1, keepdims=True)) for h in heads]
    p = [p[h] / jnp.sum(p[h], axis=-1, keepdims=True) for h in heads]
    for h in heads:
        o_ref[:, sl[h]] = _dot(p[h].astype(BF16), v_ref[:, sl[h]].astype(BF16)).astype(BF16)


def _xattn_prompt(pb, k, v, batch, seq):
    tq = TOK_TILE
    nt = seq // tq
    qblk = 3 * M_W // X_W
    return pl.pallas_call(
        _xattn_prompt_kernel,
        grid=(batch, nt),
        in_specs=[
            pl.BlockSpec((tq, X_W), lambda b, t: (b * nt + t, qblk)),
            pl.BlockSpec((N_MEM, X_W), lambda b, t: (b, 0)),
            pl.BlockSpec((N_MEM, X_W), lambda b, t: (b, 0)),
        ],
        out_specs=pl.BlockSpec((tq, X_W), lambda b, t: (b * nt + t, 0)),
        out_shape=jax.ShapeDtypeStruct((batch * seq, X_W), BF16),
        compiler_params=_params(("parallel", "parallel")),
        name="xattn_prompt",
    )(pb, k, v)


def _xattn_sample_kernel(q_ref, k_ref, v_ref, o_ref, *, seq):
    R = GROUP_B * seq
    assert R == 16 and X_HEADS == 4
    rblk = _seq_index(lax.broadcasted_iota(jnp.int32, (R, 1), 0), seq)
    nk = N_MEM * X_HEADS
    row_head = lax.broadcasted_iota(jnp.int32, (X_HEADS * R, nk), 0) >> 4
    col_head = lax.broadcasted_iota(jnp.int32, (X_HEADS * R, nk), 1) & (X_HEADS - 1)
    own = row_head == col_head
    q4 = jnp.concatenate([q_ref[:, h * X_DH:(h + 1) * X_DH] for h in range(X_HEADS)], axis=0)
    outs = [jnp.zeros((R, X_DH), F32) for _ in range(X_HEADS)]
    group = range(GROUP_B)
    s = [_dot_nt(q4, k_ref[j].astype(BF16)) * (X_DH ** -0.5) for j in group]
    s = [jnp.where(own, s[j], -jnp.inf) for j in group]
    p = [jnp.exp(s[j] - jnp.max(s[j], axis=-1, keepdims=True)) for j in group]
    p = [p[j] / jnp.sum(p[j], axis=-1, keepdims=True) for j in group]
    o4s = [_dot(p[j].astype(BF16), v_ref[j].astype(BF16)) for j in group]
    for j in group:
        o4 = o4s[j]
        outs = [jnp.where(rblk == j, o4[h * R:(h + 1) * R, :], outs[h]) for h in range(X_HEADS)]
    for h in range(X_HEADS):
        o_ref[:, h * X_DH:(h + 1) * X_DH] = outs[h].astype(BF16)


def _xattn_sample(pb, k, v, seq):
    batch = k.shape[0]
    R = GROUP_B * seq
    qblk = 3 * M_W // X_W
    return pl.pallas_call(
        functools.partial(_xattn_sample_kernel, seq=seq),
        grid=(batch // GROUP_B,),
        in_specs=[
            pl.BlockSpec((R, X_W), lambda i: (i, qblk)),
            pl.BlockSpec((GROUP_B, N_MEM * X_HEADS, X_DH), lambda i: (i, 0, 0)),
            pl.BlockSpec((GROUP_B, N_MEM * X_HEADS, X_DH), lambda i: (i, 0, 0)),
        ],
        out_specs=pl.BlockSpec((R, X_W), lambda i: (i, 0)),
        out_shape=jax.ShapeDtypeStruct((batch * seq, X_W), BF16),
        compiler_params=_params(("parallel",)),
        name="xattn_sample",
    )(pb, k, v)


def _outproj_kernel(x_ref, hm_ref, yc_ref, ox_ref, w_ref, g2_ref, wq_ref, x1_ref, xnt_ref, q_ref):
    acc = x_ref[...]
    acc = acc + _dot(hm_ref[...], w_ref[0:M_W, :])
    acc = acc + _dot(yc_ref[...], w_ref[M_W:M_W + C_W, :])
    acc = acc + _dot(ox_ref[...], w_ref[M_W + C_W:, :])
    x1_ref[...] = acc
    xn = _rms(acc, g2_ref[...])
    xnt_ref[...] = jnp.transpose(xn).astype(BF16)
    q_ref[...] = _dot(xn.astype(BF16), wq_ref[...]).astype(BF16)


def _outproj(x, hm, yc, ox, w, g2, wq):
    n = x.shape[0]
    tm = TOK_TILE
    return pl.pallas_call(
        _outproj_kernel,
        grid=(n // tm,),
        in_specs=[
            pl.BlockSpec((tm, D_MODEL), lambda i: (i, 0)),
            pl.BlockSpec((tm, M_W), lambda i: (i, 0)),
            pl.BlockSpec((tm, C_W), lambda i: (i, 0)),
            pl.BlockSpec((tm, X_W), lambda i: (i, 0)),
            _const_spec(w.shape), _const_spec((1, D_MODEL)), _const_spec(wq.shape),
        ],
        out_specs=[
            pl.BlockSpec((tm, D_MODEL), lambda i: (i, 0)),
            pl.BlockSpec((D_MODEL, tm), lambda i: (0, i)),
            pl.BlockSpec((tm, wq.shape[1]), lambda i: (i, 0)),
        ],
        out_shape=[
            jax.ShapeDtypeStruct((n, D_MODEL), F32),
            jax.ShapeDtypeStruct((D_MODEL, n), BF16),
            jax.ShapeDtypeStruct((n, wq.shape[1]), BF16),
        ],
        compiler_params=_params(("parallel",)),
        name="outproj",
    )(x, hm, yc, ox, w, g2, wq)


def _sort16_pairs():
    n, pairs, p = P_TOPK, [], 1
    while p < n:
        k = p
        while k >= 1:
            for j in range(k % p, n - k, 2 * k):
                for i in range(min(k, n - j - k)):
                    if (i + j) // (2 * p) == (i + j + k) // (2 * p):
                        pairs.append((i + j, i + j + k))
            k //= 2
        p *= 2
    return pairs


_SORT16 = _sort16_pairs()
_BITONIC16 = [(i, i + d) for d in (8, 4, 2, 1) for i in range(P_TOPK) if not i & d]
_CAND_ROW_LEN = [P_TOPK // (p + 1) for p in range(P_TOPK)]


def _exchange(xs, pairs):
    for a, b in pairs:
        hi = jnp.maximum(xs[a], xs[b])
        lo = jnp.minimum(xs[a], xs[b])
        xs[a], xs[b] = hi, lo
    return xs


def _merge_top16(xs, ys):
    xs = list(xs)
    for r, y in enumerate(ys):
        xs[P_TOPK - 1 - r] = jnp.maximum(xs[P_TOPK - 1 - r], y)
    return _exchange(xs, _BITONIC16)


def _top16_sorted(st):
    t = st.shape[1]
    x3 = st.reshape(P_NKEYS // SUBLANES, SUBLANES, t)
    xs = _exchange([x3[g] for g in range(P_TOPK)], _SORT16)
    for shift in (4, 2, 1):
        xs = _merge_top16(xs, [pltpu.roll(x, shift, 0) for x in xs])
    return xs


def _theta_and_z(a, b):
    top = [a[0] + b[q] for q in range(P_TOPK)]
    for p in range(1, P_TOPK):
        top = _merge_top16(top, [a[p] + b[q] for q in range(_CAND_ROW_LEN[p])])
    z = jnp.ones_like(top[0])
    for r in range(1, P_TOPK):
        z = z + jnp.exp(top[r] - top[0])
    return top[P_TOPK - 1], z


def _peer_kernel(x_ref, xnt_ref, q_ref, gf_ref, sk_ref, u_ref, vt_ref, y_ref,
                 s1_s, s2_s, f2_s, th_s, a0_s, rz_s, acc_s, wg_s, stage_s):
    e = pl.program_id(1)
    tm = x_ref.shape[0]
    nsb = u_ref.shape[0] // P_NKEYS

    def prologue():
        for h in range(P_HEADS):
            for c, s_s in enumerate((s1_s, s2_s)):
                r0 = (2 * h + c) * P_NKEYS
                st = _dot_nt(sk_ref[2 * h + c], q_ref[:, r0:r0 + P_NKEYS])
                for tc in range(tm // LANES):
                    s_s[h, tc] = st[:, tc * LANES:(tc + 1) * LANES]
        sub = lax.broadcasted_iota(jnp.int32, (SUBLANES, LANES), 0)
        for tc in range(tm // LANES):
            ls = slice(tc * LANES, (tc + 1) * LANES)
            tops = []
            for s_s in (s1_s, s2_s):
                top = [jnp.zeros((SUBLANES, LANES), F32) for _ in range(P_TOPK)]
                for h in range(P_HEADS):
                    srt = _top16_sorted(s_s[h, tc])
                    top = [jnp.where(sub == h, srt[r], top[r]) for r in range(P_TOPK)]
                tops.append(top)
            theta, z = _theta_and_z(tops[0], tops[1])
            th_s[:, ls] = theta
            a0_s[:, ls] = tops[0][0]
            rz_s[:, ls] = 0.5 / z
            b0 = tops[1][0]
            for h in range(P_HEADS):
                f2_s[h, tc] = jnp.exp(s2_s[h, tc] - b0[h:h + 1, :])
        acc_s[...] = jnp.zeros_like(acc_s)

    def gate_times_gelu(at, sb):
        i = e * nsb + sb
        rows = slice(sb * P_NKEYS, (sb + 1) * P_NKEYS)
        for tc in range(tm // LANES):
            ls = slice(tc * LANES, (tc + 1) * LANES)
            w = jnp.zeros((P_NKEYS, LANES), F32)
            for h in range(P_HEADS):
                s1row = s1_s[h, tc, pl.ds(i, 1), :]
                f1row = jnp.exp(s1row - a0_s[h:h + 1, ls]) * rz_s[h:h + 1, ls]
                hit = (s2_s[h, tc] + s1row) >= th_s[h:h + 1, ls]
                w = w + jnp.where(hit, f2_s[h, tc], 0.0) * f1row
            a = at[rows, ls]
            inner = a * (0.7978845608028654 + 0.035677408136300125 * (a * a))
            stage_s[:, ls] = w * (a + a * jnp.tanh(inner))
        wg_s[rows, :] = stage_s[...].astype(BF16)

    @pl.when(e == 0)
    def _():
        prologue()

    at = _dot(u_ref[...], xnt_ref[...])
    for sb in range(nsb):
        gate_times_gelu(at, sb)
    acc_s[...] += _dot(vt_ref[...], wg_s[...])

    @pl.when(e == pl.num_programs(1) - 1)
    def _():
        y = x_ref[...] + jnp.transpose(acc_s[...])
        y_ref[...] = _rms(y, gf_ref[...])


def _peer(x1, xnt, q, gf, sk, u, vt, tm):
    n = x1.shape[0]
    te = PEER_TE
    chunked = (P_HEADS, tm // LANES, P_NKEYS, LANES)
    once = pl.Buffered(1)
    return pl.pallas_call(
        _peer_kernel,
        grid=(n // tm, P_NEXP // te),
        in_specs=[
            pl.BlockSpec((tm, D_MODEL), lambda i, e: (i, 0), pipeline_mode=once),
            pl.BlockSpec((D_MODEL, tm), lambda i, e: (0, i), pipeline_mode=once),
            pl.BlockSpec((tm, q.shape[1]), lambda i, e: (i, 0), pipeline_mode=once),
            _const_spec((1, D_MODEL)), _const_spec(sk.shape),
            pl.BlockSpec((te, D_MODEL), lambda i, e: (e, 0)),
            pl.BlockSpec((D_MODEL, te), lambda i, e: (0, e)),
        ],
        out_specs=pl.BlockSpec((tm, D_MODEL), lambda i, e: (i, 0)),
        out_shape=jax.ShapeDtypeStruct((n, D_MODEL), F32),
        scratch_shapes=[
            pltpu.VMEM(chunked, F32),
            pltpu.VMEM(chunked, F32),
            pltpu.VMEM(chunked, F32),
            pltpu.VMEM((SUBLANES, tm), F32),
            pltpu.VMEM((SUBLANES, tm), F32),
            pltpu.VMEM((SUBLANES, tm), F32),
            pltpu.VMEM((D_MODEL, tm), F32),
            pltpu.VMEM((te, tm), BF16),
            pltpu.VMEM((P_NKEYS, tm), F32),
        ],
        compiler_params=_params(("parallel", "arbitrary")),
        name="peer",
    )(x1, xnt, q, gf, sk, u, vt)


def _win_split_kernel(w_ref, wb_ref, wf_ref):
    a0 = 4 * M_W + 2 * M_HEADS
    qx0 = a0 + 2 * C_W
    wb_ref[:, 0:3 * M_W] = w_ref[:, 0:3 * M_W].astype(BF16)
    wb_ref[:, 3 * M_W:] = w_ref[:, qx0:qx0 + X_W].astype(BF16)
    wf_ref[:, 0:M_W] = w_ref[:, 3 * M_W:4 * M_W].astype(BF16)
    wf_ref[:, M_W:] = w_ref[:, a0:a0 + 2 * C_W].astype(BF16)


def _win_split(w):
    r, c = w.shape
    tr = 256
    return pl.pallas_call(
        _win_split_kernel,
        grid=(r // tr,),
        in_specs=[pl.BlockSpec((tr, c), lambda i: (i, 0))],
        out_specs=[pl.BlockSpec((tr, 3 * M_W + X_W), lambda i: (i, 0)),
                   pl.BlockSpec((tr, M_W + 2 * C_W), lambda i: (i, 0))],
        out_shape=[jax.ShapeDtypeStruct((r, 3 * M_W + X_W), BF16),
                   jax.ShapeDtypeStruct((r, M_W + 2 * C_W), BF16)],
        compiler_params=_params(("parallel",)),
        name="win_split",
    )(w)


def _transpose_cast_kernel(x_ref, o_ref):
    o_ref[...] = jnp.transpose(x_ref[...]).astype(BF16)


def _transpose_cast(x):
    r, c = x.shape
    tr = PEER_TE
    return pl.pallas_call(
        _transpose_cast_kernel,
        grid=(r // tr,),
        in_specs=[pl.BlockSpec((tr, c), lambda i: (i, 0))],
        out_specs=pl.BlockSpec((c, tr), lambda i: (0, i)),
        out_shape=jax.ShapeDtypeStruct((c, r), BF16),
        compiler_params=_params(("parallel",)),
        name="transpose_cast",
    )(x)


def kernel(x_prompt, x_sample, mem_prompt, cache_mem_k, cache_mem_v, state_C, state_n, state_m, state_conv,
           norm1_g, w_in, b_gate, mlstm_norm_g, conv_w, conv_b, conv_ln_g, conv_ln_b, mem_norm_g, w_mk, w_mv,
           w_out, norm2_g, peer_wq, peer_subkeys, peer_u, peer_v, final_g):
    depth = w_in.shape[0]
    assert depth == 1
    bp, seq_p, _ = x_prompt.shape
    bs, seq_s, _ = x_sample.shape
    l = 0

    w = w_in[l]
    g0 = 4 * M_W
    wb, wf = _win_split(w)
    wgate = w[:, g0:g0 + 2 * M_HEADS]
    wg = jnp.pad(wgate.astype(BF16), ((0, 0), (0, LANES - 2 * M_HEADS)))
    wgt = jnp.pad(wgate.T.astype(BF16), ((0, 2 * SUBLANES - 2 * M_HEADS), (0, 0)))
    bgr = jnp.pad(b_gate[l][None, :], ((0, 0), (0, LANES - 2 * M_HEADS)))
    bgc = jnp.pad(b_gate[l][:, None], ((0, 2 * SUBLANES - 2 * M_HEADS), (0, 0)))
    g1 = norm1_g[l][None, :]
    gnorm = mlstm_norm_g[l].reshape(1, M_W)
    cw = jnp.pad(conv_w[l], ((0, 1), (0, 0)))
    cb, clg, clb = conv_b[l][None, :], conv_ln_g[l][None, :], conv_ln_b[l][None, :]
    wkv = jnp.concatenate([w_mk[l], w_mv[l]], axis=1).astype(BF16)
    wo = w_out[l].astype(BF16)
    wq = peer_wq[l].astype(BF16)
    sk = peer_subkeys[l].reshape(2 * P_HEADS, P_NKEYS, P_NKEYS).astype(BF16)
    u = peer_u[l].astype(BF16)
    vt = _transpose_cast(peer_v[l])
    g2 = norm2_g[l][None, :]
    gf = final_g[None, :]

    xp = x_prompt.reshape(bp * seq_p, D_MODEL)
    xs = x_sample.reshape(bs * seq_s, D_MODEL)

    pb, pf, gcol, grow = _inproj(xp, g1, wb, wf, wg, wgt, bgr, bgc)
    hm, c_p, n_p, m_p = _mlstm_prompt(pb, pf, gcol, grow, gnorm, bp, seq_p)
    yc, conv_p = _conv_prompt(pf, cw, cb, clg, clb, bp, seq_p)
    mk, mv = _memkv(mem_prompt.reshape(bp * N_MEM, D_MODEL), mem_norm_g[l][None, :], wkv)
    ox = _xattn_prompt(pb, mk, mv, bp, seq_p)
    x1, xnt, q = _outproj(xp, hm, yc, ox, wo, g2, wq)
    y_p = _peer(x1, xnt, q, gf, sk, u, vt, TOK_TILE)

    pb, pf, gcol, _ = _inproj(xs, g1, wb, wf, wg, wgt, bgr, bgc)
    mtok = jnp.pad(jnp.repeat(state_m[l], seq_s, axis=0), ((0, 0), (0, LANES - M_HEADS)))
    hm, c_s, n_s, mtok_new = _mlstm_sample(pb, pf, gcol, mtok, gnorm, state_C[l], state_n[l], seq_s)
    yc, conv_s = _conv_sample(pf, state_conv[l], cw, cb, clg, clb, seq_s)
    ox = _xattn_sample(pb, cache_mem_k.reshape(bs, N_MEM * X_HEADS, X_DH),
                       cache_mem_v.reshape(bs, N_MEM * X_HEADS, X_DH), seq_s)
    x1, xnt, q = _outproj(xs, hm, yc, ox, wo, g2, wq)
    y_s = _peer(x1, xnt, q, gf, sk, u, vt, TOK_TILE)

    m_s = mtok_new.reshape(bs, seq_s, LANES)[:, seq_s - 1, :M_HEADS]
    return (y_p.reshape(bp, seq_p, D_MODEL),
            y_s.reshape(bs, seq_s, D_MODEL),
            mk.reshape(1, bp, N_MEM, X_HEADS, X_DH),
            mv.reshape(1, bp, N_MEM, X_HEADS, X_DH),
            c_p[None], n_p[None], m_p[None, :, :M_HEADS, 0], conv_p[None],
            c_s[None], n_s[None], m_s[None], conv_s[None])
```

```python
import functools

import jax
import jax.numpy as jnp
from jax import lax
from jax.experimental import pallas as pl
from jax.experimental.pallas import tpu as pltpu

F32 = jnp.float32
BF16 = jnp.bfloat16

D_MODEL = 2048
M_HEADS = 4
M_DH = 256
M_W = M_HEADS * M_DH
C_W = 512
CONV_K = 31
CONV_PAST = CONV_K - 1
X_HEADS = 4
X_DH = 128
X_W = X_HEADS * X_DH
N_MEM = 256
P_HEADS = 8
P_NKEYS = 128
P_NEXP = P_NKEYS * P_NKEYS
P_TOPK = 16
EPS = 1e-6

SUBLANES = 8
LANES = 128
VMEM_LIMIT_BYTES = 58 * 1024 * 1024

MLSTM_L = 256
GROUP_B = 4
TOK_TILE = 512
PEER_TE = 1024
CONV_T = 256
CONV_OFF = 32


def _dot(a, b):
    return jnp.dot(a, b, preferred_element_type=F32)


def _dot_nt(a, b):
    return lax.dot_general(a, b, (((1,), (1,)), ((), ())), preferred_element_type=F32)


def _dot_tn(a, b):
    return lax.dot_general(a, b, (((0,), (0,)), ((), ())), preferred_element_type=F32)


def _dot_exact(a, b):
    return jnp.dot(a, b, preferred_element_type=F32, precision=lax.Precision.HIGHEST)


def _rms(x, g):
    return x * lax.rsqrt(jnp.mean(x * x, axis=-1, keepdims=True) + EPS) * g


def _sigmoid(x):
    return 1.0 / (1.0 + jnp.exp(-x))


def _log_sigmoid(x):
    return jnp.minimum(x, 0.0) - jnp.log1p(jnp.exp(-jnp.abs(x)))


def _params(sem):
    return pltpu.CompilerParams(dimension_semantics=sem, vmem_limit_bytes=VMEM_LIMIT_BYTES)


def _const_spec(shape):
    nd = len(shape)
    return pl.BlockSpec(shape, lambda *_: (0,) * nd, pipeline_mode=pl.Buffered(1))


def _inproj_kernel(x_ref, g_ref, wb_ref, wf_ref, wg_ref, wgt_ref, bgr_ref, bgc_ref,
                   pb_ref, pf_ref, gcol_ref, grow_ref):
    xn = _rms(x_ref[...], g_ref[...]).astype(BF16)
    for c in range(0, wb_ref.shape[1], 512):
        p = _dot(xn, wb_ref[:, c:c + 512])
        if M_W <= c < 2 * M_W:
            p = p * (M_DH ** -0.5)
        pb_ref[:, c:c + 512] = p.astype(BF16)
    for c in range(0, wf_ref.shape[1], 512):
        pf_ref[:, c:c + 512] = _dot(xn, wf_ref[:, c:c + 512])
    gc = _dot(xn, wg_ref[...]) + bgr_ref[...]
    lane = lax.broadcasted_iota(jnp.int32, gc.shape, 1)
    gcol_ref[...] = jnp.where(lane >= M_HEADS, _log_sigmoid(gc), gc)
    gr = _dot_nt(wgt_ref[...], xn) + bgc_ref[...]
    row = lax.broadcasted_iota(jnp.int32, gr.shape, 0)
    grow_ref[...] = jnp.where(row >= M_HEADS, _log_sigmoid(gr), gr)[0:SUBLANES, :]


def _inproj(x, g, wb, wf, wg, wgt, bgr, bgc):
    n = x.shape[0]
    tm = TOK_TILE
    return pl.pallas_call(
        _inproj_kernel,
        grid=(n // tm,),
        in_specs=[
            pl.BlockSpec((tm, D_MODEL), lambda i: (i, 0)),
            _const_spec((1, D_MODEL)),
            _const_spec(wb.shape), _const_spec(wf.shape), _const_spec(wg.shape),
            _const_spec(wgt.shape), _const_spec(bgr.shape), _const_spec(bgc.shape),
        ],
        out_specs=[
            pl.BlockSpec((tm, wb.shape[1]), lambda i: (i, 0)),
            pl.BlockSpec((tm, wf.shape[1]), lambda i: (i, 0)),
            pl.BlockSpec((tm, LANES), lambda i: (i, 0)),
            pl.BlockSpec((SUBLANES, tm), lambda i: (0, i)),
        ],
        out_shape=[
            jax.ShapeDtypeStruct((n, wb.shape[1]), BF16),
            jax.ShapeDtypeStruct((n, wf.shape[1]), F32),
            jax.ShapeDtypeStruct((n, LANES), F32),
            jax.ShapeDtypeStruct((SUBLANES, n), F32),
        ],
        compiler_params=_params(("parallel",)),
        name="inproj",
    )(x, g, wb, wf, wg, wgt, bgr, bgc)


def _mlstm_head_out(hh, gnorm, o):
    hn = hh * lax.rsqrt(jnp.mean(hh * hh, axis=-1, keepdims=True) + EPS) * gnorm
    return (hn * _sigmoid(o)).astype(BF16)


def _mlstm_prompt_kernel(q_ref, k_ref, v_ref, o_ref, gcol_ref, grow_ref, gn_ref,
                         hm_ref, c_out_ref, n_out_ref, m_out_ref, c_s, n_s, m_s):
    c = pl.program_id(1)
    L = MLSTM_L

    @pl.when(c == 0)
    def _():
        c_s[...] = jnp.zeros_like(c_s)
        n_s[...] = jnp.zeros_like(n_s)
        m_s[...] = jnp.zeros_like(m_s)

    row = lax.broadcasted_iota(jnp.int32, (L, L), 0)
    col = lax.broadcasted_iota(jnp.int32, (L, L), 1)
    causal = col <= row
    tril = jnp.where(causal, 1.0, 0.0).astype(F32)
    triu = jnp.where(row <= col, 1.0, 0.0).astype(F32)
    gcol = gcol_ref[...]
    grow = grow_ref[...]
    bc_all = _dot_exact(tril, gcol)
    br_all = _dot_exact(grow, triu)

    heads = range(M_HEADS)
    sl = [slice(h * M_DH, (h + 1) * M_DH) for h in heads]
    q = [q_ref[:, sl[h]] for h in heads]
    k = [k_ref[:, sl[h]] for h in heads]
    v = [v_ref[:, sl[h]] for h in heads]
    li_r = [grow[h:h + 1, :] for h in heads]
    li_c = [gcol[:, h:h + 1] for h in heads]
    b_c = [bc_all[:, M_HEADS + h:M_HEADS + h + 1] for h in heads]
    b_r = [br_all[M_HEADS + h:M_HEADS + h + 1, :] for h in heads]
    m_prev = [m_s[h:h + 1, 0:1] for h in heads]
    cmat = [c_s[h] for h in heads]
    nrow = [n_s[h:h + 1, :] for h in heads]
    qk = [_dot_nt(q[h], k[h]) for h in heads]
    qc = [_dot_nt(q[h], cmat[h].astype(BF16)) for h in heads]
    log_d = [jnp.where(causal, b_c[h] - b_r[h] + li_r[h], -jnp.inf) for h in heads]
    log_inter = [b_c[h] + m_prev[h] for h in heads]
    m_t = [jnp.maximum(log_inter[h], jnp.max(log_d[h], axis=1, keepdims=True)) for h in heads]
    dm = [jnp.exp(log_d[h] - m_t[h]) for h in heads]
    inter = [jnp.exp(log_inter[h] - m_t[h]) for h in heads]
    s = [qk[h] * dm[h] for h in heads]
    num = [_dot(s[h].astype(BF16), v[h]) + inter[h] * qc[h] for h in heads]
    qn = [jnp.sum(q[h].astype(F32) * nrow[h], axis=1, keepdims=True) for h in heads]
    den = [jnp.sum(s[h], axis=1, keepdims=True) + inter[h] * qn[h] for h in heads]
    hh = [num[h] / jnp.maximum(jnp.abs(den[h]), jnp.exp(-m_t[h])) for h in heads]
    for h in heads:
        hm_ref[:, sl[h]] = _mlstm_head_out(hh[h], gn_ref[:, sl[h]], o_ref[:, sl[h]])

    m_new = [m_t[h][L - 1:L, :] for h in heads]
    b_last = [b_c[h][L - 1:L, :] for h in heads]
    w_r = [jnp.exp(b_last[h] - b_r[h] + li_r[h] - m_new[h]) for h in heads]
    w_c = [jnp.exp(b_last[h] - b_c[h] + li_c[h] - m_new[h]) for h in heads]
    decay = [jnp.exp(b_last[h] + m_prev[h] - m_new[h]) for h in heads]
    vw = [(v[h].astype(F32) * w_c[h]).astype(BF16) for h in heads]
    upd = [_dot_tn(vw[h], k[h]) for h in heads]
    wk = [_dot(jnp.broadcast_to(w_r[h], (SUBLANES, L)).astype(BF16), k[h]) for h in heads]
    for h in heads:
        c_s[h] = decay[h] * cmat[h] + upd[h]
        n_s[h:h + 1, :] = decay[h] * nrow[h] + wk[h][0:1, :]
        m_s[h:h + 1, :] = jnp.broadcast_to(m_new[h], (1, LANES))

    @pl.when(c == pl.num_programs(1) - 1)
    def _():
        c_out_ref[0] = c_s[...]
        n_out_ref[0] = n_s[0:M_HEADS, :]
        m_out_ref[0] = m_s[...]


def _mlstm_prompt(pb, pf, gcol, grow, gnorm, batch, seq):
    L = MLSTM_L
    nc = seq // L
    n = batch * seq
    rows = lambda b, c: b * nc + c
    return pl.pallas_call(
        _mlstm_prompt_kernel,
        grid=(batch, nc),
        in_specs=[
            pl.BlockSpec((L, M_W), lambda b, c: (rows(b, c), 0)),
            pl.BlockSpec((L, M_W), lambda b, c: (rows(b, c), 1)),
            pl.BlockSpec((L, M_W), lambda b, c: (rows(b, c), 2)),
            pl.BlockSpec((L, M_W), lambda b, c: (rows(b, c), 0)),
            pl.BlockSpec((L, LANES), lambda b, c: (rows(b, c), 0)),
            pl.BlockSpec((SUBLANES, L), lambda b, c: (0, rows(b, c))),
            _const_spec((1, M_W)),
        ],
        out_specs=[
            pl.BlockSpec((L, M_W), lambda b, c: (rows(b, c), 0)),
            pl.BlockSpec((1, M_HEADS, M_DH, M_DH), lambda b, c: (b, 0, 0, 0)),
            pl.BlockSpec((1, M_HEADS, M_DH), lambda b, c: (b, 0, 0)),
            pl.BlockSpec((1, SUBLANES, LANES), lambda b, c: (b, 0, 0)),
        ],
        out_shape=[
            jax.ShapeDtypeStruct((n, M_W), BF16),
            jax.ShapeDtypeStruct((batch, M_HEADS, M_DH, M_DH), F32),
            jax.ShapeDtypeStruct((batch, M_HEADS, M_DH), F32),
            jax.ShapeDtypeStruct((batch, SUBLANES, LANES), F32),
        ],
        scratch_shapes=[
            pltpu.VMEM((M_HEADS, M_DH, M_DH), F32),
            pltpu.VMEM((SUBLANES, M_DH), F32),
            pltpu.VMEM((SUBLANES, LANES), F32),
        ],
        compiler_params=_params(("parallel", "arbitrary")),
        name="mlstm_prompt",
    )(pb, pb, pb, pf, gcol, grow, gnorm)


def _seq_index(token, seq):
    assert seq & (seq - 1) == 0
    return token >> (seq.bit_length() - 1)


def _to_row(colvec, eye):
    return jnp.sum(jnp.where(eye, colvec, 0.0), axis=0, keepdims=True)


def _mlstm_sample_kernel(q_ref, k_ref, v_ref, o_ref, gcol_ref, mtok_ref, gn_ref, c_in_ref, n_in_ref,
                         hm_ref, c_out_ref, n_out_ref, mtok_out_ref, *, seq):
    R = GROUP_B * seq
    row = lax.broadcasted_iota(jnp.int32, (R, R), 0)
    col = lax.broadcasted_iota(jnp.int32, (R, R), 1)
    eye = row == col
    same = _seq_index(row, seq) == _seq_index(col, seq)
    causal = same & (col <= row)
    rblk = _seq_index(lax.broadcasted_iota(jnp.int32, (R, 1), 0), seq)
    lane = lax.broadcasted_iota(jnp.int32, (R, LANES), 1)
    gcol = gcol_ref[...]
    mtok = mtok_ref[...]
    m_out = jnp.zeros((R, LANES), F32)

    def per_block(vals):
        out = jnp.zeros((R, 1), F32)
        for j in range(GROUP_B):
            out = jnp.where(rblk == j, vals[j], out)
        return out

    heads = range(M_HEADS)
    group = range(GROUP_B)
    last = [j * seq + seq - 1 for j in group]
    sl = [slice(h * M_DH, (h + 1) * M_DH) for h in heads]
    q = [q_ref[:, sl[h]] for h in heads]
    k = [k_ref[:, sl[h]] for h in heads]
    v = [v_ref[:, sl[h]] for h in heads]
    qk = [_dot_nt(q[h], k[h]) for h in heads]
    qc_jh = [[_dot_nt(q[h], c_in_ref[j, h].astype(BF16)) for j in group] for h in heads]
    li_c = [gcol[:, h:h + 1] for h in heads]
    lf_c = [gcol[:, M_HEADS + h:M_HEADS + h + 1] for h in heads]
    li_r = [_to_row(li_c[h], eye) for h in heads]
    lf_r = [_to_row(lf_c[h], eye) for h in heads]
    m_prev = [mtok[:, h:h + 1] for h in heads]
    b_c = [jnp.sum(jnp.where(causal, lf_r[h], 0.0), axis=1, keepdims=True) for h in heads]
    b_r = [_to_row(b_c[h], eye) for h in heads]
    log_d = [jnp.where(causal, b_c[h] - b_r[h] + li_r[h], -jnp.inf) for h in heads]
    log_inter = [b_c[h] + m_prev[h] for h in heads]
    m_t = [jnp.maximum(log_inter[h], jnp.max(log_d[h], axis=1, keepdims=True)) for h in heads]
    dm = [jnp.exp(log_d[h] - m_t[h]) for h in heads]
    inter = [jnp.exp(log_inter[h] - m_t[h]) for h in heads]
    s = [qk[h] * dm[h] for h in heads]
    qc, ntok = [], []
    for h in heads:
        qc_h = jnp.zeros((R, M_DH), F32)
        ntok_h = jnp.zeros((R, M_DH), F32)
        for j in group:
            qc_h = jnp.where(rblk == j, qc_jh[h][j], qc_h)
            ntok_h = jnp.where(rblk == j, n_in_ref[j, h:h + 1, :], ntok_h)
        qc.append(qc_h)
        ntok.append(ntok_h)
    num = [_dot(s[h].astype(BF16), v[h]) + inter[h] * qc[h] for h in heads]
    qn = [jnp.sum(q[h].astype(F32) * ntok[h], axis=1, keepdims=True) for h in heads]
    den = [jnp.sum(s[h], axis=1, keepdims=True) + inter[h] * qn[h] for h in heads]
    hh = [num[h] / jnp.maximum(jnp.abs(den[h]), jnp.exp(-m_t[h])) for h in heads]
    for h in heads:
        hm_ref[:, sl[h]] = _mlstm_head_out(hh[h], gn_ref[:, sl[h]], o_ref[:, sl[h]])

    m_new = [per_block([m_t[h][r:r + 1, :] for r in last]) for h in heads]
    b_last = [per_block([b_c[h][r:r + 1, :] for r in last]) for h in heads]
    w_c = [jnp.exp(b_last[h] - b_c[h] + li_c[h] - m_new[h]) for h in heads]
    decay_c = [jnp.exp(b_last[h] + m_prev[h] - m_new[h]) for h in heads]
    vw = [v[h].astype(F32) * w_c[h] for h in heads]
    wk = [w_c[h] * k[h].astype(F32) for h in heads]
    upd = [[_dot_tn(jnp.where(rblk == j, vw[h], 0.0).astype(BF16), k[h]) for j in group] for h in heads]
    for h in heads:
        for j in group:
            decay = decay_c[h][last[j]:last[j] + 1, :]
            c_out_ref[j, h] = decay * c_in_ref[j, h] + upd[h][j]
            n_out_ref[j, h:h + 1, :] = (decay * n_in_ref[j, h:h + 1, :]
                                        + jnp.sum(jnp.where(rblk == j, wk[h], 0.0), axis=0, keepdims=True))
        m_out = jnp.where(lane == h, m_new[h], m_out)
    mtok_out_ref[...] = m_out


def _mlstm_sample(pb, pf, gcol, mtok, gnorm, state_c, state_n, seq):
    batch = state_c.shape[0]
    R = GROUP_B * seq
    n = batch * seq
    return pl.pallas_call(
        functools.partial(_mlstm_sample_kernel, seq=seq),
        grid=(batch // GROUP_B,),
        in_specs=[
            pl.BlockSpec((R, M_W), lambda i: (i, 0)),
            pl.BlockSpec((R, M_W), lambda i: (i, 1)),
            pl.BlockSpec((R, M_W), lambda i: (i, 2)),
            pl.BlockSpec((R, M_W), lambda i: (i, 0)),
            pl.BlockSpec((R, LANES), lambda i: (i, 0)),
            pl.BlockSpec((R, LANES), lambda i: (i, 0)),
            _const_spec((1, M_W)),
            pl.BlockSpec((GROUP_B, M_HEADS, M_DH, M_DH), lambda i: (i, 0, 0, 0)),
            pl.BlockSpec((GROUP_B, M_HEADS, M_DH), lambda i: (i, 0, 0)),
        ],
        out_specs=[
            pl.BlockSpec((R, M_W), lambda i: (i, 0)),
            pl.BlockSpec((GROUP_B, M_HEADS, M_DH, M_DH), lambda i: (i, 0, 0, 0)),
            pl.BlockSpec((GROUP_B, M_HEADS, M_DH), lambda i: (i, 0, 0)),
            pl.BlockSpec((R, LANES), lambda i: (i, 0)),
        ],
        out_shape=[
            jax.ShapeDtypeStruct((n, M_W), BF16),
            jax.ShapeDtypeStruct(state_c.shape, F32),
            jax.ShapeDtypeStruct(state_n.shape, F32),
            jax.ShapeDtypeStruct((n, LANES), F32),
        ],
        compiler_params=_params(("parallel",)),
        name="mlstm_sample",
    )(pb, pb, pb, pf, gcol, mtok, gnorm, state_c, state_n)


def _conv_out(y, lg, lb):
    mu = jnp.mean(y, axis=-1, keepdims=True)
    yc = y - mu
    var = jnp.mean(yc * yc, axis=-1, keepdims=True)
    z = yc * lax.rsqrt(var + EPS) * lg + lb
    return (z * _sigmoid(z)).astype(BF16)


def _conv_prompt_kernel(a_ref, g_ref, w_ref, b_ref, lg_ref, lb_ref, yc_ref, new_ref, xp_s, xsh_s):
    t = pl.program_id(1)
    T = CONV_T

    @pl.when(t == 0)
    def _():
        xp_s[0:CONV_OFF, :] = jnp.zeros((CONV_OFF, C_W), F32)

    xp_s[CONV_OFF:CONV_OFF + T, :] = a_ref[...] * _sigmoid(g_ref[...])
    base = CONV_OFF - CONV_PAST
    acc = jnp.zeros((T, C_W), F32) + b_ref[...]
    for b in range(SUBLANES):
        taps = [j for j in range(b, CONV_K, SUBLANES)]
        span = taps[-1] - b + T
        xsh_s[b, 0:span, :] = xp_s[base + b:base + b + span, :]
        for j in taps:
            acc = acc + w_ref[j:j + 1, :] * xsh_s[b, j - b:j - b + T, :]
    yc_ref[...] = _conv_out(acc, lg_ref[...], lb_ref[...])
    tail = xp_s[CONV_OFF + T - CONV_PAST:CONV_OFF + T, :]
    xp_s[base:CONV_OFF, :] = tail

    @pl.when(t == pl.num_programs(1) - 1)
    def _():
        new_ref[0] = tail


def _conv_prompt(pf, w, b, lg, lb, batch, seq):
    T = CONV_T
    nt = seq // T
    n = batch * seq
    return pl.pallas_call(
        _conv_prompt_kernel,
        grid=(batch, nt),
        in_specs=[
            pl.BlockSpec((T, C_W), lambda bi, t: (bi * nt + t, 2)),
            pl.BlockSpec((T, C_W), lambda bi, t: (bi * nt + t, 3)),
            _const_spec(w.shape), _const_spec((1, C_W)), _const_spec((1, C_W)), _const_spec((1, C_W)),
        ],
        out_specs=[
            pl.BlockSpec((T, C_W), lambda bi, t: (bi * nt + t, 0)),
            pl.BlockSpec((1, CONV_PAST, C_W), lambda bi, t: (bi, 0, 0)),
        ],
        out_shape=[
            jax.ShapeDtypeStruct((n, C_W), BF16),
            jax.ShapeDtypeStruct((batch, CONV_PAST, C_W), F32),
        ],
        scratch_shapes=[pltpu.VMEM((CONV_OFF + T, C_W), F32),
                        pltpu.VMEM((SUBLANES, CONV_OFF + T, C_W), F32)],
        compiler_params=_params(("parallel", "arbitrary")),
        name="conv_prompt",
    )(pf, pf, w, b, lg, lb)


def _conv_sample_kernel(a_ref, g_ref, past_ref, w_ref, b_ref, lg_ref, lb_ref, yc_ref, new_ref,
                        xp_s, y_s, *, seq, gb):
    u = a_ref[...] * _sigmoid(g_ref[...])
    for j in range(gb):
        xp_s[0:CONV_PAST, :] = past_ref[j]
        xp_s[CONV_PAST:CONV_PAST + seq, :] = u[j * seq:(j + 1) * seq, :]
        acc = jnp.zeros((seq, C_W), F32) + b_ref[...]
        for i in range(CONV_K):
            acc = acc + w_ref[i:i + 1, :] * xp_s[i:i + seq, :]
        y_s[j * seq:(j + 1) * seq, :] = acc
        new_ref[j] = xp_s[seq:seq + CONV_PAST, :]
    yc_ref[...] = _conv_out(y_s[...], lg_ref[...], lb_ref[...])


def _conv_sample(pf, past, w, b, lg, lb, seq):
    batch = past.shape[0]
    gb = 8
    R = gb * seq
    n = batch * seq
    return pl.pallas_call(
        functools.partial(_conv_sample_kernel, seq=seq, gb=gb),
        grid=(batch // gb,),
        in_specs=[
            pl.BlockSpec((R, C_W), lambda i: (i, 2)),
            pl.BlockSpec((R, C_W), lambda i: (i, 3)),
            pl.BlockSpec((gb, CONV_PAST, C_W), lambda i: (i, 0, 0)),
            _const_spec(w.shape), _const_spec((1, C_W)), _const_spec((1, C_W)), _const_spec((1, C_W)),
        ],
        out_specs=[
            pl.BlockSpec((R, C_W), lambda i: (i, 0)),
            pl.BlockSpec((gb, CONV_PAST, C_W), lambda i: (i, 0, 0)),
        ],
        out_shape=[
            jax.ShapeDtypeStruct((n, C_W), BF16),
            jax.ShapeDtypeStruct(past.shape, F32),
        ],
        scratch_shapes=[pltpu.VMEM((CONV_PAST + seq + 6, C_W), F32), pltpu.VMEM((R, C_W), F32)],
        compiler_params=_params(("parallel",)),
        name="conv_sample",
    )(pf, pf, past, w, b, lg, lb)


def _memkv_kernel(x_ref, g_ref, w_ref, k_ref, v_ref):
    xn = _rms(x_ref[...], g_ref[...]).astype(BF16)
    k_ref[...] = _dot(xn, w_ref[:, 0:X_W])
    v_ref[...] = _dot(xn, w_ref[:, X_W:2 * X_W])


def _memkv(mem, g, w):
    n = mem.shape[0]
    tm = TOK_TILE
    return pl.pallas_call(
        _memkv_kernel,
        grid=(n // tm,),
        in_specs=[pl.BlockSpec((tm, D_MODEL), lambda i: (i, 0)), _const_spec((1, D_MODEL)),
                  _const_spec(w.shape)],
        out_specs=[pl.BlockSpec((tm, X_W), lambda i: (i, 0)), pl.BlockSpec((tm, X_W), lambda i: (i, 0))],
        out_shape=[jax.ShapeDtypeStruct((n, X_W), F32), jax.ShapeDtypeStruct((n, X_W), F32)],
        compiler_params=_params(("parallel",)),
        name="memkv",
    )(mem, g, w)


def _xattn_prompt_kernel(q_ref, k_ref, v_ref, o_ref):
    heads = range(X_HEADS)
    sl = [slice(h * X_DH, (h + 1) * X_DH) for h in heads]
    s = [_dot_nt(q_ref[:, sl[h]], k_ref[:, sl[h]].astype(BF16)) * (X_DH ** -0.5) for h in heads]
    p = [jnp.exp(s[h] - jnp.max(s[h], axis=-1, keepdims=True)) for h in heads]
    p = [p[h] / jnp.sum(p[h], axis=-1, keepdims=True) for h in heads]
    for h in heads:
        o_ref[:, sl[h]] = _dot(p[h].astype(BF16), v_ref[:, sl[h]].astype(BF16)).astype(BF16)


def _xattn_prompt(pb, k, v, batch, seq):
    tq = TOK_TILE
    nt = seq // tq
    qblk = 3 * M_W // X_W
    return pl.pallas_call(
        _xattn_prompt_kernel,
        grid=(batch, nt),
        in_specs=[
            pl.BlockSpec((tq, X_W), lambda b, t: (b * nt + t, qblk)),
            pl.BlockSpec((N_MEM, X_W), lambda b, t: (b, 0)),
            pl.BlockSpec((N_MEM, X_W), lambda b, t: (b, 0)),
        ],
        out_specs=pl.BlockSpec((tq, X_W), lambda b, t: (b * nt + t, 0)),
        out_shape=jax.ShapeDtypeStruct((batch * seq, X_W), BF16),
        compiler_params=_params(("parallel", "parallel")),
        name="xattn_prompt",
    )(pb, k, v)


def _xattn_sample_kernel(q_ref, k_ref, v_ref, o_ref, *, seq):
    R = GROUP_B * seq
    assert R == 16 and X_HEADS == 4
    rblk = _seq_index(lax.broadcasted_iota(jnp.int32, (R, 1), 0), seq)
    nk = N_MEM * X_HEADS
    row_head = lax.broadcasted_iota(jnp.int32, (X_HEADS * R, nk), 0) >> 4
    col_head = lax.broadcasted_iota(jnp.int32, (X_HEADS * R, nk), 1) & (X_HEADS - 1)
    own = row_head == col_head
    q4 = jnp.concatenate([q_ref[:, h * X_DH:(h + 1) * X_DH] for h in range(X_HEADS)], axis=0)
    outs = [jnp.zeros((R, X_DH), F32) for _ in range(X_HEADS)]
    group = range(GROUP_B)
    s = [_dot_nt(q4, k_ref[j].astype(BF16)) * (X_DH ** -0.5) for j in group]
    s = [jnp.where(own, s[j], -jnp.inf) for j in group]
    p = [jnp.exp(s[j] - jnp.max(s[j], axis=-1, keepdims=True)) for j in group]
    p = [p[j] / jnp.sum(p[j], axis=-1, keepdims=True) for j in group]
    o4s = [_dot(p[j].astype(BF16), v_ref[j].astype(BF16)) for j in group]
    for j in group:
        o4 = o4s[j]
        outs = [jnp.where(rblk == j, o4[h * R:(h + 1) * R, :], outs[h]) for h in range(X_HEADS)]
    for h in range(X_HEADS):
        o_ref[:, h * X_DH:(h + 1) * X_DH] = outs[h].astype(BF16)


def _xattn_sample(pb, k, v, seq):
    batch = k.shape[0]
    R = GROUP_B * seq
    qblk = 3 * M_W // X_W
    return pl.pallas_call(
        functools.partial(_xattn_sample_kernel, seq=seq),
        grid=(batch // GROUP_B,),
        in_specs=[
            pl.BlockSpec((R, X_W), lambda i: (i, qblk)),
            pl.BlockSpec((GROUP_B, N_MEM * X_HEADS, X_DH), lambda i: (i, 0, 0)),
            pl.BlockSpec((GROUP_B, N_MEM * X_HEADS, X_DH), lambda i: (i, 0, 0)),
        ],
        out_specs=pl.BlockSpec((R, X_W), lambda i: (i, 0)),
        out_shape=jax.ShapeDtypeStruct((batch * seq, X_W), BF16),
        compiler_params=_params(("parallel",)),
        name="xattn_sample",
    )(pb, k, v)


def _outproj_kernel(x_ref, hm_ref, yc_ref, ox_ref, w_ref, g2_ref, wq_ref, x1_ref, xnt_ref, q_ref):
    acc = x_ref[...]
    acc = acc + _dot(hm_ref[...], w_ref[0:M_W, :])
    acc = acc + _dot(yc_ref[...], w_ref[M_W:M_W + C_W, :])
    acc = acc + _dot(ox_ref[...], w_ref[M_W + C_W:, :])
    x1_ref[...] = acc
    xn = _rms(acc, g2_ref[...])
    xnt_ref[...] = jnp.transpose(xn).astype(BF16)
    q_ref[...] = _dot(xn.astype(BF16), wq_ref[...]).astype(BF16)


def _outproj(x, hm, yc, ox, w, g2, wq):
    n = x.shape[0]
    tm = TOK_TILE
    return pl.pallas_call(
        _outproj_kernel,
        grid=(n // tm,),
        in_specs=[
            pl.BlockSpec((tm, D_MODEL), lambda i: (i, 0)),
            pl.BlockSpec((tm, M_W), lambda i: (i, 0)),
            pl.BlockSpec((tm, C_W), lambda i: (i, 0)),
            pl.BlockSpec((tm, X_W), lambda i: (i, 0)),
            _const_spec(w.shape), _const_spec((1, D_MODEL)), _const_spec(wq.shape),
        ],
        out_specs=[
            pl.BlockSpec((tm, D_MODEL), lambda i: (i, 0)),
            pl.BlockSpec((D_MODEL, tm), lambda i: (0, i)),
            pl.BlockSpec((tm, wq.shape[1]), lambda i: (i, 0)),
        ],
        out_shape=[
            jax.ShapeDtypeStruct((n, D_MODEL), F32),
            jax.ShapeDtypeStruct((D_MODEL, n), BF16),
            jax.ShapeDtypeStruct((n, wq.shape[1]), BF16),
        ],
        compiler_params=_params(("parallel",)),
        name="outproj",
    )(x, hm, yc, ox, w, g2, wq)


def _sort16_pairs():
    n, pairs, p = P_TOPK, [], 1
    while p < n:
        k = p
        while k >= 1:
            for j in range(k % p, n - k, 2 * k):
                for i in range(min(k, n - j - k)):
                    if (i + j) // (2 * p) == (i + j + k) // (2 * p):
                        pairs.append((i + j, i + j + k))
            k //= 2
        p *= 2
    return pairs


_SORT16 = _sort16_pairs()
_BITONIC16 = [(i, i + d) for d in (8, 4, 2, 1) for i in range(P_TOPK) if not i & d]
_CAND_ROW_LEN = [P_TOPK // (p + 1) for p in range(P_TOPK)]


def _exchange(xs, pairs):
    for a, b in pairs:
        hi = jnp.maximum(xs[a], xs[b])
        lo = jnp.minimum(xs[a], xs[b])
        xs[a], xs[b] = hi, lo
    return xs


def _merge_top16(xs, ys):
    xs = list(xs)
    for r, y in enumerate(ys):
        xs[P_TOPK - 1 - r] = jnp.maximum(xs[P_TOPK - 1 - r], y)
    return _exchange(xs, _BITONIC16)


def _top16_sorted(st):
    t = st.shape[1]
    x3 = st.reshape(P_NKEYS // SUBLANES, SUBLANES, t)
    xs = _exchange([x3[g] for g in range(P_TOPK)], _SORT16)
    for shift in (4, 2, 1):
        xs = _merge_top16(xs, [pltpu.roll(x, shift, 0) for x in xs])
    return xs


def _theta_and_z(a, b):
    top = [a[0] + b[q] for q in range(P_TOPK)]
    for p in range(1, P_TOPK):
        top = _merge_top16(top, [a[p] + b[q] for q in range(_CAND_ROW_LEN[p])])
    z = jnp.ones_like(top[0])
    for r in range(1, P_TOPK):
        z = z + jnp.exp(top[r] - top[0])
    return top[P_TOPK - 1], z


def _peer_kernel(x_ref, xnt_ref, q_ref, gf_ref, sk_ref, u_ref, vt_ref, y_ref,
                 s1_s, s2_s, f2_s, th_s, a0_s, rz_s, acc_s, wg_s, stage_s):
    e = pl.program_id(1)
    tm = x_ref.shape[0]
    nsb = u_ref.shape[0] // P_NKEYS

    def prologue():
        for h in range(P_HEADS):
            for c, s_s in enumerate((s1_s, s2_s)):
                r0 = (2 * h + c) * P_NKEYS
                st = _dot_nt(sk_ref[2 * h + c], q_ref[:, r0:r0 + P_NKEYS])
                for tc in range(tm // LANES):
                    s_s[h, tc] = st[:, tc * LANES:(tc + 1) * LANES]
        sub = lax.broadcasted_iota(jnp.int32, (SUBLANES, LANES), 0)
        for tc in range(tm // LANES):
            ls = slice(tc * LANES, (tc + 1) * LANES)
            tops = []
            for s_s in (s1_s, s2_s):
                top = [jnp.zeros((SUBLANES, LANES), F32) for _ in range(P_TOPK)]
                for h in range(P_HEADS):
                    srt = _top16_sorted(s_s[h, tc])
                    top = [jnp.where(sub == h, srt[r], top[r]) for r in range(P_TOPK)]
                tops.append(top)
            theta, z = _theta_and_z(tops[0], tops[1])
            th_s[:, ls] = theta
            a0_s[:, ls] = tops[0][0]
            rz_s[:, ls] = 0.5 / z
            b0 = tops[1][0]
            for h in range(P_HEADS):
                f2_s[h, tc] = jnp.exp(s2_s[h, tc] - b0[h:h + 1, :])
        acc_s[...] = jnp.zeros_like(acc_s)

    def gate_times_gelu(at, sb):
        i = e * nsb + sb
        rows = slice(sb * P_NKEYS, (sb + 1) * P_NKEYS)
        for tc in range(tm // LANES):
            ls = slice(tc * LANES, (tc + 1) * LANES)
            w = jnp.zeros((P_NKEYS, LANES), F32)
            for h in range(P_HEADS):
                s1row = s1_s[h, tc, pl.ds(i, 1), :]
                f1row = jnp.exp(s1row - a0_s[h:h + 1, ls]) * rz_s[h:h + 1, ls]
                hit = (s2_s[h, tc] + s1row) >= th_s[h:h + 1, ls]
                w = w + jnp.where(hit, f2_s[h, tc], 0.0) * f1row
            a = at[rows, ls]
            inner = a * (0.7978845608028654 + 0.035677408136300125 * (a * a))
            stage_s[:, ls] = w * (a + a * jnp.tanh(inner))
        wg_s[rows, :] = stage_s[...].astype(BF16)

    @pl.when(e == 0)
    def _():
        prologue()

    at = _dot(u_ref[...], xnt_ref[...])
    for sb in range(nsb):
        gate_times_gelu(at, sb)
    acc_s[...] += _dot(vt_ref[...], wg_s[...])

    @pl.when(e == pl.num_programs(1) - 1)
    def _():
        y = x_ref[...] + jnp.transpose(acc_s[...])
        y_ref[...] = _rms(y, gf_ref[...])


def _peer(x1, xnt, q, gf, sk, u, vt, tm):
    n = x1.shape[0]
    te = PEER_TE
    chunked = (P_HEADS, tm // LANES, P_NKEYS, LANES)
    once = pl.Buffered(1)
    return pl.pallas_call(
        _peer_kernel,
        grid=(n // tm, P_NEXP // te),
        in_specs=[
            pl.BlockSpec((tm, D_MODEL), lambda i, e: (i, 0), pipeline_mode=once),
            pl.BlockSpec((D_MODEL, tm), lambda i, e: (0, i), pipeline_mode=once),
            pl.BlockSpec((tm, q.shape[1]), lambda i, e: (i, 0), pipeline_mode=once),
            _const_spec((1, D_MODEL)), _const_spec(sk.shape),
            pl.BlockSpec((te, D_MODEL), lambda i, e: (e, 0)),
            pl.BlockSpec((D_MODEL, te), lambda i, e: (0, e)),
        ],
        out_specs=pl.BlockSpec((tm, D_MODEL), lambda i, e: (i, 0)),
        out_shape=jax.ShapeDtypeStruct((n, D_MODEL), F32),
        scratch_shapes=[
            pltpu.VMEM(chunked, F32),
            pltpu.VMEM(chunked, F32),
            pltpu.VMEM(chunked, F32),
            pltpu.VMEM((SUBLANES, tm), F32),
            pltpu.VMEM((SUBLANES, tm), F32),
            pltpu.VMEM((SUBLANES, tm), F32),
            pltpu.VMEM((D_MODEL, tm), F32),
            pltpu.VMEM((te, tm), BF16),
            pltpu.VMEM((P_NKEYS, tm), F32),
        ],
        compiler_params=_params(("parallel", "arbitrary")),
        name="peer",
    )(x1, xnt, q, gf, sk, u, vt)


def _win_split_kernel(w_ref, wb_ref, wf_ref):
    a0 = 4 * M_W + 2 * M_HEADS
    qx0 = a0 + 2 * C_W
    wb_ref[:, 0:3 * M_W] = w_ref[:, 0:3 * M_W].astype(BF16)
    wb_ref[:, 3 * M_W:] = w_ref[:, qx0:qx0 + X_W].astype(BF16)
    wf_ref[:, 0:M_W] = w_ref[:, 3 * M_W:4 * M_W].astype(BF16)
    wf_ref[:, M_W:] = w_ref[:, a0:a0 + 2 * C_W].astype(BF16)


def _win_split(w_in, layer):
    _, r, c = w_in.shape
    tr = 256
    return pl.pallas_call(
        _win_split_kernel,
        grid=(r // tr,),
        in_specs=[pl.BlockSpec((None, tr, c), lambda i: (layer, i, 0))],
        out_specs=[pl.BlockSpec((tr, 3 * M_W + X_W), lambda i: (i, 0)),
                   pl.BlockSpec((tr, M_W + 2 * C_W), lambda i: (i, 0))],
        out_shape=[jax.ShapeDtypeStruct((r, 3 * M_W + X_W), BF16),
                   jax.ShapeDtypeStruct((r, M_W + 2 * C_W), BF16)],
        compiler_params=_params(("parallel",)),
        name="win_split",
    )(w_in)


def _transpose_cast_kernel(x_ref, o_ref):
    o_ref[...] = jnp.transpose(x_ref[...]).astype(BF16)


def _transpose_cast(x):
    r, c = x.shape
    tr = PEER_TE
    return pl.pallas_call(
        _transpose_cast_kernel,
        grid=(r // tr,),
        in_specs=[pl.BlockSpec((tr, c), lambda i: (i, 0))],
        out_specs=pl.BlockSpec((c, tr), lambda i: (0, i)),
        out_shape=jax.ShapeDtypeStruct((c, r), BF16),
        compiler_params=_params(("parallel",)),
        name="transpose_cast",
    )(x)


def kernel(x_prompt, x_sample, mem_prompt, cache_mem_k, cache_mem_v, state_C, state_n, state_m, state_conv,
           norm1_g, w_in, b_gate, mlstm_norm_g, conv_w, conv_b, conv_ln_g, conv_ln_b, mem_norm_g, w_mk, w_mv,
           w_out, norm2_g, peer_wq, peer_subkeys, peer_u, peer_v, final_g):
    depth = w_in.shape[0]
    assert depth == 1
    bp, seq_p, _ = x_prompt.shape
    bs, seq_s, _ = x_sample.shape
    l = 0

    w = w_in[l]
    g0 = 4 * M_W
    wb, wf = _win_split(w_in, l)
    wgate = w[:, g0:g0 + 2 * M_HEADS]
    wg = jnp.pad(wgate.astype(BF16), ((0, 0), (0, LANES - 2 * M_HEADS)))
    wgt = jnp.pad(wgate.T.astype(BF16), ((0, 2 * SUBLANES - 2 * M_HEADS), (0, 0)))
    bgr = jnp.pad(b_gate[l][None, :], ((0, 0), (0, LANES - 2 * M_HEADS)))
    bgc = jnp.pad(b_gate[l][:, None], ((0, 2 * SUBLANES - 2 * M_HEADS), (0, 0)))
    g1 = norm1_g[l][None, :]
    gnorm = mlstm_norm_g[l].reshape(1, M_W)
    cw = jnp.pad(conv_w[l], ((0, 1), (0, 0)))
    cb, clg, clb = conv_b[l][None, :], conv_ln_g[l][None, :], conv_ln_b[l][None, :]
    wkv = jnp.concatenate([w_mk[l], w_mv[l]], axis=1).astype(BF16)
    wo = w_out[l].astype(BF16)
    wq = peer_wq[l].astype(BF16)
    sk = peer_subkeys[l].reshape(2 * P_HEADS, P_NKEYS, P_NKEYS).astype(BF16)
    u = peer_u[l].astype(BF16)
    vt = _transpose_cast(peer_v[l])
    g2 = norm2_g[l][None, :]
    gf = final_g[None, :]

    xp = x_prompt.reshape(bp * seq_p, D_MODEL)
    xs = x_sample.reshape(bs * seq_s, D_MODEL)

    pb, pf, gcol, grow = _inproj(xp, g1, wb, wf, wg, wgt, bgr, bgc)
    hm, c_p, n_p, m_p = _mlstm_prompt(pb, pf, gcol, grow, gnorm, bp, seq_p)
    yc, conv_p = _conv_prompt(pf, cw, cb, clg, clb, bp, seq_p)
    mk, mv = _memkv(mem_prompt.reshape(bp * N_MEM, D_MODEL), mem_norm_g[l][None, :], wkv)
    ox = _xattn_prompt(pb, mk, mv, bp, seq_p)
    x1, xnt, q = _outproj(xp, hm, yc, ox, wo, g2, wq)
    y_p = _peer(x1, xnt, q, gf, sk, u, vt, TOK_TILE)

    pb, pf, gcol, _ = _inproj(xs, g1, wb, wf, wg, wgt, bgr, bgc)
    mtok = jnp.pad(jnp.repeat(state_m[l], seq_s, axis=0), ((0, 0), (0, LANES - M_HEADS)))
    hm, c_s, n_s, mtok_new = _mlstm_sample(pb, pf, gcol, mtok, gnorm, state_C[l], state_n[l], seq_s)
    yc, conv_s = _conv_sample(pf, state_conv[l], cw, cb, clg, clb, seq_s)
    ox = _xattn_sample(pb, cache_mem_k.reshape(bs, N_MEM * X_HEADS, X_DH),
                       cache_mem_v.reshape(bs, N_MEM * X_HEADS, X_DH), seq_s)
    x1, xnt, q = _outproj(xs, hm, yc, ox, wo, g2, wq)
    y_s = _peer(x1, xnt, q, gf, sk, u, vt, TOK_TILE)

    m_s = mtok_new.reshape(bs, seq_s, LANES)[:, seq_s - 1, :M_HEADS]
    return (y_p.reshape(bp, seq_p, D_MODEL),
            y_s.reshape(bs, seq_s, D_MODEL),
            mk.reshape(1, bp, N_MEM, X_HEADS, X_DH),
            mv.reshape(1, bp, N_MEM, X_HEADS, X_DH),
            c_p[None], n_p[None], m_p[None, :, :M_HEADS, 0], conv_p[None],
            c_s[None], n_s[None], m_s[None], conv_s[None])
```

```python
import functools

import jax
import jax.numpy as jnp
from jax import lax
from jax.experimental import pallas as pl
from jax.experimental.pallas import tpu as pltpu

F32 = jnp.float32
BF16 = jnp.bfloat16

D_MODEL = 2048
M_HEADS = 4
M_DH = 256
M_W = M_HEADS * M_DH
C_W = 512
CONV_K = 31
CONV_PAST = CONV_K - 1
X_HEADS = 4
X_DH = 128
X_W = X_HEADS * X_DH
N_MEM = 256
P_HEADS = 8
P_NKEYS = 128
P_NEXP = P_NKEYS * P_NKEYS
P_TOPK = 16
EPS = 1e-6

SUBLANES = 8
LANES = 128
VMEM_LIMIT_BYTES = 58 * 1024 * 1024

MLSTM_L = 256
GROUP_B = 4
TOK_TILE = 512
PEER_TE = 1024
CONV_T = 256
CONV_OFF = 32


def _dot(a, b):
    return jnp.dot(a, b, preferred_element_type=F32)


def _dot_nt(a, b):
    return lax.dot_general(a, b, (((1,), (1,)), ((), ())), preferred_element_type=F32)


def _dot_tn(a, b):
    return lax.dot_general(a, b, (((0,), (0,)), ((), ())), preferred_element_type=F32)


def _dot_exact(a, b):
    return jnp.dot(a, b, preferred_element_type=F32, precision=lax.Precision.HIGHEST)


def _rms(x, g):
    return x * lax.rsqrt(jnp.mean(x * x, axis=-1, keepdims=True) + EPS) * g


def _sigmoid(x):
    return 1.0 / (1.0 + jnp.exp(-x))


def _log_sigmoid(x):
    return jnp.minimum(x, 0.0) - jnp.log1p(jnp.exp(-jnp.abs(x)))


def _params(sem):
    return pltpu.CompilerParams(dimension_semantics=sem, vmem_limit_bytes=VMEM_LIMIT_BYTES)


def _const_spec(shape):
    nd = len(shape)
    return pl.BlockSpec(shape, lambda *_: (0,) * nd, pipeline_mode=pl.Buffered(1))


def _inproj_kernel(x_ref, g_ref, wb_ref, wf_ref, wg_ref, wgt_ref, bgr_ref, bgc_ref,
                   pb_ref, pf_ref, gcol_ref, grow_ref):
    xn = _rms(x_ref[...], g_ref[...]).astype(BF16)
    for c in range(0, wb_ref.shape[1], 512):
        p = _dot(xn, wb_ref[:, c:c + 512])
        if M_W <= c < 2 * M_W:
            p = p * (M_DH ** -0.5)
        pb_ref[:, c:c + 512] = p.astype(BF16)
    for c in range(0, wf_ref.shape[1], 512):
        pf_ref[:, c:c + 512] = _dot(xn, wf_ref[:, c:c + 512])
    gc = _dot(xn, wg_ref[...]) + bgr_ref[...]
    lane = lax.broadcasted_iota(jnp.int32, gc.shape, 1)
    gcol_ref[...] = jnp.where(lane >= M_HEADS, _log_sigmoid(gc), gc)
    gr = _dot_nt(wgt_ref[...], xn) + bgc_ref[...]
    row = lax.broadcasted_iota(jnp.int32, gr.shape, 0)
    grow_ref[...] = jnp.where(row >= M_HEADS, _log_sigmoid(gr), gr)[0:SUBLANES, :]


def _inproj(x, g, wb, wf, wg, wgt, bgr, bgc):
    n = x.shape[0]
    tm = TOK_TILE
    return pl.pallas_call(
        _inproj_kernel,
        grid=(n // tm,),
        in_specs=[
            pl.BlockSpec((tm, D_MODEL), lambda i: (i, 0)),
            _const_spec((1, D_MODEL)),
            _const_spec(wb.shape), _const_spec(wf.shape), _const_spec(wg.shape),
            _const_spec(wgt.shape), _const_spec(bgr.shape), _const_spec(bgc.shape),
        ],
        out_specs=[
            pl.BlockSpec((tm, wb.shape[1]), lambda i: (i, 0)),
            pl.BlockSpec((tm, wf.shape[1]), lambda i: (i, 0)),
            pl.BlockSpec((tm, LANES), lambda i: (i, 0)),
            pl.BlockSpec((SUBLANES, tm), lambda i: (0, i)),
        ],
        out_shape=[
            jax.ShapeDtypeStruct((n, wb.shape[1]), BF16),
            jax.ShapeDtypeStruct((n, wf.shape[1]), F32),
            jax.ShapeDtypeStruct((n, LANES), F32),
            jax.ShapeDtypeStruct((SUBLANES, n), F32),
        ],
        compiler_params=_params(("parallel",)),
        name="inproj",
    )(x, g, wb, wf, wg, wgt, bgr, bgc)


def _mlstm_head_out(hh, gnorm, o):
    hn = hh * lax.rsqrt(jnp.mean(hh * hh, axis=-1, keepdims=True) + EPS) * gnorm
    return (hn * _sigmoid(o)).astype(BF16)


def _mlstm_prompt_kernel(q_ref, k_ref, v_ref, o_ref, gcol_ref, grow_ref, gn_ref,
                         hm_ref, c_out_ref, n_out_ref, m_out_ref, c_s, n_s, m_s):
    c = pl.program_id(1)
    L = MLSTM_L

    @pl.when(c == 0)
    def _():
        c_s[...] = jnp.zeros_like(c_s)
        n_s[...] = jnp.zeros_like(n_s)
        m_s[...] = jnp.zeros_like(m_s)

    row = lax.broadcasted_iota(jnp.int32, (L, L), 0)
    col = lax.broadcasted_iota(jnp.int32, (L, L), 1)
    causal = col <= row
    tril = jnp.where(causal, 1.0, 0.0).astype(F32)
    triu = jnp.where(row <= col, 1.0, 0.0).astype(F32)
    gcol = gcol_ref[...]
    grow = grow_ref[...]
    bc_all = _dot_exact(tril, gcol)
    br_all = _dot_exact(grow, triu)

    heads = range(M_HEADS)
    sl = [slice(h * M_DH, (h + 1) * M_DH) for h in heads]
    q = [q_ref[:, sl[h]] for h in heads]
    k = [k_ref[:, sl[h]] for h in heads]
    v = [v_ref[:, sl[h]] for h in heads]
    li_r = [grow[h:h + 1, :] for h in heads]
    li_c = [gcol[:, h:h + 1] for h in heads]
    b_c = [bc_all[:, M_HEADS + h:M_HEADS + h + 1] for h in heads]
    b_r = [br_all[M_HEADS + h:M_HEADS + h + 1, :] for h in heads]
    m_prev = [m_s[h:h + 1, 0:1] for h in heads]
    cmat = [c_s[h] for h in heads]
    nrow = [n_s[h:h + 1, :] for h in heads]
    qk = [_dot_nt(q[h], k[h]) for h in heads]
    qc = [_dot_nt(q[h], cmat[h].astype(BF16)) for h in heads]
    log_d = [jnp.where(causal, b_c[h] - b_r[h] + li_r[h], -jnp.inf) for h in heads]
    log_inter = [b_c[h] + m_prev[h] for h in heads]
    m_t = [jnp.maximum(log_inter[h], jnp.max(log_d[h], axis=1, keepdims=True)) for h in heads]
    dm = [jnp.exp(log_d[h] - m_t[h]) for h in heads]
    inter = [jnp.exp(log_inter[h] - m_t[h]) for h in heads]
    s = [qk[h] * dm[h] for h in heads]
    num = [_dot(s[h].astype(BF16), v[h]) + inter[h] * qc[h] for h in heads]
    qn = [jnp.sum(q[h].astype(F32) * nrow[h], axis=1, keepdims=True) for h in heads]
    den = [jnp.sum(s[h], axis=1, keepdims=True) + inter[h] * qn[h] for h in heads]
    hh = [num[h] / jnp.maximum(jnp.abs(den[h]), jnp.exp(-m_t[h])) for h in heads]
    for h in heads:
        hm_ref[:, sl[h]] = _mlstm_head_out(hh[h], gn_ref[:, sl[h]], o_ref[:, sl[h]])

    m_new = [m_t[h][L - 1:L, :] for h in heads]
    b_last = [b_c[h][L - 1:L, :] for h in heads]
    w_r = [jnp.exp(b_last[h] - b_r[h] + li_r[h] - m_new[h]) for h in heads]
    w_c = [jnp.exp(b_last[h] - b_c[h] + li_c[h] - m_new[h]) for h in heads]
    decay = [jnp.exp(b_last[h] + m_prev[h] - m_new[h]) for h in heads]
    vw = [(v[h].astype(F32) * w_c[h]).astype(BF16) for h in heads]
    upd = [_dot_tn(vw[h], k[h]) for h in heads]
    wk = [_dot(jnp.broadcast_to(w_r[h], (SUBLANES, L)).astype(BF16), k[h]) for h in heads]
    for h in heads:
        c_s[h] = decay[h] * cmat[h] + upd[h]
        n_s[h:h + 1, :] = decay[h] * nrow[h] + wk[h][0:1, :]
        m_s[h:h + 1, :] = jnp.broadcast_to(m_new[h], (1, LANES))

    @pl.when(c == pl.num_programs(1) - 1)
    def _():
        c_out_ref[0] = c_s[...]
        n_out_ref[0] = n_s[0:M_HEADS, :]
        m_out_ref[0] = m_s[...]


def _mlstm_prompt(pb, pf, gcol, grow, gnorm, batch, seq):
    L = MLSTM_L
    nc = seq // L
    n = batch * seq
    rows = lambda b, c: b * nc + c
    return pl.pallas_call(
        _mlstm_prompt_kernel,
        grid=(batch, nc),
        in_specs=[
            pl.BlockSpec((L, M_W), lambda b, c: (rows(b, c), 0)),
            pl.BlockSpec((L, M_W), lambda b, c: (rows(b, c), 1)),
            pl.BlockSpec((L, M_W), lambda b, c: (rows(b, c), 2)),
            pl.BlockSpec((L, M_W), lambda b, c: (rows(b, c), 0)),
            pl.BlockSpec((L, LANES), lambda b, c: (rows(b, c), 0)),
            pl.BlockSpec((SUBLANES, L), lambda b, c: (0, rows(b, c))),
            _const_spec((1, M_W)),
        ],
        out_specs=[
            pl.BlockSpec((L, M_W), lambda b, c: (rows(b, c), 0)),
            pl.BlockSpec((1, M_HEADS, M_DH, M_DH), lambda b, c: (b, 0, 0, 0)),
            pl.BlockSpec((1, M_HEADS, M_DH), lambda b, c: (b, 0, 0)),
            pl.BlockSpec((1, SUBLANES, LANES), lambda b, c: (b, 0, 0)),
        ],
        out_shape=[
            jax.ShapeDtypeStruct((n, M_W), BF16),
            jax.ShapeDtypeStruct((batch, M_HEADS, M_DH, M_DH), F32),
            jax.ShapeDtypeStruct((batch, M_HEADS, M_DH), F32),
            jax.ShapeDtypeStruct((batch, SUBLANES, LANES), F32),
        ],
        scratch_shapes=[
            pltpu.VMEM((M_HEADS, M_DH, M_DH), F32),
            pltpu.VMEM((SUBLANES, M_DH), F32),
            pltpu.VMEM((SUBLANES, LANES), F32),
        ],
        compiler_params=_params(("parallel", "arbitrary")),
        name="mlstm_prompt",
    )(pb, pb, pb, pf, gcol, grow, gnorm)


def _seq_index(token, seq):
    assert seq & (seq - 1) == 0
    return token >> (seq.bit_length() - 1)


def _to_row(colvec, eye):
    return jnp.sum(jnp.where(eye, colvec, 0.0), axis=0, keepdims=True)


def _mlstm_sample_kernel(q_ref, k_ref, v_ref, o_ref, gcol_ref, mtok_ref, gn_ref, c_in_ref, n_in_ref,
                         hm_ref, c_out_ref, n_out_ref, mtok_out_ref, *, seq):
    R = GROUP_B * seq
    row = lax.broadcasted_iota(jnp.int32, (R, R), 0)
    col = lax.broadcasted_iota(jnp.int32, (R, R), 1)
    eye = row == col
    same = _seq_index(row, seq) == _seq_index(col, seq)
    causal = same & (col <= row)
    rblk = _seq_index(lax.broadcasted_iota(jnp.int32, (R, 1), 0), seq)
    lane = lax.broadcasted_iota(jnp.int32, (R, LANES), 1)
    gcol = gcol_ref[...]
    mtok = mtok_ref[...]
    m_out = jnp.zeros((R, LANES), F32)

    def per_block(vals):
        out = jnp.zeros((R, 1), F32)
        for j in range(GROUP_B):
            out = jnp.where(rblk == j, vals[j], out)
        return out

    heads = range(M_HEADS)
    group = range(GROUP_B)
    last = [j * seq + seq - 1 for j in group]
    sl = [slice(h * M_DH, (h + 1) * M_DH) for h in heads]
    q = [q_ref[:, sl[h]] for h in heads]
    k = [k_ref[:, sl[h]] for h in heads]
    v = [v_ref[:, sl[h]] for h in heads]
    qk = [_dot_nt(q[h], k[h]) for h in heads]
    qc_jh = [[_dot_nt(q[h], c_in_ref[j, h].astype(BF16)) for j in group] for h in heads]
    li_c = [gcol[:, h:h + 1] for h in heads]
    lf_c = [gcol[:, M_HEADS + h:M_HEADS + h + 1] for h in heads]
    li_r = [_to_row(li_c[h], eye) for h in heads]
    lf_r = [_to_row(lf_c[h], eye) for h in heads]
    m_prev = [mtok[:, h:h + 1] for h in heads]
    b_c = [jnp.sum(jnp.where(causal, lf_r[h], 0.0), axis=1, keepdims=True) for h in heads]
    b_r = [_to_row(b_c[h], eye) for h in heads]
    log_d = [jnp.where(causal, b_c[h] - b_r[h] + li_r[h], -jnp.inf) for h in heads]
    log_inter = [b_c[h] + m_prev[h] for h in heads]
    m_t = [jnp.maximum(log_inter[h], jnp.max(log_d[h], axis=1, keepdims=True)) for h in heads]
    dm = [jnp.exp(log_d[h] - m_t[h]) for h in heads]
    inter = [jnp.exp(log_inter[h] - m_t[h]) for h in heads]
    s = [qk[h] * dm[h] for h in heads]
    qc, ntok = [], []
    for h in heads:
        qc_h = jnp.zeros((R, M_DH), F32)
        ntok_h = jnp.zeros((R, M_DH), F32)
        for j in group:
            qc_h = jnp.where(rblk == j, qc_jh[h][j], qc_h)
            ntok_h = jnp.where(rblk == j, n_in_ref[j, h:h + 1, :], ntok_h)
        qc.append(qc_h)
        ntok.append(ntok_h)
    num = [_dot(s[h].astype(BF16), v[h]) + inter[h] * qc[h] for h in heads]
    qn = [jnp.sum(q[h].astype(F32) * ntok[h], axis=1, keepdims=True) for h in heads]
    den = [jnp.sum(s[h], axis=1, keepdims=True) + inter[h] * qn[h] for h in heads]
    hh = [num[h] / jnp.maximum(jnp.abs(den[h]), jnp.exp(-m_t[h])) for h in heads]
    for h in heads:
        hm_ref[:, sl[h]] = _mlstm_head_out(hh[h], gn_ref[:, sl[h]], o_ref[:, sl[h]])

    m_new = [per_block([m_t[h][r:r + 1, :] for r in last]) for h in heads]
    b_last = [per_block([b_c[h][r:r + 1, :] for r in last]) for h in heads]
    w_c = [jnp.exp(b_last[h] - b_c[h] + li_c[h] - m_new[h]) for h in heads]
    decay_c = [jnp.exp(b_last[h] + m_prev[h] - m_new[h]) for h in heads]
    vw = [v[h].astype(F32) * w_c[h] for h in heads]
    wk = [w_c[h] * k[h].astype(F32) for h in heads]
    upd = [[_dot_tn(jnp.where(rblk == j, vw[h], 0.0).astype(BF16), k[h]) for j in group] for h in heads]
    for h in heads:
        for j in group:
            decay = decay_c[h][last[j]:last[j] + 1, :]
            c_out_ref[j, h] = decay * c_in_ref[j, h] + upd[h][j]
            n_out_ref[j, h:h + 1, :] = (decay * n_in_ref[j, h:h + 1, :]
                                        + jnp.sum(jnp.where(rblk == j, wk[h], 0.0), axis=0, keepdims=True))
        m_out = jnp.where(lane == h, m_new[h], m_out)
    mtok_out_ref[...] = m_out


def _mlstm_sample(pb, pf, gcol, mtok, gnorm, state_c, state_n, seq):
    batch = state_c.shape[0]
    R = GROUP_B * seq
    n = batch * seq
    return pl.pallas_call(
        functools.partial(_mlstm_sample_kernel, seq=seq),
        grid=(batch // GROUP_B,),
        in_specs=[
            pl.BlockSpec((R, M_W), lambda i: (i, 0)),
            pl.BlockSpec((R, M_W), lambda i: (i, 1)),
            pl.BlockSpec((R, M_W), lambda i: (i, 2)),
            pl.BlockSpec((R, M_W), lambda i: (i, 0)),
            pl.BlockSpec((R, LANES), lambda i: (i, 0)),
            pl.BlockSpec((R, LANES), lambda i: (i, 0)),
            _const_spec((1, M_W)),
            pl.BlockSpec((GROUP_B, M_HEADS, M_DH, M_DH), lambda i: (i, 0, 0, 0)),
            pl.BlockSpec((GROUP_B, M_HEADS, M_DH), lambda i: (i, 0, 0)),
        ],
        out_specs=[
            pl.BlockSpec((R, M_W), lambda i: (i, 0)),
            pl.BlockSpec((GROUP_B, M_HEADS, M_DH, M_DH), lambda i: (i, 0, 0, 0)),
            pl.BlockSpec((GROUP_B, M_HEADS, M_DH), lambda i: (i, 0, 0)),
            pl.BlockSpec((R, LANES), lambda i: (i, 0)),
        ],
        out_shape=[
            jax.ShapeDtypeStruct((n, M_W), BF16),
            jax.ShapeDtypeStruct(state_c.shape, F32),
            jax.ShapeDtypeStruct(state_n.shape, F32),
            jax.ShapeDtypeStruct((n, LANES), F32),
        ],
        compiler_params=_params(("parallel",)),
        name="mlstm_sample",
    )(pb, pb, pb, pf, gcol, mtok, gnorm, state_c, state_n)


def _conv_out(y, lg, lb):
    mu = jnp.mean(y, axis=-1, keepdims=True)
    yc = y - mu
    var = jnp.mean(yc * yc, axis=-1, keepdims=True)
    z = yc * lax.rsqrt(var + EPS) * lg + lb
    return (z * _sigmoid(z)).astype(BF16)


def _conv_prompt_kernel(a_ref, g_ref, w_ref, b_ref, lg_ref, lb_ref, yc_ref, new_ref, xp_s, xsh_s):
    t = pl.program_id(1)
    T = CONV_T

    @pl.when(t == 0)
    def _():
        xp_s[0:CONV_OFF, :] = jnp.zeros((CONV_OFF, C_W), F32)

    xp_s[CONV_OFF:CONV_OFF + T, :] = a_ref[...] * _sigmoid(g_ref[...])
    base = CONV_OFF - CONV_PAST
    acc = jnp.zeros((T, C_W), F32) + b_ref[...]
    for b in range(SUBLANES):
        taps = [j for j in range(b, CONV_K, SUBLANES)]
        span = taps[-1] - b + T
        xsh_s[b, 0:span, :] = xp_s[base + b:base + b + span, :]
        for j in taps:
            acc = acc + w_ref[j:j + 1, :] * xsh_s[b, j - b:j - b + T, :]
    yc_ref[...] = _conv_out(acc, lg_ref[...], lb_ref[...])
    tail = xp_s[CONV_OFF + T - CONV_PAST:CONV_OFF + T, :]
    xp_s[base:CONV_OFF, :] = tail

    @pl.when(t == pl.num_programs(1) - 1)
    def _():
        new_ref[0] = tail


def _conv_prompt(pf, w, b, lg, lb, batch, seq):
    T = CONV_T
    nt = seq // T
    n = batch * seq
    return pl.pallas_call(
        _conv_prompt_kernel,
        grid=(batch, nt),
        in_specs=[
            pl.BlockSpec((T, C_W), lambda bi, t: (bi * nt + t, 2)),
            pl.BlockSpec((T, C_W), lambda bi, t: (bi * nt + t, 3)),
            _const_spec(w.shape), _const_spec((1, C_W)), _const_spec((1, C_W)), _const_spec((1, C_W)),
        ],
        out_specs=[
            pl.BlockSpec((T, C_W), lambda bi, t: (bi * nt + t, 0)),
            pl.BlockSpec((1, CONV_PAST, C_W), lambda bi, t: (bi, 0, 0)),
        ],
        out_shape=[
            jax.ShapeDtypeStruct((n, C_W), BF16),
            jax.ShapeDtypeStruct((batch, CONV_PAST, C_W), F32),
        ],
        scratch_shapes=[pltpu.VMEM((CONV_OFF + T, C_W), F32),
                        pltpu.VMEM((SUBLANES, CONV_OFF + T, C_W), F32)],
        compiler_params=_params(("parallel", "arbitrary")),
        name="conv_prompt",
    )(pf, pf, w, b, lg, lb)


def _conv_sample_kernel(a_ref, g_ref, past_ref, w_ref, b_ref, lg_ref, lb_ref, yc_ref, new_ref,
                        xp_s, y_s, *, seq, gb):
    u = a_ref[...] * _sigmoid(g_ref[...])
    for j in range(gb):
        xp_s[0:CONV_PAST, :] = past_ref[j]
        xp_s[CONV_PAST:CONV_PAST + seq, :] = u[j * seq:(j + 1) * seq, :]
        acc = jnp.zeros((seq, C_W), F32) + b_ref[...]
        for i in range(CONV_K):
            acc = acc + w_ref[i:i + 1, :] * xp_s[i:i + seq, :]
        y_s[j * seq:(j + 1) * seq, :] = acc
        new_ref[j] = xp_s[seq:seq + CONV_PAST, :]
    yc_ref[...] = _conv_out(y_s[...], lg_ref[...], lb_ref[...])


def _conv_sample(pf, past, w, b, lg, lb, seq):
    batch = past.shape[0]
    gb = 8
    R = gb * seq
    n = batch * seq
    return pl.pallas_call(
        functools.partial(_conv_sample_kernel, seq=seq, gb=gb),
        grid=(batch // gb,),
        in_specs=[
            pl.BlockSpec((R, C_W), lambda i: (i, 2)),
            pl.BlockSpec((R, C_W), lambda i: (i, 3)),
            pl.BlockSpec((gb, CONV_PAST, C_W), lambda i: (i, 0, 0)),
            _const_spec(w.shape), _const_spec((1, C_W)), _const_spec((1, C_W)), _const_spec((1, C_W)),
        ],
        out_specs=[
            pl.BlockSpec((R, C_W), lambda i: (i, 0)),
            pl.BlockSpec((gb, CONV_PAST, C_W), lambda i: (i, 0, 0)),
        ],
        out_shape=[
            jax.ShapeDtypeStruct((n, C_W), BF16),
            jax.ShapeDtypeStruct(past.shape, F32),
        ],
        scratch_shapes=[pltpu.VMEM((CONV_PAST + seq + 6, C_W), F32), pltpu.VMEM((R, C_W), F32)],
        compiler_params=_params(("parallel",)),
        name="conv_sample",
    )(pf, pf, past, w, b, lg, lb)


def _memkv_kernel(x_ref, g_ref, w_ref, k_ref, v_ref):
    xn = _rms(x_ref[...], g_ref[...]).astype(BF16)
    k_ref[...] = _dot(xn, w_ref[:, 0:X_W])
    v_ref[...] = _dot(xn, w_ref[:, X_W:2 * X_W])


def _memkv(mem, g, w):
    n = mem.shape[0]
    tm = TOK_TILE
    return pl.pallas_call(
        _memkv_kernel,
        grid=(n // tm,),
        in_specs=[pl.BlockSpec((tm, D_MODEL), lambda i: (i, 0)), _const_spec((1, D_MODEL)),
                  _const_spec(w.shape)],
        out_specs=[pl.BlockSpec((tm, X_W), lambda i: (i, 0)), pl.BlockSpec((tm, X_W), lambda i: (i, 0))],
        out_shape=[jax.ShapeDtypeStruct((n, X_W), F32), jax.ShapeDtypeStruct((n, X_W), F32)],
        compiler_params=_params(("parallel",)),
        name="memkv",
    )(mem, g, w)


def _xattn_prompt_kernel(q_ref, k_ref, v_ref, o_ref):
    heads = range(X_HEADS)
    sl = [slice(h * X_DH, (h + 1) * X_DH) for h in heads]
    s = [_dot_nt(q_ref[:, sl[h]], k_ref[:, sl[h]].astype(BF16)) * (X_DH ** -0.5) for h in heads]
    p = [jnp.exp(s[h] - jnp.max(s[h], axis=-1, keepdims=True)) for h in heads]
    p = [p[h] / jnp.sum(p[h], axis=-1, keepdims=True) for h in heads]
    for h in heads:
        o_ref[:, sl[h]] = _dot(p[h].astype(BF16), v_ref[:, sl[h]].astype(BF16)).astype(BF16)


def _xattn_prompt(pb, k, v, batch, seq):
    tq = TOK_TILE
    nt = seq // tq
    qblk = 3 * M_W // X_W
    return pl.pallas_call(
        _xattn_prompt_kernel,
        grid=(batch, nt),
        in_specs=[
            pl.BlockSpec((tq, X_W), lambda b, t: (b * nt + t, qblk)),
            pl.BlockSpec((N_MEM, X_W), lambda b, t: (b, 0)),
            pl.BlockSpec((N_MEM, X_W), lambda b, t: (b, 0)),
        ],
        out_specs=pl.BlockSpec((tq, X_W), lambda b, t: (b * nt + t, 0)),
        out_shape=jax.ShapeDtypeStruct((batch * seq, X_W), BF16),
        compiler_params=_params(("parallel", "parallel")),
        name="xattn_prompt",
    )(pb, k, v)


def _xattn_sample_kernel(q_ref, k_ref, v_ref, o_ref, *, seq):
    R = GROUP_B * seq
    assert R == 16 and X_HEADS == 4
    rblk = _seq_index(lax.broadcasted_iota(jnp.int32, (R, 1), 0), seq)
    nk = N_MEM * X_HEADS
    row_head = lax.broadcasted_iota(jnp.int32, (X_HEADS * R, nk), 0) >> 4
    col_head = lax.broadcasted_iota(jnp.int32, (X_HEADS * R, nk), 1) & (X_HEADS - 1)
    own = row_head == col_head
    q4 = jnp.concatenate([q_ref[:, h * X_DH:(h + 1) * X_DH] for h in range(X_HEADS)], axis=0)
    outs = [jnp.zeros((R, X_DH), F32) for _ in range(X_HEADS)]
    group = range(GROUP_B)
    s = [_dot_nt(q4, k_ref[j].astype(BF16)) * (X_DH ** -0.5) for j in group]
    s = [jnp.where(own, s[j], -jnp.inf) for j in group]
    p = [jnp.exp(s[j] - jnp.max(s[j], axis=-1, keepdims=True)) for j in group]
    p = [p[j] / jnp.sum(p[j], axis=-1, keepdims=True) for j in group]
    o4s = [_dot(p[j].astype(BF16), v_ref[j].astype(BF16)) for j in group]
    for j in group:
        o4 = o4s[j]
        outs = [jnp.where(rblk == j, o4[h * R:(h + 1) * R, :], outs[h]) for h in range(X_HEADS)]
    for h in range(X_HEADS):
        o_ref[:, h * X_DH:(h + 1) * X_DH] = outs[h].astype(BF16)


def _xattn_sample(pb, k, v, seq):
    batch = k.shape[0]
    R = GROUP_B * seq
    qblk = 3 * M_W // X_W
    return pl.pallas_call(
        functools.partial(_xattn_sample_kernel, seq=seq),
        grid=(batch // GROUP_B,),
        in_specs=[
            pl.BlockSpec((R, X_W), lambda i: (i, qblk)),
            pl.BlockSpec((GROUP_B, N_MEM * X_HEADS, X_DH), lambda i: (i, 0, 0)),
            pl.BlockSpec((GROUP_B, N_MEM * X_HEADS, X_DH), lambda i: (i, 0, 0)),
        ],
        out_specs=pl.BlockSpec((R, X_W), lambda i: (i, 0)),
        out_shape=jax.ShapeDtypeStruct((batch * seq, X_W), BF16),
        compiler_params=_params(("parallel",)),
        name="xattn_sample",
    )(pb, k, v)


def _outproj_kernel(x_ref, hm_ref, yc_ref, ox_ref, w_ref, g2_ref, wq_ref, x1_ref, xnt_ref, q_ref):
    acc = x_ref[...]
    acc = acc + _dot(hm_ref[...], w_ref[0:M_W, :])
    acc = acc + _dot(yc_ref[...], w_ref[M_W:M_W + C_W, :])
    acc = acc + _dot(ox_ref[...], w_ref[M_W + C_W:, :])
    x1_ref[...] = acc
    xn = _rms(acc, g2_ref[...])
    xnt_ref[...] = jnp.transpose(xn).astype(BF16)
    q_ref[...] = _dot(xn.astype(BF16), wq_ref[...]).astype(BF16)


def _outproj(x, hm, yc, ox, w, g2, wq):
    n = x.shape[0]
    tm = TOK_TILE
    return pl.pallas_call(
        _outproj_kernel,
        grid=(n // tm,),
        in_specs=[
            pl.BlockSpec((tm, D_MODEL), lambda i: (i, 0)),
            pl.BlockSpec((tm, M_W), lambda i: (i, 0)),
            pl.BlockSpec((tm, C_W), lambda i: (i, 0)),
            pl.BlockSpec((tm, X_W), lambda i: (i, 0)),
            _const_spec(w.shape), _const_spec((1, D_MODEL)), _const_spec(wq.shape),
        ],
        out_specs=[
            pl.BlockSpec((tm, D_MODEL), lambda i: (i, 0)),
            pl.BlockSpec((D_MODEL, tm), lambda i: (0, i)),
            pl.BlockSpec((tm, wq.shape[1]), lambda i: (i, 0)),
        ],
        out_shape=[
            jax.ShapeDtypeStruct((n, D_MODEL), F32),
            jax.ShapeDtypeStruct((D_MODEL, n), BF16),
            jax.ShapeDtypeStruct((n, wq.shape[1]), BF16),
        ],
        compiler_params=_params(("parallel",)),
        name="outproj",
    )(x, hm, yc, ox, w, g2, wq)


def _sort16_pairs():
    n, pairs, p = P_TOPK, [], 1
    while p < n:
        k = p
        while k >= 1:
            for j in range(k % p, n - k, 2 * k):
                for i in range(min(k, n - j - k)):
                    if (i + j) // (2 * p) == (i + j + k) // (2 * p):
                        pairs.append((i + j, i + j + k))
            k //= 2
        p *= 2
    return pairs


_SORT16 = _sort16_pairs()
_BITONIC16 = [(i, i + d) for d in (8, 4, 2, 1) for i in range(P_TOPK) if not i & d]
_CAND_ROW_LEN = [P_TOPK // (p + 1) for p in range(P_TOPK)]


def _exchange(xs, pairs):
    for a, b in pairs:
        hi = jnp.maximum(xs[a], xs[b])
        lo = jnp.minimum(xs[a], xs[b])
        xs[a], xs[b] = hi, lo
    return xs


def _merge_top16(xs, ys):
    xs = list(xs)
    for r, y in enumerate(ys):
        xs[P_TOPK - 1 - r] = jnp.maximum(xs[P_TOPK - 1 - r], y)
    return _exchange(xs, _BITONIC16)


def _top16_sorted(st):
    t = st.shape[1]
    x3 = st.reshape(P_NKEYS // SUBLANES, SUBLANES, t)
    xs = _exchange([x3[g] for g in range(P_TOPK)], _SORT16)
    for shift in (4, 2, 1):
        xs = _merge_top16(xs, [pltpu.roll(x, shift, 0) for x in xs])
    return xs


def _theta_and_z(a, b):
    top = [a[0] + b[q] for q in range(P_TOPK)]
    for p in range(1, P_TOPK):
        top = _merge_top16(top, [a[p] + b[q] for q in range(_CAND_ROW_LEN[p])])
    z = jnp.ones_like(top[0])
    for r in range(1, P_TOPK):
        z = z + jnp.exp(top[r] - top[0])
    return top[P_TOPK - 1], z


def _peer_kernel(x_ref, xnt_ref, q_ref, gf_ref, sk_ref, u_ref, vt_ref, y_ref,
                 s1_s, s2_s, f2_s, th_s, a0_s, rz_s, acc_s, wg_s, stage_s):
    e = pl.program_id(1)
    tm = x_ref.shape[0]
    nsb = u_ref.shape[0] // P_NKEYS

    def prologue():
        for h in range(P_HEADS):
            for c, s_s in enumerate((s1_s, s2_s)):
                r0 = (2 * h + c) * P_NKEYS
                st = _dot_nt(sk_ref[2 * h + c], q_ref[:, r0:r0 + P_NKEYS])
                for tc in range(tm // LANES):
                    s_s[h, tc] = st[:, tc * LANES:(tc + 1) * LANES]
        sub = lax.broadcasted_iota(jnp.int32, (SUBLANES, LANES), 0)
        for tc in range(tm // LANES):
            ls = slice(tc * LANES, (tc + 1) * LANES)
            tops = []
            for s_s in (s1_s, s2_s):
                top = [jnp.zeros((SUBLANES, LANES), F32) for _ in range(P_TOPK)]
                for h in range(P_HEADS):
                    srt = _top16_sorted(s_s[h, tc])
                    top = [jnp.where(sub == h, srt[r], top[r]) for r in range(P_TOPK)]
                tops.append(top)
            theta, z = _theta_and_z(tops[0], tops[1])
            th_s[:, ls] = theta
            a0_s[:, ls] = tops[0][0]
            rz_s[:, ls] = 0.5 / z
            b0 = tops[1][0]
            for h in range(P_HEADS):
                f2_s[h, tc] = jnp.exp(s2_s[h, tc] - b0[h:h + 1, :])
        acc_s[...] = jnp.zeros_like(acc_s)

    def gate_times_gelu(at, sb):
        i = e * nsb + sb
        rows = slice(sb * P_NKEYS, (sb + 1) * P_NKEYS)
        for tc in range(tm // LANES):
            ls = slice(tc * LANES, (tc + 1) * LANES)
            w = jnp.zeros((P_NKEYS, LANES), F32)
            for h in range(P_HEADS):
                s1row = s1_s[h, tc, pl.ds(i, 1), :]
                f1row = jnp.exp(s1row - a0_s[h:h + 1, ls]) * rz_s[h:h + 1, ls]
                hit = (s2_s[h, tc] + s1row) >= th_s[h:h + 1, ls]
                w = w + jnp.where(hit, f2_s[h, tc], 0.0) * f1row
            a = at[rows, ls]
            inner = a * (0.7978845608028654 + 0.035677408136300125 * (a * a))
            stage_s[:, ls] = w * (a + a * jnp.tanh(inner))
        wg_s[rows, :] = stage_s[...].astype(BF16)

    @pl.when(e == 0)
    def _():
        prologue()

    at = _dot(u_ref[...], xnt_ref[...])
    for sb in range(nsb):
        gate_times_gelu(at, sb)
    acc_s[...] += _dot(vt_ref[...], wg_s[...])

    @pl.when(e == pl.num_programs(1) - 1)
    def _():
        y = x_ref[...] + jnp.transpose(acc_s[...])
        y_ref[...] = _rms(y, gf_ref[...])


def _peer(x1, xnt, q, gf, sk, u, vt, tm):
    n = x1.shape[0]
    te = PEER_TE
    chunked = (P_HEADS, tm // LANES, P_NKEYS, LANES)
    once = pl.Buffered(1)
    return pl.pallas_call(
        _peer_kernel,
        grid=(n // tm, P_NEXP // te),
        in_specs=[
            pl.BlockSpec((tm, D_MODEL), lambda i, e: (i, 0), pipeline_mode=once),
            pl.BlockSpec((D_MODEL, tm), lambda i, e: (0, i)),
            pl.BlockSpec((tm, q.shape[1]), lambda i, e: (i, 0)),
            _const_spec((1, D_MODEL)), _const_spec(sk.shape),
            pl.BlockSpec((te, D_MODEL), lambda i, e: (e, 0)),
            pl.BlockSpec((D_MODEL, te), lambda i, e: (0, e)),
        ],
        out_specs=pl.BlockSpec((tm, D_MODEL), lambda i, e: (i, 0), pipeline_mode=once),
        out_shape=jax.ShapeDtypeStruct((n, D_MODEL), F32),
        scratch_shapes=[
            pltpu.VMEM(chunked, F32),
            pltpu.VMEM(chunked, F32),
            pltpu.VMEM(chunked, F32),
            pltpu.VMEM((SUBLANES, tm), F32),
            pltpu.VMEM((SUBLANES, tm), F32),
            pltpu.VMEM((SUBLANES, tm), F32),
            pltpu.VMEM((D_MODEL, tm), F32),
            pltpu.VMEM((te, tm), BF16),
            pltpu.VMEM((P_NKEYS, tm), F32),
        ],
        compiler_params=_params(("parallel", "arbitrary")),
        name="peer",
    )(x1, xnt, q, gf, sk, u, vt)


def _win_split_kernel(w_ref, wb_ref, wf_ref):
    a0 = 4 * M_W + 2 * M_HEADS
    qx0 = a0 + 2 * C_W
    wb_ref[:, 0:3 * M_W] = w_ref[:, 0:3 * M_W].astype(BF16)
    wb_ref[:, 3 * M_W:] = w_ref[:, qx0:qx0 + X_W].astype(BF16)
    wf_ref[:, 0:M_W] = w_ref[:, 3 * M_W:4 * M_W].astype(BF16)
    wf_ref[:, M_W:] = w_ref[:, a0:a0 + 2 * C_W].astype(BF16)


def _win_split(w):
    r, c = w.shape
    tr = 256
    return pl.pallas_call(
        _win_split_kernel,
        grid=(r // tr,),
        in_specs=[pl.BlockSpec((tr, c), lambda i: (i, 0))],
        out_specs=[pl.BlockSpec((tr, 3 * M_W + X_W), lambda i: (i, 0)),
                   pl.BlockSpec((tr, M_W + 2 * C_W), lambda i: (i, 0))],
        out_shape=[jax.ShapeDtypeStruct((r, 3 * M_W + X_W), BF16),
                   jax.ShapeDtypeStruct((r, M_W + 2 * C_W), BF16)],
        compiler_params=_params(("parallel",)),
        name="win_split",
    )(w)


def _transpose_cast_kernel(x_ref, o_ref):
    o_ref[...] = jnp.transpose(x_ref[...]).astype(BF16)


def _transpose_cast(x):
    r, c = x.shape
    tr = PEER_TE
    return pl.pallas_call(
        _transpose_cast_kernel,
        grid=(r // tr,),
        in_specs=[pl.BlockSpec((tr, c), lambda i: (i, 0))],
        out_specs=pl.BlockSpec((c, tr), lambda i: (0, i)),
        out_shape=jax.ShapeDtypeStruct((c, r), BF16),
        compiler_params=_params(("parallel",)),
        name="transpose_cast",
    )(x)


def kernel(x_prompt, x_sample, mem_prompt, cache_mem_k, cache_mem_v, state_C, state_n, state_m, state_conv,
           norm1_g, w_in, b_gate, mlstm_norm_g, conv_w, conv_b, conv_ln_g, conv_ln_b, mem_norm_g, w_mk, w_mv,
           w_out, norm2_g, peer_wq, peer_subkeys, peer_u, peer_v, final_g):
    depth = w_in.shape[0]
    assert depth == 1
    bp, seq_p, _ = x_prompt.shape
    bs, seq_s, _ = x_sample.shape
    l = 0

    w = w_in[l]
    g0 = 4 * M_W
    wb, wf = _win_split(w)
    wgate = w[:, g0:g0 + 2 * M_HEADS]
    wg = jnp.pad(wgate.astype(BF16), ((0, 0), (0, LANES - 2 * M_HEADS)))
    wgt = jnp.pad(wgate.T.astype(BF16), ((0, 2 * SUBLANES - 2 * M_HEADS), (0, 0)))
    bgr = jnp.pad(b_gate[l][None, :], ((0, 0), (0, LANES - 2 * M_HEADS)))
    bgc = jnp.pad(b_gate[l][:, None], ((0, 2 * SUBLANES - 2 * M_HEADS), (0, 0)))
    g1 = norm1_g[l][None, :]
    gnorm = mlstm_norm_g[l].reshape(1, M_W)
    cw = jnp.pad(conv_w[l], ((0, 1), (0, 0)))
    cb, clg, clb = conv_b[l][None, :], conv_ln_g[l][None, :], conv_ln_b[l][None, :]
    wkv = jnp.concatenate([w_mk[l], w_mv[l]], axis=1).astype(BF16)
    wo = w_out[l].astype(BF16)
    wq = peer_wq[l].astype(BF16)
    sk = peer_subkeys[l].reshape(2 * P_HEADS, P_NKEYS, P_NKEYS).astype(BF16)
    u = peer_u[l].astype(BF16)
    vt = _transpose_cast(peer_v[l])
    g2 = norm2_g[l][None, :]
    gf = final_g[None, :]

    xp = x_prompt.reshape(bp * seq_p, D_MODEL)
    xs = x_sample.reshape(bs * seq_s, D_MODEL)

    pb, pf, gcol, grow = _inproj(xp, g1, wb, wf, wg, wgt, bgr, bgc)
    hm, c_p, n_p, m_p = _mlstm_prompt(pb, pf, gcol, grow, gnorm, bp, seq_p)
    yc, conv_p = _conv_prompt(pf, cw, cb, clg, clb, bp, seq_p)
    mk, mv = _memkv(mem_prompt.reshape(bp * N_MEM, D_MODEL), mem_norm_g[l][None, :], wkv)
    ox = _xattn_prompt(pb, mk, mv, bp, seq_p)
    x1, xnt, q = _outproj(xp, hm, yc, ox, wo, g2, wq)
    y_p = _peer(x1, xnt, q, gf, sk, u, vt, TOK_TILE)

    pb, pf, gcol, _ = _inproj(xs, g1, wb, wf, wg, wgt, bgr, bgc)
    mtok = jnp.pad(jnp.repeat(state_m[l], seq_s, axis=0), ((0, 0), (0, LANES - M_HEADS)))
    hm, c_s, n_s, mtok_new = _mlstm_sample(pb, pf, gcol, mtok, gnorm, state_C[l], state_n[l], seq_s)
    yc, conv_s = _conv_sample(pf, state_conv[l], cw, cb, clg, clb, seq_s)
    ox = _xattn_sample(pb, cache_mem_k.reshape(bs, N_MEM * X_HEADS, X_DH),
                       cache_mem_v.reshape(bs, N_MEM * X_HEADS, X_DH), seq_s)
    x1, xnt, q = _outproj(xs, hm, yc, ox, wo, g2, wq)
    y_s = _peer(x1, xnt, q, gf, sk, u, vt, TOK_TILE)

    m_s = mtok_new.reshape(bs, seq_s, LANES)[:, seq_s - 1, :M_HEADS]
    return (y_p.reshape(bp, seq_p, D_MODEL),
            y_s.reshape(bs, seq_s, D_MODEL),
            mk.reshape(1, bp, N_MEM, X_HEADS, X_DH),
            mv.reshape(1, bp, N_MEM, X_HEADS, X_DH),
            c_p[None], n_p[None], m_p[None, :, :M_HEADS, 0], conv_p[None],
            c_s[None], n_s[None], m_s[None], conv_s[None])
```
